```python
import jax, jax.numpy as jnp
from jax import lax
import numpy as np

D_MODEL = 4096
BATCH = 4
SEQ = 2048
DEPTH = 1

GRID_W = 64
CTX_LEN = 256
HG_HEADS = 16
HG_HEAD_DIM = 128
HG_WIDTH = HG_HEADS * HG_HEAD_DIM
POOL_WIDTH = D_MODEL - HG_WIDTH
POOL_WINDOWS = (2, 4, 8, 16)
POOL_GROUP = POOL_WIDTH // len(POOL_WINDOWS)
MIX_WIDTH = HG_WIDTH + POOL_WIDTH
IN_COLS = 5 * HG_WIDTH + POOL_WIDTH
CTX_STATE_COLS = 3 * HG_WIDTH
CHUNK = 64
N_EXPERTS = 16
EXPERT_FF = D_MODEL // 2
CAPACITY_FACTOR = 2
EPS = 1e-6

kernel_name = 'hybrid_hgrn2_pool_ec_flow_block'


def rms_norm(x, g):
    x32 = x.astype(jnp.float32)
    y = x32 * lax.rsqrt(jnp.mean(x32 * x32, axis=-1, keepdims=True) + EPS)
    return y * g.astype(jnp.float32)


def ada_mod(cvec, w, b):
    mod = jax.nn.silu(cvec.astype(jnp.float32)) @ w + b
    return jnp.split(mod, 6, axis=-1)


def modulate(h, shift, scale):
    return h * (1.0 + scale) + shift


def lower_bounds(lb_param):
    p = jax.nn.softmax(lb_param.astype(jnp.float32), axis=0)
    return jnp.cumsum(p, axis=0)


def forget_gate(f_raw, lb):
    f = lb + (1.0 - lb) * jax.nn.sigmoid(f_raw.astype(jnp.float32))
    return jnp.log(f), 1.0 - f


def to_heads(t):
    return t.reshape(t.shape[0], t.shape[1], HG_HEADS, HG_HEAD_DIM)


def rev(t):
    return jnp.flip(t, axis=1)


def gla_chunk_scan(q, k, v, logf, s0):
    B, L, H, _ = q.shape
    n = L // CHUNK

    def chunks(t):
        return t.reshape(B, n, CHUNK, H, t.shape[-1]).transpose(1, 0, 3, 2, 4)

    mask = jnp.tril(jnp.ones((CHUNK, CHUNK), dtype=bool))[:, :, None]

    def step(S, inp):
        qc, kc, vc, gc = inp
        b = jnp.cumsum(gc, axis=2)
        o_inter = jnp.einsum('bhtk,bhkv->bhtv', qc * jnp.exp(b), S)
        diff = b[:, :, :, None, :] - b[:, :, None, :, :]
        decay = jnp.where(mask, jnp.exp(jnp.where(mask, diff, 0.0)), 0.0)
        scores = jnp.einsum('bhtk,bhsk,bhtsk->bhts', qc, kc, decay)
        o_intra = jnp.einsum('bhts,bhsv->bhtv', scores, vc)
        b_end = b[:, :, -1:, :]
        S_new = (jnp.exp(b_end[:, :, 0, :])[..., None] * S
                 + jnp.einsum('bhsk,bhsv->bhkv', kc * jnp.exp(b_end - b), vc))
        return S_new, o_inter + o_intra

    S_fin, o = lax.scan(step, s0, (chunks(q), chunks(k), chunks(v), chunks(logf)))
    o = o.transpose(1, 0, 3, 2, 4).reshape(B, L, H, v.shape[-1])
    return o, S_fin


def final_state(k, v, logf):
    suffix = jnp.flip(jnp.cumsum(jnp.flip(logf, 1), axis=1), 1) - logf
    return jnp.einsum('blhk,blhv->bhkv', k * jnp.exp(suffix), v)


def hgrn2_gates(f_fwd, f_bwd, lb):
    logf_f, k_f = forget_gate(f_fwd, lb[0])
    logf_b, k_b = forget_gate(f_bwd, lb[1])
    return to_heads(logf_f), to_heads(k_f), to_heads(logf_b), to_heads(k_b)


def hgrn2_context_states(f_fwd, f_bwd, i, lb):
    logf_f, k_f, logf_b, k_b = hgrn2_gates(f_fwd, f_bwd, lb)
    i = to_heads(i)
    s_f = final_state(k_f, i, logf_f)
    s_b = final_state(rev(k_b), rev(i), rev(logf_b))
    return s_f, s_b


def hgrn2_mixer(q, f_fwd, f_bwd, i, g, lb, s0_f, s0_b, norm_g):
    logf_f, k_f, logf_b, k_b = hgrn2_gates(f_fwd, f_bwd, lb)
    q, i = to_heads(q), to_heads(i)
    o_f, s_f = gla_chunk_scan(q, k_f, i, logf_f, s0_f)
    o_b, s_b = gla_chunk_scan(rev(q), rev(k_b), rev(i), rev(logf_b), s0_b)
    o = (o_f + rev(o_b)).astype(jnp.float32)
    o = o * lax.rsqrt(jnp.mean(o * o, axis=-1, keepdims=True) + EPS)
    o = o.reshape(o.shape[0], o.shape[1], HG_WIDTH) * norm_g * jax.nn.silu(g.astype(jnp.float32))
    return o, s_f, s_b


def box_bounds(n, w):
    idx = jnp.arange(n)
    start = idx - w // 2
    return jnp.clip(start, 0, n), jnp.clip(start + w, 0, n)


def grid_window_mean(v, w):
    B, L, C = v.shape
    rows = L // GRID_W
    g = v.reshape(B, rows, GRID_W, C)
    P = jnp.pad(jnp.cumsum(jnp.cumsum(g, axis=1), axis=2), ((0, 0), (1, 0), (1, 0), (0, 0)))
    r0, r1 = box_bounds(rows, w)
    c0, c1 = box_bounds(GRID_W, w)

    def corner(r, cidx):
        return jnp.take(jnp.take(P, r, axis=1), cidx, axis=2)

    s = corner(r1, c1) - corner(r0, c1) - corner(r1, c0) + corner(r0, c0)
    cnt = ((r1 - r0)[:, None] * (c1 - c0)[None, :]).astype(jnp.float32)
    return (s / cnt[None, :, :, None]).reshape(B, L, C)


def seq_window_mean(v, w):
    B, L, C = v.shape
    P = jnp.pad(jnp.cumsum(v, axis=1), ((0, 0), (1, 0), (0, 0)))
    t0, t1 = box_bounds(L, w)
    s = jnp.take(P, t1, axis=1) - jnp.take(P, t0, axis=1)
    return s / (t1 - t0).astype(jnp.float32)[None, :, None]


def multiscale_pool(v, pool_w, pool_scale, on_grid):
    v32 = v.astype(jnp.float32)
    outs = []
    for gi, w in enumerate(POOL_WINDOWS):
        vg = v32[..., gi * POOL_GROUP:(gi + 1) * POOL_GROUP]
        mean = grid_window_mean(vg, w) if on_grid else seq_window_mean(vg, w)
        outs.append((mean - vg) @ pool_w[gi])
    return jnp.concatenate(outs, axis=-1) * pool_scale


def expert_choice_ffn(h, router_w, w1, w3, w2):
    B, L, D = h.shape
    cap = CAPACITY_FACTOR * L // N_EXPERTS
    aff = jax.nn.softmax((h @ router_w).astype(jnp.float32), axis=-1)
    gate, idx = lax.top_k(jnp.swapaxes(aff, 1, 2), cap)
    xg = jax.vmap(lambda hb, ib: hb[ib])(h, idx)
    hid = jax.nn.silu(jnp.einsum('becd,edf->becf', xg, w1)) * jnp.einsum('becd,edf->becf', xg, w3)
    y = jnp.einsum('becf,efd->becd', hid, w2) * gate[..., None]
    out = jax.vmap(lambda ib, yb: jnp.zeros((L, D), yb.dtype).at[ib.reshape(-1)].add(yb.reshape(-1, D)))(idx, y)
    return out


def split_in(p):
    H = HG_WIDTH
    return (p[..., :H], p[..., H:2 * H], p[..., 2 * H:3 * H], p[..., 3 * H:4 * H],
            p[..., 4 * H:5 * H], p[..., 5 * H:])


def setup_inputs(seed: int = 0) -> dict:
    key = jax.random.key(seed)
    ks = jax.random.split(key, 20)
    nrm = jax.random.normal
    D = D_MODEL
    return {
        'x': nrm(ks[0], (BATCH, SEQ, D), jnp.float32),
        'c': nrm(ks[1], (BATCH, D), jnp.float32),
        'ctx': nrm(ks[2], (BATCH, CTX_LEN, D), jnp.float32),
        'c_ctx': nrm(ks[3], (D,), jnp.float32),
        'ada_w': nrm(ks[4], (DEPTH, D, 6 * D), jnp.float32) * (0.5 * D ** -0.5),
        'ada_b': 0.02 * nrm(ks[5], (DEPTH, 6 * D), jnp.float32),
        'norm1_g': 1.0 + 0.05 * nrm(ks[6], (DEPTH, D), jnp.float32),
        'norm2_g': 1.0 + 0.05 * nrm(ks[7], (DEPTH, D), jnp.float32),
        'w_in': nrm(ks[8], (DEPTH, D, IN_COLS), jnp.float32) * D ** -0.5,
        'lb_param': nrm(ks[9], (DEPTH + 1, 2, HG_WIDTH), jnp.float32),
        'hg_norm_g': 1.0 + 0.05 * nrm(ks[10], (DEPTH, HG_WIDTH), jnp.float32),
        'pool_w': nrm(ks[11], (DEPTH, len(POOL_WINDOWS), POOL_GROUP, POOL_GROUP), jnp.float32) * POOL_GROUP ** -0.5,
        'pool_scale': 1.0 + 0.1 * nrm(ks[12], (DEPTH, POOL_WIDTH), jnp.float32),
        'w_out': nrm(ks[13], (DEPTH, MIX_WIDTH, D), jnp.float32) * MIX_WIDTH ** -0.5,
        'router_w': nrm(ks[14], (DEPTH, D, N_EXPERTS), jnp.float32) * D ** -0.5,
        'moe_w1': nrm(ks[15], (DEPTH, N_EXPERTS, D, EXPERT_FF), jnp.float32) * D ** -0.5,
        'moe_w3': nrm(ks[16], (DEPTH, N_EXPERTS, D, EXPERT_FF), jnp.float32) * D ** -0.5,
        'moe_w2': nrm(ks[17], (DEPTH, N_EXPERTS, EXPERT_FF, D), jnp.float32) * EXPERT_FF ** -0.5,
        'final_norm_g': 1.0 + 0.05 * nrm(ks[18], (D,), jnp.float32),
    }


def reference(x, c, ctx, c_ctx, ada_w, ada_b, norm1_g, norm2_g, w_in, lb_param, hg_norm_g,
              pool_w, pool_scale, w_out, router_w, moe_w1, moe_w3, moe_w2, final_norm_g):
    out_dtype = x.dtype
    lbs = lower_bounds(lb_param)
    zero_state = jnp.zeros((ctx.shape[0], HG_HEADS, HG_HEAD_DIM, HG_HEAD_DIM), jnp.float32)
    for layer in range(DEPTH):
        last = layer == DEPTH - 1
        sh1, sc1, g1, sh2, sc2, g2 = ada_mod(c[:, None, :], ada_w[layer], ada_b[layer])
        csh1, csc1, cg1, csh2, csc2, cg2 = ada_mod(c_ctx[None, None, :], ada_w[layer], ada_b[layer])

        hc = modulate(rms_norm(ctx, norm1_g[layer]), csh1, csc1)
        if last:
            pc = hc @ w_in[layer][:, :CTX_STATE_COLS]
            s_f, s_b = hgrn2_context_states(pc[..., :HG_WIDTH], pc[..., HG_WIDTH:2 * HG_WIDTH],
                                            pc[..., 2 * HG_WIDTH:], lbs[layer])
        else:
            cf_f, cf_b, ci, cq, cgo, cv = split_in(hc @ w_in[layer])
            oc, s_f, s_b = hgrn2_mixer(cq, cf_f, cf_b, ci, cgo, lbs[layer], zero_state, zero_state,
                                       hg_norm_g[layer])
            pcm = multiscale_pool(cv, pool_w[layer], pool_scale[layer], on_grid=False)
            ctx_next = ctx + cg1 * (jnp.concatenate([oc, pcm], axis=-1) @ w_out[layer])
            hc2 = modulate(rms_norm(ctx_next, norm2_g[layer]), csh2, csc2)
            ctx_next = ctx_next + cg2 * expert_choice_ffn(hc2, router_w[layer], moe_w1[layer],
                                                          moe_w3[layer], moe_w2[layer])

        h = modulate(rms_norm(x, norm1_g[layer]), sh1, sc1)
        f_f, f_b, i, q, go, v = split_in(h @ w_in[layer])
        o, _, _ = hgrn2_mixer(q, f_f, f_b, i, go, lbs[layer], s_f, s_b, hg_norm_g[layer])
        pm = multiscale_pool(v, pool_w[layer], pool_scale[layer], on_grid=True)
        x = x + g1 * (jnp.concatenate([o, pm], axis=-1) @ w_out[layer])
        h2 = modulate(rms_norm(x, norm2_g[layer]), sh2, sc2)
        x = x + g2 * expert_choice_ffn(h2, router_w[layer], moe_w1[layer], moe_w3[layer], moe_w2[layer])

        if not last:
            ctx = ctx_next
    return rms_norm(x, final_norm_g).astype(out_dtype)
```

```python
import functools

import jax
import jax.numpy as jnp
import numpy as np
from jax import lax
from jax.experimental import pallas as pl
from jax.experimental.pallas import tpu as pltpu

F32 = jnp.float32
BF16 = jnp.bfloat16
I32 = jnp.int32

D_MODEL = 4096
BATCH = 4
SEQ = 2048
GRID_W = 64
CTX_LEN = 256
HG_HEADS = 16
HG_HEAD_DIM = 128
HG_WIDTH = HG_HEADS * HG_HEAD_DIM
POOL_WIDTH = D_MODEL - HG_WIDTH
POOL_WINDOWS = (2, 4, 8, 16)
POOL_GROUP = POOL_WIDTH // len(POOL_WINDOWS)
N_EXPERTS = 16
EXPERT_FF = D_MODEL // 2
CAPACITY = 2 * SEQ // N_EXPERTS
EPS = 1e-6

V7X_VMEM_BYTES = 64 * 1024 * 1024
V7X_VMEM_HEADROOM_BYTES = 10 * 1024 * 1024
NORM_ROWS = 16
ADA_ROWS = 8

SCAN_CHUNK = 64
SCAN_LEVELS = 6

_NT = (((1,), (1,)), ((), ()))
_TN = (((0,), (0,)), ((), ()))


def _dot(a, b):
    return jnp.dot(a, b, preferred_element_type=F32)


def _dot_nt(a, b):
    return lax.dot_general(a, b, _NT, preferred_element_type=F32)


def _dot_tn(a, b):
    return lax.dot_general(a, b, _TN, preferred_element_type=F32)


def _split_f32(x):
    hi = x.astype(BF16).astype(F32)
    return hi, x - hi


def _params(vmem_bytes, semantics):
    limit = min(int(vmem_bytes) + V7X_VMEM_HEADROOM_BYTES, V7X_VMEM_BYTES)
    return pltpu.CompilerParams(dimension_semantics=semantics, vmem_limit_bytes=limit)


def _ada_kernel(c_ref, w_ref, b_ref, o_ref):
    c = c_ref[...]
    s = c * jax.nn.sigmoid(c)
    hi, lo = _split_f32(s)
    lhs = jnp.concatenate([hi, lo], axis=0).astype(BF16)
    r = _dot(lhs, w_ref[...].astype(BF16))
    o_ref[...] = r[:ADA_ROWS] + r[ADA_ROWS:] + b_ref[...]


def _ada_mod(cvecs, w, b):
    d, n = w.shape
    tn = 512
    vmem = 2 * d * tn * 4 + d * tn * 2 + 4 * ADA_ROWS * d * 4
    return pl.pallas_call(
        _ada_kernel,
        grid=(n // tn,),
        in_specs=[pl.BlockSpec((ADA_ROWS, d), lambda j: (0, 0)),
                  pl.BlockSpec((d, tn), lambda j: (0, j)),
                  pl.BlockSpec((1, tn), lambda j: (0, j))],
        out_specs=pl.BlockSpec((ADA_ROWS, tn), lambda j: (0, j)),
        out_shape=jax.ShapeDtypeStruct((ADA_ROWS, n), F32),
        compiler_params=_params(vmem, ("arbitrary",)),
        name="ada_mod",
    )(cvecs, w, b)


def _modulated_norm(x, g, scale, shift):
    ms = jnp.mean(x * x, axis=-1, keepdims=True)
    return (x * lax.rsqrt(ms + EPS) * g) * (1.0 + scale) + shift


def _in_proj_kernel(x_ref, g_ref, sc_ref, sh_ref, lbp_ref, w_ref, logf_ref, k_ref, r_ref, h_scr, *, n_gate):
    j = pl.program_id(1)

    @pl.when(j == 0)
    def _():
        g, sc, sh = g_ref[...], sc_ref[...], sh_ref[...]

        def norm_rows(r, carry):
            rows = pl.ds(pl.multiple_of(r * NORM_ROWS, NORM_ROWS), NORM_ROWS)
            h_scr[rows, :] = _modulated_norm(x_ref[rows, :], g, sc, sh).astype(BF16)
            return carry

        lax.fori_loop(0, x_ref.shape[0] // NORM_ROWS, norm_rows, 0)

    z = _dot(h_scr[...], w_ref[...])

    @pl.when(j < n_gate)
    def _():
        p = lbp_ref[...]
        e = jnp.exp(p - jnp.max(p, axis=0, keepdims=True))
        lb = e[0:1] / jnp.sum(e, axis=0, keepdims=True)
        f = lb + (1.0 - lb) * jax.nn.sigmoid(z)
        logf_ref[...] = jnp.log(f)
        k_ref[...] = (1.0 - f).astype(BF16)

    @pl.when(j >= n_gate)
    def _():
        r_ref[...] = z.astype(BF16)


def _in_proj(x2d, g, sc, sh, lbp, w_bf16, *, rows_per_sample, tm, n_cols):
    n, d = x2d.shape
    tn = 1024
    n_gate = 2 * HG_WIDTH // tn
    n_tiles = n_cols // tn
    tiles_per_sample = rows_per_sample // tm
    gate_idx = lambda i, j: (i, jnp.minimum(j, n_gate - 1))
    vmem = 2 * tm * d * 4 + tm * d * 2 + 2 * d * tn * 2 + 2 * tm * tn * (4 + 2 + 2) + 2 * tm * tn * 4
    return pl.pallas_call(
        functools.partial(_in_proj_kernel, n_gate=n_gate),
        grid=(n // tm, n_tiles),
        in_specs=[pl.BlockSpec((tm, d), lambda i, j: (i, 0)),
                  pl.BlockSpec((1, d), lambda i, j: (0, 0)),
                  pl.BlockSpec((None, 1, d), lambda i, j: (i // tiles_per_sample, 0, 0)),
                  pl.BlockSpec((None, 1, d), lambda i, j: (i // tiles_per_sample, 0, 0)),
                  pl.BlockSpec((2, tn), lambda i, j: (0, jnp.minimum(j, n_gate - 1))),
                  pl.BlockSpec((d, tn), lambda i, j: (0, j))],
        out_specs=[pl.BlockSpec((tm, tn), gate_idx),
                   pl.BlockSpec((tm, tn), gate_idx),
                   pl.BlockSpec((tm, tn), lambda i, j: (i, jnp.maximum(j - n_gate, 0)))],
        out_shape=[jax.ShapeDtypeStruct((n, 2 * HG_WIDTH), F32),
                   jax.ShapeDtypeStruct((n, 2 * HG_WIDTH), BF16),
                   jax.ShapeDtypeStruct((n, n_cols - 2 * HG_WIDTH), BF16)],
        scratch_shapes=[pltpu.VMEM((tm, d), BF16)],
        compiler_params=_params(vmem, ("arbitrary", "arbitrary")),
        name="in_proj",
    )(x2d, g, sc, sh, lbp, w_bf16)


def _ctx_tri():
    s = np.arange(CTX_LEN)[:, None]
    u = np.arange(CTX_LEN)[None, :]
    tri = np.stack([u > s, u < s]).astype(np.float32)
    return jnp.asarray(np.concatenate([tri, tri], axis=2), dtype=BF16)


def _ctx_state_kernel(lff_ref, lfb_ref, kf_ref, kb_ref, i_ref, tri_ref, sf_ref, sb_ref):
    v = i_ref[...]

    def state(lf, k, a):
        hi, lo = _split_f32(lf)
        g = _dot(a, jnp.concatenate([hi, lo], axis=0).astype(BF16))
        kd = (k.astype(F32) * jnp.exp(g)).astype(BF16)
        return _dot_tn(v, kd)

    sf_ref[...] = state(lff_ref[...], kf_ref[...], tri_ref[0])
    sb_ref[...] = state(lfb_ref[...], kb_ref[...], tri_ref[1])


def _ctx_states(logf, kk, vi):
    t, dh, nh = CTX_LEN, HG_HEAD_DIM, HG_HEADS
    blk = lambda off: pl.BlockSpec((t, dh), lambda b, h: (b, off + h))
    st = pl.BlockSpec((None, None, dh, dh), lambda b, h: (b, h, 0, 0))
    shape = jax.ShapeDtypeStruct((BATCH, nh, dh, dh), F32)
    vmem = 2 * (2 * t * dh * 4 + 3 * t * dh * 2 + 2 * dh * dh * 4) + 2 * 2 * t * 2 * t * 2
    return pl.pallas_call(
        _ctx_state_kernel,
        grid=(BATCH, nh),
        in_specs=[blk(0), blk(nh), blk(0), blk(nh), blk(0),
                  pl.BlockSpec((2, t, 2 * t), lambda b, h: (0, 0, 0))],
        out_specs=[st, st],
        out_shape=[shape, shape],
        compiler_params=_params(vmem, ("arbitrary", "arbitrary")),
        name="ctx_state",
    )(logf, logf, kk, kk, vi, _ctx_tri())


def _scan_consts():
    c, nl = SCAN_CHUNK, SCAN_LEVELS
    t = np.arange(c)[:, None]
    u = np.arange(c)[None, :]
    a = np.zeros((2, nl + 2, c, c), np.float32)
    for l in range(nl):
        h = 1 << l
        mid = t - t % (2 * h) + h
        upper = t >= mid
        a[0, l] = np.where(upper, (u >= mid) & (u <= t), (u > t) & (u < mid))
        a[1, l] = np.where(upper, (u >= mid) & (u < t), (u >= t) & (u < mid))
    a[0, nl], a[0, nl + 1] = u <= t, u > t
    a[1, nl], a[1, nl + 1] = u >= t, u < t
    a = a.reshape(2, (nl + 2) * c, c)
    acat = np.concatenate([a, a], axis=2)
    x = t ^ u
    lev = np.floor(np.log2(np.maximum(x, 1))).astype(np.int32)
    lidx = np.stack([np.where(t > u, lev, np.where(t == u, nl, -1)),
                     np.where(t < u, lev, np.where(t == u, nl, -1))]).astype(np.int32)
    return jnp.asarray(acat, dtype=BF16), jnp.asarray(lidx)


def _scan_chunk(direction, lf, k, q, v, st_ref, acat, lidx, t_idx):
    c, nl = SCAN_CHUNK, SCAN_LEVELS
    hi, lo = _split_f32(lf)
    g = _dot(acat, jnp.concatenate([hi, lo], axis=0).astype(BF16))
    e = jnp.exp(g)
    qf = q.astype(F32)
    kf = k.astype(F32)
    scores = jnp.zeros((c, c), F32)
    for l in range(nl):
        bit = (t_idx & (1 << l)) != 0
        is_query = bit if direction == 0 else jnp.logical_not(bit)
        x = (e[l * c:(l + 1) * c] * jnp.where(is_query, qf, kf)).astype(BF16)
        scores = jnp.where(lidx == l, _dot_nt(x, x), scores)
    scores = jnp.where(lidx == nl, _dot_nt(q, k), scores)
    q_in = (e[nl * c:(nl + 1) * c] * qf).astype(BF16)
    k_st = (e[(nl + 1) * c:(nl + 2) * c] * kf).astype(BF16)
    st = st_ref[...]
    o = _dot(scores.astype(BF16), v) + _dot_nt(q_in, st.astype(BF16))
    tot_row = nl * c + (c - 1 if direction == 0 else 0)
    st_ref[...] = e[tot_row:tot_row + 1] * st + _dot_tn(v, k_st)
    return o


def _scan_kernel(lff_ref, lfb_ref, kf_ref, kb_ref, i_ref, q_ref, g_ref, s0f_ref, s0b_ref, ng_ref,
                 acat_ref, lidx_ref, o_ref, stf_ref, stb_ref, part_ref):
    c = SCAN_CHUNK
    n_chunks = SEQ // c
    stf_ref[...] = s0f_ref[...]
    stb_ref[...] = s0b_ref[...]
    t_idx = lax.broadcasted_iota(I32, (c, HG_HEAD_DIM), 0)
    ng = ng_ref[...]

    def chunk(direction, r0):
        rows = pl.ds(r0, c)
        lf_ref, k_ref, st_ref = (lff_ref, kf_ref, stf_ref) if direction == 0 else (lfb_ref, kb_ref, stb_ref)
        return _scan_chunk(direction, lf_ref[rows, :], k_ref[rows, :], q_ref[rows, :], i_ref[rows, :],
                           st_ref, acat_ref[direction], lidx_ref[direction], t_idx)

    def finalize(o, r0):
        rows = pl.ds(r0, c)
        gate = g_ref[rows, :].astype(F32)
        ms = jnp.mean(o * o, axis=-1, keepdims=True)
        o_ref[rows, :] = (o * lax.rsqrt(ms + EPS) * ng * (gate * jax.nn.sigmoid(gate))).astype(BF16)

    def starts(n):
        return pl.multiple_of(n * c, c), pl.multiple_of((n_chunks - 1 - n) * c, c)

    def first_half(n, carry):
        rf, rb = starts(n)
        part_ref[pl.ds(rf, c), :] = chunk(0, rf)
        part_ref[pl.ds(rb, c), :] = chunk(1, rb)
        return carry

    def second_half(n, carry):
        rf, rb = starts(n)
        finalize(chunk(0, rf) + part_ref[pl.ds(rf, c), :], rf)
        finalize(chunk(1, rb) + part_ref[pl.ds(rb, c), :], rb)
        return carry

    lax.fori_loop(0, n_chunks // 2, first_half, 0)
    lax.fori_loop(n_chunks // 2, n_chunks, second_half, 0)


def _hgrn2_scan(logf, kk, rest, s0f, s0b, norm_g):
    dh, nh = HG_HEAD_DIM, HG_HEADS
    acat, lidx = _scan_consts()
    blk = lambda off: pl.BlockSpec((SEQ, dh), lambda b, h: (b, off + h))
    st = pl.BlockSpec((None, None, dh, dh), lambda b, h: (b, h, 0, 0))
    vmem = (2 * (2 * SEQ * dh * 4 + 6 * SEQ * dh * 2 + 2 * dh * dh * 4) + SEQ * dh * 4 + 2 * dh * dh * 4
            + 2 * acat.size * 2 + 2 * 2 * SCAN_CHUNK * 128 * 4)
    return pl.pallas_call(
        _scan_kernel,
        grid=(BATCH, nh),
        in_specs=[blk(0), blk(nh), blk(0), blk(nh), blk(0), blk(nh), blk(2 * nh), st, st,
                  pl.BlockSpec((1, dh), lambda b, h: (0, h)),
                  pl.BlockSpec(acat.shape, lambda b, h: (0, 0, 0)),
                  pl.BlockSpec(lidx.shape, lambda b, h: (0, 0, 0))],
        out_specs=blk(0),
        out_shape=jax.ShapeDtypeStruct((BATCH * SEQ, HG_WIDTH), BF16),
        scratch_shapes=[pltpu.VMEM((dh, dh), F32), pltpu.VMEM((dh, dh), F32), pltpu.VMEM((SEQ, dh), F32)],
        compiler_params=_params(vmem, ("arbitrary", "arbitrary")),
        name="hgrn2_scan",
    )(logf, logf, kk, kk, rest, rest, rest, s0f, s0b, norm_g, acat, lidx)


POOL_PIECE = 256


def _box_bounds(n, w):
    start = np.arange(n) - w // 2
    return np.clip(start, 0, n), np.clip(start + w, 0, n)


def _pool_consts():
    rows = SEQ // GRID_W
    col_box = np.zeros((len(POOL_WINDOWS), POOL_PIECE, POOL_PIECE), np.float32)
    cnt = np.zeros((len(POOL_WINDOWS), SEQ, 1), np.float32)
    cc = np.arange(GRID_W)[None, :]
    for gi, w in enumerate(POOL_WINDOWS):
        c0, c1 = _box_bounds(GRID_W, w)
        r0, r1 = _box_bounds(rows, w)
        wc = ((cc >= c0[:, None]) & (cc < c1[:, None])).astype(np.float32)
        col_box[gi] = np.kron(np.eye(POOL_PIECE // GRID_W, dtype=np.float32), wc)
        cnt[gi] = ((r1 - r0)[:, None] * (c1 - c0)[None, :]).reshape(SEQ, 1)
    return jnp.asarray(col_box, dtype=BF16), jnp.asarray(cnt)


def _pool_kernel(v_ref, box_ref, cnt_ref, pw_ref, ps_ref, o_ref, pre_ref, diff_ref):
    gi = pl.program_id(1)
    rows = SEQ // GRID_W
    gw = GRID_W
    box = box_ref[...]
    pre_ref[0:gw, :] = jnp.zeros((gw, POOL_GROUP), F32)
    for p in range(SEQ // POOL_PIECE):
        yc = _dot(box, v_ref[p * POOL_PIECE:(p + 1) * POOL_PIECE, :])
        for rr in range(POOL_PIECE // gw):
            r = p * (POOL_PIECE // gw) + rr
            pre_ref[(r + 1) * gw:(r + 2) * gw, :] = pre_ref[r * gw:(r + 1) * gw, :] + yc[rr * gw:(rr + 1) * gw]
    for k, w in enumerate(POOL_WINDOWS):
        @pl.when(gi == k)
        def _(w=w):
            r0, r1 = _box_bounds(rows, w)
            for r in range(rows):
                sl = slice(r * gw, (r + 1) * gw)
                box_sum = pre_ref[int(r1[r]) * gw:(int(r1[r]) + 1) * gw, :] - pre_ref[int(r0[r]) * gw:(int(r0[r]) + 1) * gw, :]
                diff_ref[sl, :] = (box_sum / cnt_ref[sl, :] - v_ref[sl, :].astype(F32)).astype(BF16)
    o_ref[...] = (_dot(diff_ref[...], pw_ref[...].astype(BF16)) * ps_ref[...]).astype(BF16)


def _grid_pool(rest, pool_w, pool_scale):
    ng, pg = len(POOL_WINDOWS), POOL_GROUP
    col_box, cnt = _pool_consts()
    v_off = 3 * HG_WIDTH // pg
    vmem = (2 * (2 * SEQ * pg * 2 + POOL_PIECE * POOL_PIECE * 2 + SEQ * 128 * 4 + pg * pg * 4)
            + (SEQ + GRID_W) * pg * 4 + SEQ * pg * 2 + SEQ * pg * 4)
    return pl.pallas_call(
        _pool_kernel,
        grid=(BATCH, ng),
        in_specs=[pl.BlockSpec((SEQ, pg), lambda b, k: (b, v_off + k)),
                  pl.BlockSpec((None, POOL_PIECE, POOL_PIECE), lambda b, k: (k, 0, 0)),
                  pl.BlockSpec((None, SEQ, 1), lambda b, k: (k, 0, 0)),
                  pl.BlockSpec((None, pg, pg), lambda b, k: (k, 0, 0)),
                  pl.BlockSpec((1, pg), lambda b, k: (0, k))],
        out_specs=pl.BlockSpec((SEQ, pg), lambda b, k: (b, k)),
        out_shape=jax.ShapeDtypeStruct((BATCH * SEQ, POOL_WIDTH), BF16),
        scratch_shapes=[pltpu.VMEM((SEQ + GRID_W, pg), F32), pltpu.VMEM((SEQ, pg), BF16)],
        compiler_params=_params(vmem, ("arbitrary", "arbitrary")),
        name="grid_pool",
    )(rest, col_box, cnt, pool_w, pool_scale)


def _out_proj_kernel(o_ref, pm_ref, wa_ref, wb_ref, x_ref, g1_ref, out_ref):
    acc = _dot(o_ref[...], wa_ref[...]) + _dot(pm_ref[...], wb_ref[...])
    out_ref[...] = x_ref[...] + g1_ref[...] * acc


def _out_proj(o, pm, w_bf16, x2d, g1):
    n, d = x2d.shape
    tm, tn = 1024, 512
    half = w_bf16.shape[0] // 2
    tiles_per_sample = SEQ // tm
    vmem = 2 * (2 * tm * half * 2 + 2 * half * tn * 2 + 2 * tm * tn * 4) + tm * tn * 4
    return pl.pallas_call(
        _out_proj_kernel,
        grid=(n // tm, d // tn),
        in_specs=[pl.BlockSpec((tm, half), lambda i, j: (i, 0)),
                  pl.BlockSpec((tm, half), lambda i, j: (i, 0)),
                  pl.BlockSpec((half, tn), lambda i, j: (0, j)),
                  pl.BlockSpec((half, tn), lambda i, j: (1, j)),
                  pl.BlockSpec((tm, tn), lambda i, j: (i, j)),
                  pl.BlockSpec((None, 1, tn), lambda i, j: (i // tiles_per_sample, 0, j))],
        out_specs=pl.BlockSpec((tm, tn), lambda i, j: (i, j)),
        out_shape=jax.ShapeDtypeStruct((n, d), F32),
        compiler_params=_params(vmem, ("arbitrary", "arbitrary")),
        name="out_proj",
    )(o, pm, w_bf16, w_bf16, x2d, g1)


def _norm2_router_kernel(x_ref, g_ref, sc_ref, sh_ref, rw_ref, h2_ref, aff_ref):
    h = _modulated_norm(x_ref[...], g_ref[...], sc_ref[...], sh_ref[...])
    h2_ref[...] = h.astype(BF16)
    h_hi, h_lo = _split_f32(h)
    r_hi, r_lo = _split_f32(rw_ref[...])
    h_hi, h_lo, r_hi, r_lo = (a.astype(BF16) for a in (h_hi, h_lo, r_hi, r_lo))
    logits = _dot_nt(r_hi, h_hi) + (_dot_nt(r_hi, h_lo) + _dot_nt(r_lo, h_hi))
    e = jnp.exp(logits - jnp.max(logits, axis=0, keepdims=True))
    aff_ref[...] = e / jnp.sum(e, axis=0, keepdims=True)


def _norm2_router(x1, g, sc, sh, router_wt):
    n, d = x1.shape
    tm = 256
    tiles_per_sample = SEQ // tm
    vmem = 2 * (tm * d * 4 + tm * d * 2 + N_EXPERTS * d * 4) + 6 * tm * d * 4
    return pl.pallas_call(
        _norm2_router_kernel,
        grid=(n // tm,),
        in_specs=[pl.BlockSpec((tm, d), lambda i: (i, 0)),
                  pl.BlockSpec((1, d), lambda i: (0, 0)),
                  pl.BlockSpec((None, 1, d), lambda i: (i // tiles_per_sample, 0, 0)),
                  pl.BlockSpec((None, 1, d), lambda i: (i // tiles_per_sample, 0, 0)),
                  pl.BlockSpec((N_EXPERTS, d), lambda i: (0, 0))],
        out_specs=[pl.BlockSpec((tm, d), lambda i: (i, 0)),
                   pl.BlockSpec((None, N_EXPERTS, tm), lambda i: (i // tiles_per_sample, 0, i % tiles_per_sample))],
        out_shape=[jax.ShapeDtypeStruct((n, d), BF16),
                   jax.ShapeDtypeStruct((BATCH, N_EXPERTS, SEQ), F32)],
        compiler_params=_params(vmem, ("arbitrary",)),
        name="norm2_router",
    )(x1, g, sc, sh, router_wt)


ROUTE_COLS = 256
ROUTE_ROWS = 8


def _route_kernel(aff_ref, slot_ref, gate_ref):
    a = aff_ref[...]
    rows, n = a.shape
    thr = jnp.zeros((rows, 1), I32)
    for b in range(30, -1, -1):
        cand = thr | (1 << b)
        cnt = jnp.sum((a >= pltpu.bitcast(cand, F32)).astype(I32), axis=-1, keepdims=True)
        thr = jnp.where(cnt >= CAPACITY, cand, thr)
    gt = a >= pltpu.bitcast(thr + 1, F32)
    eq = jnp.logical_and(a >= pltpu.bitcast(thr, F32), jnp.logical_not(gt))
    need = CAPACITY - jnp.sum(gt.astype(I32), axis=-1, keepdims=True)

    def prefix_count(mask):
        m = jnp.where(mask, 1.0, 0.0).astype(BF16)
        r = lax.broadcasted_iota(I32, (n, ROUTE_COLS), 0)
        c = lax.broadcasted_iota(I32, (n, ROUTE_COLS), 1)
        parts = [_dot(m, jnp.where(r < c + cb * ROUTE_COLS, 1.0, 0.0).astype(BF16))
                 for cb in range(n // ROUTE_COLS)]
        return jnp.concatenate(parts, axis=1).astype(I32)

    sel = jnp.logical_or(gt, jnp.logical_and(eq, prefix_count(eq) < need))
    slot = jnp.where(sel, prefix_count(sel), -1)
    slot_ref[...] = slot

    a_hi, a_rest = _split_f32(a)
    a_mid, a_lo = _split_f32(a_rest)
    slot_ids = lax.broadcasted_iota(I32, (CAPACITY, n), 0)
    pad = jnp.zeros((ROUTE_ROWS - 3, n), F32)
    for r in range(rows):
        onehot = jnp.where(slot[r:r + 1] == slot_ids, 1.0, 0.0).astype(BF16)
        pieces = jnp.concatenate([a_hi[r:r + 1], a_mid[r:r + 1], a_lo[r:r + 1], pad], axis=0).astype(BF16)
        res = _dot_nt(pieces, onehot)
        gate_ref[r:r + 1, :] = res[0:1] + res[1:2] + res[2:3]


def _route(aff_rows):
    rows, n = aff_rows.shape
    vmem = 4 * ROUTE_ROWS * n * 4 + 4 * n * ROUTE_COLS * 4 + 4 * CAPACITY * n * 4
    return pl.pallas_call(
        _route_kernel,
        grid=(rows // ROUTE_ROWS,),
        in_specs=[pl.BlockSpec((ROUTE_ROWS, n), lambda i: (i, 0))],
        out_specs=[pl.BlockSpec((ROUTE_ROWS, n), lambda i: (i, 0)),
                   pl.BlockSpec((ROUTE_ROWS, CAPACITY), lambda i: (i, 0))],
        out_shape=[jax.ShapeDtypeStruct((rows, n), I32),
                   jax.ShapeDtypeStruct((rows, CAPACITY), F32)],
        compiler_params=_params(vmem, ("arbitrary",)),
        name="route",
    )(aff_rows)


GATHER_EXPERTS = 2


def _moe_gather_kernel(slot_ref, h2_ref, xg_ref):
    slots = slot_ref[...]
    slot_ids = lax.broadcasted_iota(I32, (CAPACITY, SEQ), 0)
    h2 = h2_ref[...]
    for j in range(GATHER_EXPERTS):
        onehot = jnp.where(slots[j:j + 1] == slot_ids, 1.0, 0.0).astype(BF16)
        xg_ref[j] = _dot(onehot, h2).astype(BF16)


def _moe_gather(slot_bel, h2):
    d = h2.shape[1]
    vmem = (SEQ * d * 2 + 2 * GATHER_EXPERTS * CAPACITY * d * 2 + 2 * N_EXPERTS * SEQ * 4
            + CAPACITY * SEQ * (4 + 2) + CAPACITY * d * (4 + 2))
    return pl.pallas_call(
        _moe_gather_kernel,
        grid=(BATCH, N_EXPERTS // GATHER_EXPERTS),
        in_specs=[pl.BlockSpec((None, None, GATHER_EXPERTS, SEQ), lambda b, g: (b, g, 0, 0)),
                  pl.BlockSpec((SEQ, d), lambda b, g: (b, 0), pipeline_mode=pl.Buffered(1))],
        out_specs=pl.BlockSpec((GATHER_EXPERTS, None, CAPACITY, d), lambda b, g: (g, b, 0, 0)),
        out_shape=jax.ShapeDtypeStruct((N_EXPERTS, BATCH, CAPACITY, d), BF16),
        compiler_params=_params(vmem, ("arbitrary", "arbitrary")),
        name="moe_gather",
    )(slot_bel.reshape(BATCH, N_EXPERTS // GATHER_EXPERTS, GATHER_EXPERTS, SEQ), h2)


def _moe_up_kernel(xg_ref, w1_ref, w3_ref, hid_ref):
    xg = xg_ref[...]
    a = _dot(xg, w1_ref[...].astype(BF16))
    b = _dot(xg, w3_ref[...].astype(BF16))
    hid_ref[...] = (a * jax.nn.sigmoid(a) * b).astype(BF16)


def _moe_up(xg, w1, w3):
    ne, m, d = xg.shape
    ff = w1.shape[2]
    tf = 256
    vmem = 2 * m * d * 2 + 2 * 2 * d * tf * 4 + 2 * d * tf * 2 + 3 * m * tf * 4 + 2 * m * tf * 2
    w_spec = pl.BlockSpec((None, d, tf), lambda e, f: (e, 0, f))
    return pl.pallas_call(
        _moe_up_kernel,
        grid=(ne, ff // tf),
        in_specs=[pl.BlockSpec((None, m, d), lambda e, f: (e, 0, 0)), w_spec, w_spec],
        out_specs=pl.BlockSpec((None, m, tf), lambda e, f: (e, 0, f)),
        out_shape=jax.ShapeDtypeStruct((ne, m, ff), BF16),
        compiler_params=_params(vmem, ("arbitrary", "arbitrary")),
        name="moe_up",
    )(xg, w1, w3)


def _moe_down_kernel(hid_ref, w2_ref, gate_ref, y_ref):
    y = _dot(hid_ref[...], w2_ref[...].astype(BF16))
    y_ref[...] = (y * gate_ref[...]).astype(BF16)


def _moe_down(hid, w2, gate_col):
    ne, m, ff = hid.shape
    d = w2.shape[2]
    tn = 1024
    vmem = 2 * m * ff * 2 + 2 * ff * tn * 4 + ff * tn * 2 + 2 * m * 128 * 4 + 2 * m * tn * 4 + 2 * m * tn * 2
    return pl.pallas_call(
        _moe_down_kernel,
        grid=(ne, d // tn),
        in_specs=[pl.BlockSpec((None, m, ff), lambda e, j: (e, 0, 0)),
                  pl.BlockSpec((None, ff, tn), lambda e, j: (e, 0, j)),
                  pl.BlockSpec((None, m, 1), lambda e, j: (e, 0, 0))],
        out_specs=pl.BlockSpec((None, m, tn), lambda e, j: (e, 0, j)),
        out_shape=jax.ShapeDtypeStruct((ne, m, d), BF16),
        compiler_params=_params(vmem, ("arbitrary", "arbitrary")),
        name="moe_down",
    )(hid, w2, gate_col)


COMBINE_TOKENS = 128


def _moe_combine_kernel(slot_ref, y_ref, x1_ref, g2_ref, fg_ref, out_ref):
    st = slot_ref[...]
    slot_ids = lax.broadcasted_iota(I32, (COMBINE_TOKENS, CAPACITY), 1)
    onehot = jnp.concatenate(
        [jnp.where(st[:, e:e + 1] == slot_ids, 1.0, 0.0).astype(BF16) for e in range(N_EXPERTS)], axis=1)
    y = y_ref[...].reshape(N_EXPERTS * CAPACITY, y_ref.shape[-1])
    x2 = x1_ref[...] + g2_ref[...] * _dot(onehot, y)
    ms = jnp.mean(x2 * x2, axis=-1, keepdims=True)
    out_ref[...] = x2 * lax.rsqrt(ms + EPS) * fg_ref[...]


def _moe_combine(slot_ble, y, x1, g2, final_g):
    n, d = x1.shape
    tm = COMBINE_TOKENS
    tiles_per_sample = SEQ // tm
    vmem = (N_EXPERTS * CAPACITY * d * 2 + 2 * 2 * tm * d * 4 + 2 * tm * 128 * 4
            + tm * N_EXPERTS * CAPACITY * 2 + 3 * tm * d * 4)
    return pl.pallas_call(
        _moe_combine_kernel,
        grid=(BATCH, tiles_per_sample),
        in_specs=[pl.BlockSpec((None, tm, N_EXPERTS), lambda b, t: (b, t, 0)),
                  pl.BlockSpec((N_EXPERTS, None, CAPACITY, d), lambda b, t: (0, b, 0, 0),
                               pipeline_mode=pl.Buffered(1)),
                  pl.BlockSpec((tm, d), lambda b, t: (b * tiles_per_sample + t, 0)),
                  pl.BlockSpec((None, 1, d), lambda b, t: (b, 0, 0)),
                  pl.BlockSpec((1, d), lambda b, t: (0, 0))],
        out_specs=pl.BlockSpec((tm, d), lambda b, t: (b * tiles_per_sample + t, 0)),
        out_shape=jax.ShapeDtypeStruct((n, d), F32),
        compiler_params=_params(vmem, ("arbitrary", "arbitrary")),
        name="moe_combine",
    )(slot_ble, y, x1, g2, final_g)


def kernel(x, c, ctx, c_ctx, ada_w, ada_b, norm1_g, norm2_g, w_in, lb_param, hg_norm_g, pool_w, pool_scale,
           w_out, router_w, moe_w1, moe_w3, moe_w2, final_norm_g):
    nb, seq, d = x.shape
    x2d = x.reshape(nb * seq, d)
    ctx2d = ctx.reshape(nb * CTX_LEN, d)

    cvecs = jnp.zeros((ADA_ROWS, d), F32).at[:nb].set(c).at[nb].set(c_ctx)
    mod = _ada_mod(cvecs, ada_w[0], ada_b).reshape(ADA_ROWS, 6, d)
    sh1, sc1, g1, sh2, sc2, g2 = (mod[:nb, k][:, None, :] for k in range(6))
    csh1, csc1 = mod[nb:nb + 1, 0][:, None, :], mod[nb:nb + 1, 1][:, None, :]

    w_in_bf = w_in[0].astype(BF16)
    w_out_bf = w_out[0].astype(BF16)
    lbp = lb_param.reshape(lb_param.shape[0], 2 * HG_WIDTH)

    logf_c, k_c, i_c = _in_proj(ctx2d, norm1_g, csc1, csh1, lbp, w_in_bf,
                                rows_per_sample=nb * CTX_LEN, tm=256, n_cols=3 * HG_WIDTH)
    s_f, s_b = _ctx_states(logf_c, k_c, i_c)

    logf, kk, rest = _in_proj(x2d, norm1_g, sc1, sh1, lbp, w_in_bf,
                              rows_per_sample=seq, tm=512, n_cols=w_in.shape[2])
    o = _hgrn2_scan(logf, kk, rest, s_f, s_b, hg_norm_g)
    pm = _grid_pool(rest, pool_w[0], pool_scale)
    x1 = _out_proj(o, pm, w_out_bf, x2d, g1)

    h2, aff_t = _norm2_router(x1, norm2_g, sc2, sh2, router_w[0].T)
    slot, gates = _route(aff_t.reshape(nb * N_EXPERTS, seq))
    slot = slot.reshape(nb, N_EXPERTS, seq)
    xg = _moe_gather(slot, h2).reshape(N_EXPERTS, nb * CAPACITY, d)
    hid = _moe_up(xg, moe_w1[0], moe_w3[0])
    gate_col = gates.reshape(nb, N_EXPERTS, CAPACITY).transpose(1, 0, 2).reshape(N_EXPERTS, nb * CAPACITY, 1)
    y = _moe_down(hid, moe_w2[0], gate_col).reshape(N_EXPERTS, nb, CAPACITY, d)
    out = _moe_combine(slot.transpose(0, 2, 1), y, x1, g2, final_norm_g[None, :])
    return out.reshape(nb, seq, d).astype(x.dtype)
```

```python
import functools

import jax
import jax.numpy as jnp
import numpy as np
from jax import lax
from jax.experimental import pallas as pl
from jax.experimental.pallas import tpu as pltpu

F32 = jnp.float32
BF16 = jnp.bfloat16
I32 = jnp.int32

D_MODEL = 4096
BATCH = 4
SEQ = 2048
GRID_W = 64
CTX_LEN = 256
HG_HEADS = 16
HG_HEAD_DIM = 128
HG_WIDTH = HG_HEADS * HG_HEAD_DIM
POOL_WIDTH = D_MODEL - HG_WIDTH
POOL_WINDOWS = (2, 4, 8, 16)
POOL_GROUP = POOL_WIDTH // len(POOL_WINDOWS)
N_EXPERTS = 16
EXPERT_FF = D_MODEL // 2
CAPACITY = 2 * SEQ // N_EXPERTS
EPS = 1e-6

V7X_VMEM_BYTES = 64 * 1024 * 1024
V7X_VMEM_HEADROOM_BYTES = 10 * 1024 * 1024
V7X_SUBLANES = 8
V7X_LANES = 128
NORM_ROWS = 16
ADA_ROWS = 8

SCAN_CHUNK = 64
SCAN_LEVELS = 6
SCAN_UNROLL = 4

_NT = (((1,), (1,)), ((), ()))
_TN = (((0,), (0,)), ((), ()))


def _dot(a, b):
    return jnp.dot(a, b, preferred_element_type=F32)


def _dot_nt(a, b):
    return lax.dot_general(a, b, _NT, preferred_element_type=F32)


def _dot_tn(a, b):
    return lax.dot_general(a, b, _TN, preferred_element_type=F32)


def _split_f32(x):
    hi = x.astype(BF16).astype(F32)
    return hi, x - hi


def _params(vmem_bytes, semantics):
    limit = min(int(vmem_bytes) + V7X_VMEM_HEADROOM_BYTES, V7X_VMEM_BYTES)
    return pltpu.CompilerParams(dimension_semantics=semantics, vmem_limit_bytes=limit)


def _ada_kernel(c_ref, w_ref, b_ref, o_ref):
    c = c_ref[...]
    s = c * jax.nn.sigmoid(c)
    hi, lo = _split_f32(s)
    lhs = jnp.concatenate([hi, lo], axis=0).astype(BF16)
    r = _dot(lhs, w_ref[...].astype(BF16))
    o_ref[...] = r[:ADA_ROWS] + r[ADA_ROWS:] + b_ref[...]


def _ada_mod(cvecs, w, b):
    d, n = w.shape
    tn = 512
    vmem = 2 * d * tn * 4 + d * tn * 2 + 4 * ADA_ROWS * d * 4
    return pl.pallas_call(
        _ada_kernel,
        grid=(n // tn,),
        in_specs=[pl.BlockSpec((ADA_ROWS, d), lambda j: (0, 0)),
                  pl.BlockSpec((d, tn), lambda j: (0, j)),
                  pl.BlockSpec((1, tn), lambda j: (0, j))],
        out_specs=pl.BlockSpec((ADA_ROWS, tn), lambda j: (0, j)),
        out_shape=jax.ShapeDtypeStruct((ADA_ROWS, n), F32),
        compiler_params=_params(vmem, ("arbitrary",)),
        name="ada_mod",
    )(cvecs, w, b)


def _modulated_norm(x, g, scale, shift):
    ms = jnp.mean(x * x, axis=-1, keepdims=True)
    return (x * lax.rsqrt(ms + EPS) * g) * (1.0 + scale) + shift


def _in_proj_kernel(x_ref, g_ref, sc_ref, sh_ref, lbp_ref, w_ref, f_ref, k_ref, r_ref, h_scr, *, n_gate):
    j = pl.program_id(1)

    @pl.when(j == 0)
    def _():
        g, sc, sh = g_ref[...], sc_ref[...], sh_ref[...]

        def norm_rows(r, carry):
            rows = pl.ds(pl.multiple_of(r * NORM_ROWS, NORM_ROWS), NORM_ROWS)
            h_scr[rows, :] = _modulated_norm(x_ref[rows, :], g, sc, sh).astype(BF16)
            return carry

        lax.fori_loop(0, x_ref.shape[0] // NORM_ROWS, norm_rows, 0)

    z = _dot(h_scr[...], w_ref[...])

    @pl.when(j < n_gate)
    def _():
        p = lbp_ref[...]
        e = jnp.exp(p - jnp.max(p, axis=0, keepdims=True))
        lb = e[0:1] / jnp.sum(e, axis=0, keepdims=True)
        f = lb + (1.0 - lb) * jax.nn.sigmoid(z)
        f_ref[...] = f
        k_ref[...] = (1.0 - f).astype(BF16)

    @pl.when(j >= n_gate)
    def _():
        r_ref[...] = z.astype(BF16)


def _in_proj(x2d, g, sc, sh, lbp, w_bf16, *, rows_per_sample, tm, n_cols):
    n, d = x2d.shape
    tn = 1024
    n_gate = 2 * HG_WIDTH // tn
    n_tiles = n_cols // tn
    tiles_per_sample = rows_per_sample // tm
    gate_idx = lambda i, j: (i, jnp.minimum(j, n_gate - 1))
    vmem = 2 * tm * d * 4 + tm * d * 2 + 2 * d * tn * 2 + 2 * tm * tn * (4 + 2 + 2) + 2 * tm * tn * 4
    return pl.pallas_call(
        functools.partial(_in_proj_kernel, n_gate=n_gate),
        grid=(n // tm, n_tiles),
        in_specs=[pl.BlockSpec((tm, d), lambda i, j: (i, 0)),
                  pl.BlockSpec((1, d), lambda i, j: (0, 0)),
                  pl.BlockSpec((None, 1, d), lambda i, j: (i // tiles_per_sample, 0, 0)),
                  pl.BlockSpec((None, 1, d), lambda i, j: (i // tiles_per_sample, 0, 0)),
                  pl.BlockSpec((2, tn), lambda i, j: (0, jnp.minimum(j, n_gate - 1))),
                  pl.BlockSpec((d, tn), lambda i, j: (0, j))],
        out_specs=[pl.BlockSpec((tm, tn), gate_idx),
                   pl.BlockSpec((tm, tn), gate_idx),
                   pl.BlockSpec((tm, tn), lambda i, j: (i, jnp.maximum(j - n_gate, 0)))],
        out_shape=[jax.ShapeDtypeStruct((n, 2 * HG_WIDTH), F32),
                   jax.ShapeDtypeStruct((n, 2 * HG_WIDTH), BF16),
                   jax.ShapeDtypeStruct((n, n_cols - 2 * HG_WIDTH), BF16)],
        scratch_shapes=[pltpu.VMEM((tm, d), BF16)],
        compiler_params=_params(vmem, ("arbitrary", "arbitrary")),
        name="in_proj",
    )(x2d, g, sc, sh, lbp, w_bf16)


def _ctx_tri():
    s = np.arange(CTX_LEN)[:, None]
    u = np.arange(CTX_LEN)[None, :]
    tri = np.stack([u > s, u < s]).astype(np.float32)
    return jnp.asarray(np.concatenate([tri, tri], axis=2), dtype=BF16)


def _ctx_state_kernel(ff_ref, fb_ref, kf_ref, kb_ref, i_ref, tri_ref, sf_ref, sb_ref):
    v = i_ref[...]

    def state(f, k, a):
        hi, lo = _split_f32(jnp.log(f))
        g = _dot(a, jnp.concatenate([hi, lo], axis=0).astype(BF16))
        kd = (k.astype(F32) * jnp.exp(g)).astype(BF16)
        return _dot_tn(v, kd)

    sf_ref[...] = state(ff_ref[...], kf_ref[...], tri_ref[0])
    sb_ref[...] = state(fb_ref[...], kb_ref[...], tri_ref[1])


def _ctx_states(fdec, kk, vi):
    t, dh, nh = CTX_LEN, HG_HEAD_DIM, HG_HEADS
    blk = lambda off: pl.BlockSpec((t, dh), lambda b, h: (b, off + h))
    st = pl.BlockSpec((None, None, dh, dh), lambda b, h: (b, h, 0, 0))
    shape = jax.ShapeDtypeStruct((BATCH, nh, dh, dh), F32)
    vmem = 2 * (2 * t * dh * 4 + 3 * t * dh * 2 + 2 * dh * dh * 4) + 2 * 2 * t * 2 * t * 2
    return pl.pallas_call(
        _ctx_state_kernel,
        grid=(BATCH, nh),
        in_specs=[blk(0), blk(nh), blk(0), blk(nh), blk(0),
                  pl.BlockSpec((2, t, 2 * t), lambda b, h: (0, 0, 0))],
        out_specs=[st, st],
        out_shape=[shape, shape],
        compiler_params=_params(vmem, ("arbitrary", "arbitrary")),
        name="ctx_state",
    )(fdec, fdec, kk, kk, vi, _ctx_tri())


def _pair_levels():
    c, nl = SCAN_CHUNK, SCAN_LEVELS
    t = np.arange(c)[:, None]
    u = np.arange(c)[None, :]
    lev = np.floor(np.log2(np.maximum(t ^ u, 1))).astype(np.int32)
    lidx = np.stack([np.where(t > u, lev, np.where(t == u, nl, -1)),
                     np.where(t < u, lev, np.where(t == u, nl, -1))]).astype(np.int32)
    return jnp.asarray(lidx)


def _swap_halves(x, h, t_idx):
    c = x.shape[0]
    if h >= V7X_SUBLANES:
        tiles = [x[i * h:(i + 1) * h] for i in range(c // h)]
        return jnp.concatenate([tiles[i ^ 1] for i in range(len(tiles))], axis=0)
    x3 = x.reshape(c // V7X_SUBLANES, V7X_SUBLANES, x.shape[1])
    if 2 * h == V7X_SUBLANES:
        return pltpu.roll(x3, h, axis=1).reshape(x.shape)
    up = pltpu.roll(x3, h, axis=1).reshape(x.shape)
    down = pltpu.roll(x3, V7X_SUBLANES - h, axis=1).reshape(x.shape)
    return jnp.where((t_idx & h) != 0, up, down)


def _scan_chunk(direction, f, k, q, v, st_ref, lidx, t_idx):
    c, nl = SCAN_CHUNK, SCAN_LEVELS
    qf = q.astype(F32)
    kf = k.astype(F32)
    q_dec = f
    k_dec = jnp.ones_like(f)
    total = f
    scores = jnp.zeros((c, c), F32)
    for l in range(nl):
        h = 1 << l
        bit = (t_idx & h) != 0
        is_query = bit if direction == 0 else jnp.logical_not(bit)
        x = jnp.where(is_query, q_dec * qf, k_dec * kf).astype(BF16)
        scores = jnp.where(lidx == l, _dot_nt(x, x), scores)
        other = _swap_halves(total, h, t_idx)
        q_dec = q_dec * jnp.where(is_query, other, 1.0)
        k_dec = k_dec * jnp.where(is_query, 1.0, other)
        total = total * other
    scores = jnp.where(lidx == nl, _dot_nt(q, k), scores)
    q_in = (q_dec * qf).astype(BF16)
    k_st = (k_dec * kf).astype(BF16)
    st = st_ref[...]
    o = _dot(scores.astype(BF16), v) + _dot_nt(q_in, st.astype(BF16))
    st_ref[...] = total[0:1] * st + _dot_tn(v, k_st)
    return o


def _scan_kernel(ff_ref, fb_ref, kf_ref, kb_ref, i_ref, q_ref, g_ref, s0f_ref, s0b_ref, ng_ref,
                 lidx_ref, o_ref, stf_ref, stb_ref, part_ref):
    c = SCAN_CHUNK
    n_chunks = SEQ // c
    stf_ref[...] = s0f_ref[...]
    stb_ref[...] = s0b_ref[...]
    t_idx = lax.broadcasted_iota(I32, (c, HG_HEAD_DIM), 0)
    ng = ng_ref[...]

    def chunk(direction, r0):
        rows = pl.ds(r0, c)
        f_ref, k_ref, st_ref = (ff_ref, kf_ref, stf_ref) if direction == 0 else (fb_ref, kb_ref, stb_ref)
        return _scan_chunk(direction, f_ref[rows, :], k_ref[rows, :], q_ref[rows, :], i_ref[rows, :],
                           st_ref, lidx_ref[direction], t_idx)

    def finalize(o, r0):
        rows = pl.ds(r0, c)
        gate = g_ref[rows, :].astype(F32)
        ms = jnp.mean(o * o, axis=-1, keepdims=True)
        o_ref[rows, :] = (o * lax.rsqrt(ms + EPS) * ng * (gate * jax.nn.sigmoid(gate))).astype(BF16)

    def starts(n):
        return pl.multiple_of(n * c, c), pl.multiple_of((n_chunks - 1 - n) * c, c)

    def first_half(n, carry):
        rf, rb = starts(n)
        part_ref[pl.ds(rf, c), :] = chunk(0, rf)
        part_ref[pl.ds(rb, c), :] = chunk(1, rb)
        return carry

    def second_half(n, carry):
        rf, rb = starts(n)
        finalize(chunk(0, rf) + part_ref[pl.ds(rf, c), :], rf)
        finalize(chunk(1, rb) + part_ref[pl.ds(rb, c), :], rb)
        return carry

    lax.fori_loop(0, n_chunks // 2, first_half, 0, unroll=SCAN_UNROLL)
    lax.fori_loop(n_chunks // 2, n_chunks, second_half, 0, unroll=SCAN_UNROLL)


def _hgrn2_scan(fdec, kk, rest, s0f, s0b, norm_g):
    dh, nh = HG_HEAD_DIM, HG_HEADS
    lidx = _pair_levels()
    blk = lambda off: pl.BlockSpec((SEQ, dh), lambda b, h: (b, off + h))
    st = pl.BlockSpec((None, None, dh, dh), lambda b, h: (b, h, 0, 0))
    vmem = (2 * (2 * SEQ * dh * 4 + 6 * SEQ * dh * 2 + 2 * dh * dh * 4) + SEQ * dh * 4 + 2 * dh * dh * 4
            + 2 * 2 * SCAN_CHUNK * V7X_LANES * 4)
    return pl.pallas_call(
        _scan_kernel,
        grid=(BATCH, nh),
        in_specs=[blk(0), blk(nh), blk(0), blk(nh), blk(0), blk(nh), blk(2 * nh), st, st,
                  pl.BlockSpec((1, dh), lambda b, h: (0, h)),
                  pl.BlockSpec(lidx.shape, lambda b, h: (0, 0, 0))],
        out_specs=blk(0),
        out_shape=jax.ShapeDtypeStruct((BATCH * SEQ, HG_WIDTH), BF16),
        scratch_shapes=[pltpu.VMEM((dh, dh), F32), pltpu.VMEM((dh, dh), F32), pltpu.VMEM((SEQ, dh), F32)],
        compiler_params=_params(vmem, ("arbitrary", "arbitrary")),
        name="hgrn2_scan",
    )(fdec, fdec, kk, kk, rest, rest, rest, s0f, s0b, norm_g, lidx)


POOL_PIECE = 256


def _box_bounds(n, w):
    start = np.arange(n) - w // 2
    return np.clip(start, 0, n), np.clip(start + w, 0, n)


def _pool_consts():
    rows = SEQ // GRID_W
    col_box = np.zeros((len(POOL_WINDOWS), POOL_PIECE, POOL_PIECE), np.float32)
    cnt = np.zeros((len(POOL_WINDOWS), SEQ, 1), np.float32)
    cc = np.arange(GRID_W)[None, :]
    for gi, w in enumerate(POOL_WINDOWS):
        c0, c1 = _box_bounds(GRID_W, w)
        r0, r1 = _box_bounds(rows, w)
        wc = ((cc >= c0[:, None]) & (cc < c1[:, None])).astype(np.float32)
        col_box[gi] = np.kron(np.eye(POOL_PIECE // GRID_W, dtype=np.float32), wc)
        cnt[gi] = ((r1 - r0)[:, None] * (c1 - c0)[None, :]).reshape(SEQ, 1)
    return jnp.asarray(col_box, dtype=BF16), jnp.asarray(cnt)


def _pool_kernel(v_ref, box_ref, cnt_ref, pw_ref, ps_ref, o_ref, pre_ref, diff_ref):
    gi = pl.program_id(1)
    rows = SEQ // GRID_W
    gw = GRID_W
    box = box_ref[...]
    pre_ref[0:gw, :] = jnp.zeros((gw, POOL_GROUP), F32)
    for p in range(SEQ // POOL_PIECE):
        yc = _dot(box, v_ref[p * POOL_PIECE:(p + 1) * POOL_PIECE, :])
        for rr in range(POOL_PIECE // gw):
            r = p * (POOL_PIECE // gw) + rr
            pre_ref[(r + 1) * gw:(r + 2) * gw, :] = pre_ref[r * gw:(r + 1) * gw, :] + yc[rr * gw:(rr + 1) * gw]
    for k, w in enumerate(POOL_WINDOWS):
        @pl.when(gi == k)
        def _(w=w):
            r0, r1 = _box_bounds(rows, w)
            for r in range(rows):
                sl = slice(r * gw, (r + 1) * gw)
                box_sum = pre_ref[int(r1[r]) * gw:(int(r1[r]) + 1) * gw, :] - pre_ref[int(r0[r]) * gw:(int(r0[r]) + 1) * gw, :]
                diff_ref[sl, :] = (box_sum / cnt_ref[sl, :] - v_ref[sl, :].astype(F32)).astype(BF16)
    o_ref[...] = (_dot(diff_ref[...], pw_ref[...].astype(BF16)) * ps_ref[...]).astype(BF16)


def _grid_pool(rest, pool_w, pool_scale):
    ng, pg = len(POOL_WINDOWS), POOL_GROUP
    col_box, cnt = _pool_consts()
    v_off = 3 * HG_WIDTH // pg
    vmem = (2 * (2 * SEQ * pg * 2 + POOL_PIECE * POOL_PIECE * 2 + SEQ * 128 * 4 + pg * pg * 4)
            + (SEQ + GRID_W) * pg * 4 + SEQ * pg * 2 + SEQ * pg * 4)
    return pl.pallas_call(
        _pool_kernel,
        grid=(BATCH, ng),
        in_specs=[pl.BlockSpec((SEQ, pg), lambda b, k: (b, v_off + k)),
                  pl.BlockSpec((None, POOL_PIECE, POOL_PIECE), lambda b, k: (k, 0, 0)),
                  pl.BlockSpec((None, SEQ, 1), lambda b, k: (k, 0, 0)),
                  pl.BlockSpec((None, pg, pg), lambda b, k: (k, 0, 0)),
                  pl.BlockSpec((1, pg), lambda b, k: (0, k))],
        out_specs=pl.BlockSpec((SEQ, pg), lambda b, k: (b, k)),
        out_shape=jax.ShapeDtypeStruct((BATCH * SEQ, POOL_WIDTH), BF16),
        scratch_shapes=[pltpu.VMEM((SEQ + GRID_W, pg), F32), pltpu.VMEM((SEQ, pg), BF16)],
        compiler_params=_params(vmem, ("arbitrary", "arbitrary")),
        name="grid_pool",
    )(rest, col_box, cnt, pool_w, pool_scale)


def _out_proj_kernel(o_ref, pm_ref, wa_ref, wb_ref, x_ref, g1_ref, out_ref):
    acc = _dot(o_ref[...], wa_ref[...]) + _dot(pm_ref[...], wb_ref[...])
    out_ref[...] = x_ref[...] + g1_ref[...] * acc


def _out_proj(o, pm, w_bf16, x2d, g1):
    n, d = x2d.shape
    tm, tn = 1024, 512
    half = w_bf16.shape[0] // 2
    tiles_per_sample = SEQ // tm
    vmem = 2 * (2 * tm * half * 2 + 2 * half * tn * 2 + 2 * tm * tn * 4) + tm * tn * 4
    return pl.pallas_call(
        _out_proj_kernel,
        grid=(n // tm, d // tn),
        in_specs=[pl.BlockSpec((tm, half), lambda i, j: (i, 0)),
                  pl.BlockSpec((tm, half), lambda i, j: (i, 0)),
                  pl.BlockSpec((half, tn), lambda i, j: (0, j)),
                  pl.BlockSpec((half, tn), lambda i, j: (1, j)),
                  pl.BlockSpec((tm, tn), lambda i, j: (i, j)),
                  pl.BlockSpec((None, 1, tn), lambda i, j: (i // tiles_per_sample, 0, j))],
        out_specs=pl.BlockSpec((tm, tn), lambda i, j: (i, j)),
        out_shape=jax.ShapeDtypeStruct((n, d), F32),
        compiler_params=_params(vmem, ("arbitrary", "arbitrary")),
        name="out_proj",
    )(o, pm, w_bf16, w_bf16, x2d, g1)


def _norm2_router_kernel(x_ref, g_ref, sc_ref, sh_ref, rw_ref, h2_ref, aff_ref):
    h = _modulated_norm(x_ref[...], g_ref[...], sc_ref[...], sh_ref[...])
    h2_ref[...] = h.astype(BF16)
    h_hi, h_lo = _split_f32(h)
    r_hi, r_lo = _split_f32(rw_ref[...])
    h_hi, h_lo, r_hi, r_lo = (a.astype(BF16) for a in (h_hi, h_lo, r_hi, r_lo))
    logits = _dot_nt(r_hi, h_hi) + (_dot_nt(r_hi, h_lo) + _dot_nt(r_lo, h_hi))
    e = jnp.exp(logits - jnp.max(logits, axis=0, keepdims=True))
    aff_ref[...] = e / jnp.sum(e, axis=0, keepdims=True)


def _norm2_router(x1, g, sc, sh, router_wt):
    n, d = x1.shape
    tm = 256
    tiles_per_sample = SEQ // tm
    vmem = 2 * (tm * d * 4 + tm * d * 2 + N_EXPERTS * d * 4) + 6 * tm * d * 4
    return pl.pallas_call(
        _norm2_router_kernel,
        grid=(n // tm,),
        in_specs=[pl.BlockSpec((tm, d), lambda i: (i, 0)),
                  pl.BlockSpec((1, d), lambda i: (0, 0)),
                  pl.BlockSpec((None, 1, d), lambda i: (i // tiles_per_sample, 0, 0)),
                  pl.BlockSpec((None, 1, d), lambda i: (i // tiles_per_sample, 0, 0)),
                  pl.BlockSpec((N_EXPERTS, d), lambda i: (0, 0))],
        out_specs=[pl.BlockSpec((tm, d), lambda i: (i, 0)),
                   pl.BlockSpec((None, N_EXPERTS, tm), lambda i: (i // tiles_per_sample, 0, i % tiles_per_sample))],
        out_shape=[jax.ShapeDtypeStruct((n, d), BF16),
                   jax.ShapeDtypeStruct((BATCH, N_EXPERTS, SEQ), F32)],
        compiler_params=_params(vmem, ("arbitrary",)),
        name="norm2_router",
    )(x1, g, sc, sh, router_wt)


ROUTE_COLS = 256
ROUTE_ROWS = 8


def _route_kernel(aff_ref, slot_ref, gate_ref):
    a = aff_ref[...]
    rows, n = a.shape
    thr = jnp.zeros((rows, 1), I32)
    for b in range(30, -1, -1):
        cand = thr | (1 << b)
        cnt = jnp.sum((a >= pltpu.bitcast(cand, F32)).astype(I32), axis=-1, keepdims=True)
        thr = jnp.where(cnt >= CAPACITY, cand, thr)
    gt = a >= pltpu.bitcast(thr + 1, F32)
    eq = jnp.logical_and(a >= pltpu.bitcast(thr, F32), jnp.logical_not(gt))
    need = CAPACITY - jnp.sum(gt.astype(I32), axis=-1, keepdims=True)

    def prefix_count(mask):
        m = jnp.where(mask, 1.0, 0.0).astype(BF16)
        r = lax.broadcasted_iota(I32, (n, ROUTE_COLS), 0)
        c = lax.broadcasted_iota(I32, (n, ROUTE_COLS), 1)
        parts = [_dot(m, jnp.where(r < c + cb * ROUTE_COLS, 1.0, 0.0).astype(BF16))
                 for cb in range(n // ROUTE_COLS)]
        return jnp.concatenate(parts, axis=1).astype(I32)

    sel = jnp.logical_or(gt, jnp.logical_and(eq, prefix_count(eq) < need))
    slot = jnp.where(sel, prefix_count(sel), -1)
    slot_ref[...] = slot

    a_hi, a_rest = _split_f32(a)
    a_mid, a_lo = _split_f32(a_rest)
    slot_ids = lax.broadcasted_iota(I32, (CAPACITY, n), 0)
    pad = jnp.zeros((ROUTE_ROWS - 3, n), F32)
    for r in range(rows):
        onehot = jnp.where(slot[r:r + 1] == slot_ids, 1.0, 0.0).astype(BF16)
        pieces = jnp.concatenate([a_hi[r:r + 1], a_mid[r:r + 1], a_lo[r:r + 1], pad], axis=0).astype(BF16)
        res = _dot_nt(pieces, onehot)
        gate_ref[r:r + 1, :] = res[0:1] + res[1:2] + res[2:3]


def _route(aff_rows):
    rows, n = aff_rows.shape
    vmem = 4 * ROUTE_ROWS * n * 4 + 4 * n * ROUTE_COLS * 4 + 4 * CAPACITY * n * 4
    return pl.pallas_call(
        _route_kernel,
        grid=(rows // ROUTE_ROWS,),
        in_specs=[pl.BlockSpec((ROUTE_ROWS, n), lambda i: (i, 0))],
        out_specs=[pl.BlockSpec((ROUTE_ROWS, n), lambda i: (i, 0)),
                   pl.BlockSpec((ROUTE_ROWS, CAPACITY), lambda i: (i, 0))],
        out_shape=[jax.ShapeDtypeStruct((rows, n), I32),
                   jax.ShapeDtypeStruct((rows, CAPACITY), F32)],
        compiler_params=_params(vmem, ("arbitrary",)),
        name="route",
    )(aff_rows)


GATHER_EXPERTS = 2


def _moe_gather_kernel(slot_ref, h2_ref, xg_ref):
    slots = slot_ref[...]
    slot_ids = lax.broadcasted_iota(I32, (CAPACITY, SEQ), 0)
    h2 = h2_ref[...]
    for j in range(GATHER_EXPERTS):
        onehot = jnp.where(slots[j:j + 1] == slot_ids, 1.0, 0.0).astype(BF16)
        xg_ref[j] = _dot(onehot, h2).astype(BF16)


def _moe_gather(slot_bel, h2):
    d = h2.shape[1]
    vmem = (SEQ * d * 2 + 2 * GATHER_EXPERTS * CAPACITY * d * 2 + 2 * N_EXPERTS * SEQ * 4
            + CAPACITY * SEQ * (4 + 2) + CAPACITY * d * (4 + 2))
    return pl.pallas_call(
        _moe_gather_kernel,
        grid=(BATCH, N_EXPERTS // GATHER_EXPERTS),
        in_specs=[pl.BlockSpec((None, None, GATHER_EXPERTS, SEQ), lambda b, g: (b, g, 0, 0)),
                  pl.BlockSpec((SEQ, d), lambda b, g: (b, 0), pipeline_mode=pl.Buffered(1))],
        out_specs=pl.BlockSpec((GATHER_EXPERTS, None, CAPACITY, d), lambda b, g: (g, b, 0, 0)),
        out_shape=jax.ShapeDtypeStruct((N_EXPERTS, BATCH, CAPACITY, d), BF16),
        compiler_params=_params(vmem, ("arbitrary", "arbitrary")),
        name="moe_gather",
    )(slot_bel.reshape(BATCH, N_EXPERTS // GATHER_EXPERTS, GATHER_EXPERTS, SEQ), h2)


def _moe_up_kernel(xg_ref, w1_ref, w3_ref, hid_ref):
    xg = xg_ref[...]
    a = _dot(xg, w1_ref[...].astype(BF16))
    b = _dot(xg, w3_ref[...].astype(BF16))
    hid_ref[...] = (a * jax.nn.sigmoid(a) * b).astype(BF16)


def _moe_up(xg, w1, w3):
    ne, m, d = xg.shape
    ff = w1.shape[2]
    tf = 256
    vmem = 2 * m * d * 2 + 2 * 2 * d * tf * 4 + 2 * d * tf * 2 + 3 * m * tf * 4 + 2 * m * tf * 2
    w_spec = pl.BlockSpec((None, d, tf), lambda e, f: (e, 0, f))
    return pl.pallas_call(
        _moe_up_kernel,
        grid=(ne, ff // tf),
        in_specs=[pl.BlockSpec((None, m, d), lambda e, f: (e, 0, 0)), w_spec, w_spec],
        out_specs=pl.BlockSpec((None, m, tf), lambda e, f: (e, 0, f)),
        out_shape=jax.ShapeDtypeStruct((ne, m, ff), BF16),
        compiler_params=_params(vmem, ("arbitrary", "arbitrary")),
        name="moe_up",
    )(xg, w1, w3)


def _moe_down_kernel(hid_ref, w2_ref, gate_ref, y_ref):
    y = _dot(hid_ref[...], w2_ref[...].astype(BF16))
    y_ref[...] = (y * gate_ref[...]).astype(BF16)


def _moe_down(hid, w2, gate_col):
    ne, m, ff = hid.shape
    d = w2.shape[2]
    tn = 1024
    vmem = 2 * m * ff * 2 + 2 * ff * tn * 4 + ff * tn * 2 + 2 * m * 128 * 4 + 2 * m * tn * 4 + 2 * m * tn * 2
    return pl.pallas_call(
        _moe_down_kernel,
        grid=(ne, d // tn),
        in_specs=[pl.BlockSpec((None, m, ff), lambda e, j: (e, 0, 0)),
                  pl.BlockSpec((None, ff, tn), lambda e, j: (e, 0, j)),
                  pl.BlockSpec((None, m, 1), lambda e, j: (e, 0, 0))],
        out_specs=pl.BlockSpec((None, m, tn), lambda e, j: (e, 0, j)),
        out_shape=jax.ShapeDtypeStruct((ne, m, d), BF16),
        compiler_params=_params(vmem, ("arbitrary", "arbitrary")),
        name="moe_down",
    )(hid, w2, gate_col)


COMBINE_TOKENS = 128


def _moe_combine_kernel(slot_ref, y_ref, x1_ref, g2_ref, fg_ref, out_ref):
    st = slot_ref[...]
    slot_ids = lax.broadcasted_iota(I32, (COMBINE_TOKENS, CAPACITY), 1)
    onehot = jnp.concatenate(
        [jnp.where(st[:, e:e + 1] == slot_ids, 1.0, 0.0).astype(BF16) for e in range(N_EXPERTS)], axis=1)
    y = y_ref[...].reshape(N_EXPERTS * CAPACITY, y_ref.shape[-1])
    x2 = x1_ref[...] + g2_ref[...] * _dot(onehot, y)
    ms = jnp.mean(x2 * x2, axis=-1, keepdims=True)
    out_ref[...] = x2 * lax.rsqrt(ms + EPS) * fg_ref[...]


def _moe_combine(slot_ble, y, x1, g2, final_g):
    n, d = x1.shape
    tm = COMBINE_TOKENS
    tiles_per_sample = SEQ // tm
    vmem = (N_EXPERTS * CAPACITY * d * 2 + 2 * 2 * tm * d * 4 + 2 * tm * 128 * 4
            + tm * N_EXPERTS * CAPACITY * 2 + 3 * tm * d * 4)
    return pl.pallas_call(
        _moe_combine_kernel,
        grid=(BATCH, tiles_per_sample),
        in_specs=[pl.BlockSpec((None, tm, N_EXPERTS), lambda b, t: (b, t, 0)),
                  pl.BlockSpec((N_EXPERTS, None, CAPACITY, d), lambda b, t: (0, b, 0, 0),
                               pipeline_mode=pl.Buffered(1)),
                  pl.BlockSpec((tm, d), lambda b, t: (b * tiles_per_sample + t, 0)),
                  pl.BlockSpec((None, 1, d), lambda b, t: (b, 0, 0)),
                  pl.BlockSpec((1, d), lambda b, t: (0, 0))],
        out_specs=pl.BlockSpec((tm, d), lambda b, t: (b * tiles_per_sample + t, 0)),
        out_shape=jax.ShapeDtypeStruct((n, d), F32),
        compiler_params=_params(vmem, ("arbitrary", "arbitrary")),
        name="moe_combine",
    )(slot_ble, y, x1, g2, final_g)


def kernel(x, c, ctx, c_ctx, ada_w, ada_b, norm1_g, norm2_g, w_in, lb_param, hg_norm_g, pool_w, pool_scale,
           w_out, router_w, moe_w1, moe_w3, moe_w2, final_norm_g):
    nb, seq, d = x.shape
    assert (nb, seq, d) == (BATCH, SEQ, D_MODEL) and ctx.shape[1] == CTX_LEN
    assert ada_w.shape[0] == 1 and lb_param.shape[0] == 2, "single-layer block: layer 0 uses lower-bound row 0"
    x2d = x.reshape(nb * seq, d)
    ctx2d = ctx.reshape(nb * CTX_LEN, d)

    cvecs = jnp.zeros((ADA_ROWS, d), F32).at[:nb].set(c).at[nb].set(c_ctx)
    mod = _ada_mod(cvecs, ada_w[0], ada_b).reshape(ADA_ROWS, 6, d)
    sh1, sc1, g1, sh2, sc2, g2 = (mod[:nb, k][:, None, :] for k in range(6))
    csh1, csc1 = mod[nb:nb + 1, 0][:, None, :], mod[nb:nb + 1, 1][:, None, :]

    w_in_bf = w_in[0].astype(BF16)
    w_out_bf = w_out[0].astype(BF16)
    lbp = lb_param.reshape(lb_param.shape[0], 2 * HG_WIDTH)

    f_c, k_c, i_c = _in_proj(ctx2d, norm1_g, csc1, csh1, lbp, w_in_bf,
                             rows_per_sample=nb * CTX_LEN, tm=256, n_cols=3 * HG_WIDTH)
    s_f, s_b = _ctx_states(f_c, k_c, i_c)

    fdec, kk, rest = _in_proj(x2d, norm1_g, sc1, sh1, lbp, w_in_bf,
                              rows_per_sample=seq, tm=512, n_cols=w_in.shape[2])
    o = _hgrn2_scan(fdec, kk, rest, s_f, s_b, hg_norm_g)
    pm = _grid_pool(rest, pool_w[0], pool_scale)
    x1 = _out_proj(o, pm, w_out_bf, x2d, g1)

    h2, aff_t = _norm2_router(x1, norm2_g, sc2, sh2, router_w[0].T)
    slot, gates = _route(aff_t.reshape(nb * N_EXPERTS, seq))
    slot = slot.reshape(nb, N_EXPERTS, seq)
    xg = _moe_gather(slot, h2).reshape(N_EXPERTS, nb * CAPACITY, d)
    hid = _moe_up(xg, moe_w1[0], moe_w3[0])
    gate_col = gates.reshape(nb, N_EXPERTS, CAPACITY).transpose(1, 0, 2).reshape(N_EXPERTS, nb * CAPACITY, 1)
    y = _moe_down(hid, moe_w2[0], gate_col).reshape(N_EXPERTS, nb, CAPACITY, d)
    out = _moe_combine(slot.transpose(0, 2, 1), y, x1, g2, final_norm_g[None, :])
    return out.reshape(nb, seq, d).astype(x.dtype)
```

```python
import jax
import jax.numpy as jnp
import numpy as np
from jax import lax
from jax.experimental import pallas as pl
from jax.experimental.pallas import tpu as pltpu

F32 = jnp.float32
BF16 = jnp.bfloat16
I32 = jnp.int32

D_MODEL = 4096
BATCH = 4
SEQ = 2048
GRID_W = 64
CTX_LEN = 256
HG_HEADS = 16
HG_HEAD_DIM = 128
HG_WIDTH = HG_HEADS * HG_HEAD_DIM
POOL_WIDTH = D_MODEL - HG_WIDTH
POOL_WINDOWS = (2, 4, 8, 16)
POOL_GROUP = POOL_WIDTH // len(POOL_WINDOWS)
N_EXPERTS = 16
EXPERT_FF = D_MODEL // 2
CAPACITY = 2 * SEQ // N_EXPERTS
EPS = 1e-6

V7X_VMEM_BYTES = 64 * 1024 * 1024
V7X_VMEM_HEADROOM_BYTES = 10 * 1024 * 1024
V7X_SUBLANES = 8
V7X_LANES = 128
NORM_ROWS = 16
ADA_ROWS = 8

SCAN_CHUNK = 64
SCAN_LEVELS = 6
SCAN_UNROLL = 4

_NT = (((1,), (1,)), ((), ()))
_TN = (((0,), (0,)), ((), ()))


def _dot(a, b):
    return jnp.dot(a, b, preferred_element_type=F32)


def _dot_nt(a, b):
    return lax.dot_general(a, b, _NT, preferred_element_type=F32)


def _dot_tn(a, b):
    return lax.dot_general(a, b, _TN, preferred_element_type=F32)


def _split_f32(x):
    hi = x.astype(BF16).astype(F32)
    return hi, x - hi


def _params(vmem_bytes, semantics):
    limit = min(int(vmem_bytes) + V7X_VMEM_HEADROOM_BYTES, V7X_VMEM_BYTES)
    return pltpu.CompilerParams(dimension_semantics=semantics, vmem_limit_bytes=limit)


def _ada_kernel(c_ref, w_ref, b_ref, o_ref):
    c = c_ref[...]
    s = c * jax.nn.sigmoid(c)
    hi, lo = _split_f32(s)
    lhs = jnp.concatenate([hi, lo], axis=0).astype(BF16)
    r = _dot(lhs, w_ref[...].astype(BF16))
    o_ref[...] = r[:ADA_ROWS] + r[ADA_ROWS:] + b_ref[...]


def _ada_mod(cvecs, w, b):
    d, n = w.shape
    tn = 512
    vmem = 2 * d * tn * 4 + d * tn * 2 + 4 * ADA_ROWS * d * 4
    return pl.pallas_call(
        _ada_kernel,
        grid=(n // tn,),
        in_specs=[pl.BlockSpec((ADA_ROWS, d), lambda j: (0, 0)),
                  pl.BlockSpec((d, tn), lambda j: (0, j)),
                  pl.BlockSpec((1, tn), lambda j: (0, j))],
        out_specs=pl.BlockSpec((ADA_ROWS, tn), lambda j: (0, j)),
        out_shape=jax.ShapeDtypeStruct((ADA_ROWS, n), F32),
        compiler_params=_params(vmem, ("arbitrary",)),
        name="ada_mod",
    )(cvecs, w, b)


def _modulated_norm(x, g, scale, shift):
    ms = jnp.mean(x * x, axis=-1, keepdims=True)
    return (x * lax.rsqrt(ms + EPS) * g) * (1.0 + scale) + shift


def _norm1_kernel(x_ref, g_ref, sc_ref, sh_ref, h_ref):
    g, sc, sh = g_ref[...], sc_ref[...], sh_ref[...]

    def norm_rows(r, carry):
        rows = pl.ds(pl.multiple_of(r * NORM_ROWS, NORM_ROWS), NORM_ROWS)
        h_ref[rows, :] = _modulated_norm(x_ref[rows, :], g, sc, sh).astype(BF16)
        return carry

    lax.fori_loop(0, x_ref.shape[0] // NORM_ROWS, norm_rows, 0, unroll=2)


def _norm1(x2d, g, sc, sh, *, rows_per_sample):
    n, d = x2d.shape
    tm = 256
    tiles_per_sample = rows_per_sample // tm
    mod = pl.BlockSpec((None, 1, d), lambda i: (i // tiles_per_sample, 0, 0))
    vmem = 2 * tm * d * (4 + 2) + 8 * NORM_ROWS * d * 4
    return pl.pallas_call(
        _norm1_kernel,
        grid=(n // tm,),
        in_specs=[pl.BlockSpec((tm, d), lambda i: (i, 0)), pl.BlockSpec((1, d), lambda i: (0, 0)), mod, mod],
        out_specs=pl.BlockSpec((tm, d), lambda i: (i, 0)),
        out_shape=jax.ShapeDtypeStruct((n, d), BF16),
        compiler_params=_params(vmem, ("arbitrary",)),
        name="norm1",
    )(x2d, g, sc, sh)


def _gate_proj_kernel(h_ref, lbp_ref, w_ref, f_ref, k_ref):
    z = _dot(h_ref[...], w_ref[...])
    p = lbp_ref[...]
    e = jnp.exp(p - jnp.max(p, axis=0, keepdims=True))
    lb = e[0:1] / jnp.sum(e, axis=0, keepdims=True)
    f = lb + (1.0 - lb) * jax.nn.sigmoid(z)
    f_ref[...] = f
    k_ref[...] = (1.0 - f).astype(BF16)


def _rest_proj_kernel(h_ref, w_ref, r_ref):
    r_ref[...] = _dot(h_ref[...], w_ref[...]).astype(BF16)


def _in_proj(h, lbp, w_bf16, *, n_cols):
    n, d = h.shape
    tn = 1024
    n_gate = 2 * HG_WIDTH // tn
    tm_gate, tm_rest = 512, 1024
    vmem = 2 * tm_gate * d * 2 + 2 * d * tn * 2 + 2 * tm_gate * tn * (4 + 2) + 4 * tm_gate * tn * 4
    fdec, kk = pl.pallas_call(
        _gate_proj_kernel,
        grid=(n // tm_gate, n_gate),
        in_specs=[pl.BlockSpec((tm_gate, d), lambda i, j: (i, 0)),
                  pl.BlockSpec((2, tn), lambda i, j: (0, j)),
                  pl.BlockSpec((d, tn), lambda i, j: (0, j))],
        out_specs=[pl.BlockSpec((tm_gate, tn), lambda i, j: (i, j)),
                   pl.BlockSpec((tm_gate, tn), lambda i, j: (i, j))],
        out_shape=[jax.ShapeDtypeStruct((n, 2 * HG_WIDTH), F32),
                   jax.ShapeDtypeStruct((n, 2 * HG_WIDTH), BF16)],
        compiler_params=_params(vmem, ("arbitrary", "arbitrary")),
        name="gate_proj",
    )(h, lbp, w_bf16)
    vmem = 2 * tm_rest * d * 2 + 2 * d * tn * 2 + 2 * tm_rest * tn * 2 + 2 * tm_rest * tn * 4
    rest = pl.pallas_call(
        _rest_proj_kernel,
        grid=(n // tm_rest, (n_cols - 2 * HG_WIDTH) // tn),
        in_specs=[pl.BlockSpec((tm_rest, d), lambda i, j: (i, 0)),
                  pl.BlockSpec((d, tn), lambda i, j: (0, n_gate + j))],
        out_specs=pl.BlockSpec((tm_rest, tn), lambda i, j: (i, j)),
        out_shape=jax.ShapeDtypeStruct((n, n_cols - 2 * HG_WIDTH), BF16),
        compiler_params=_params(vmem, ("arbitrary", "arbitrary")),
        name="rest_proj",
    )(h, w_bf16)
    return fdec, kk, rest


def _ctx_tri():
    s = np.arange(CTX_LEN)[:, None]
    u = np.arange(CTX_LEN)[None, :]
    tri = np.stack([u > s, u < s]).astype(np.float32)
    return jnp.asarray(np.concatenate([tri, tri], axis=2), dtype=BF16)


def _ctx_state_kernel(ff_ref, fb_ref, kf_ref, kb_ref, i_ref, tri_ref, sf_ref, sb_ref):
    v = i_ref[...]

    def state(f, k, a):
        hi, lo = _split_f32(jnp.log(f))
        g = _dot(a, jnp.concatenate([hi, lo], axis=0).astype(BF16))
        kd = (k.astype(F32) * jnp.exp(g)).astype(BF16)
        return _dot_tn(v, kd)

    sf_ref[...] = state(ff_ref[...], kf_ref[...], tri_ref[0])
    sb_ref[...] = state(fb_ref[...], kb_ref[...], tri_ref[1])


def _ctx_states(fdec, kk, vi):
    t, dh, nh = CTX_LEN, HG_HEAD_DIM, HG_HEADS
    blk = lambda off: pl.BlockSpec((t, dh), lambda b, h: (b, off + h))
    st = pl.BlockSpec((None, None, dh, dh), lambda b, h: (b, h, 0, 0))
    shape = jax.ShapeDtypeStruct((BATCH, nh, dh, dh), F32)
    vmem = 2 * (2 * t * dh * 4 + 3 * t * dh * 2 + 2 * dh * dh * 4) + 2 * 2 * t * 2 * t * 2
    return pl.pallas_call(
        _ctx_state_kernel,
        grid=(BATCH, nh),
        in_specs=[blk(0), blk(nh), blk(0), blk(nh), blk(0),
                  pl.BlockSpec((2, t, 2 * t), lambda b, h: (0, 0, 0))],
        out_specs=[st, st],
        out_shape=[shape, shape],
        compiler_params=_params(vmem, ("arbitrary", "arbitrary")),
        name="ctx_state",
    )(fdec, fdec, kk, kk, vi, _ctx_tri())


def _pair_levels():
    c, nl = SCAN_CHUNK, SCAN_LEVELS
    t = np.arange(c)[:, None]
    u = np.arange(c)[None, :]
    lev = np.floor(np.log2(np.maximum(t ^ u, 1))).astype(np.int32)
    lidx = np.stack([np.where(t > u, lev, np.where(t == u, nl, -1)),
                     np.where(t < u, lev, np.where(t == u, nl, -1))]).astype(np.int32)
    return jnp.asarray(lidx)


def _swap_halves(x, h, t_idx):
    c = x.shape[0]
    if h >= V7X_SUBLANES:
        tiles = [x[i * h:(i + 1) * h] for i in range(c // h)]
        return jnp.concatenate([tiles[i ^ 1] for i in range(len(tiles))], axis=0)
    x3 = x.reshape(c // V7X_SUBLANES, V7X_SUBLANES, x.shape[1])
    if 2 * h == V7X_SUBLANES:
        return pltpu.roll(x3, h, axis=1).reshape(x.shape)
    up = pltpu.roll(x3, h, axis=1).reshape(x.shape)
    down = pltpu.roll(x3, V7X_SUBLANES - h, axis=1).reshape(x.shape)
    return jnp.where((t_idx & h) != 0, up, down)


def _scan_chunk(direction, f, k, q, v, st_ref, lidx, t_idx):
    c, nl = SCAN_CHUNK, SCAN_LEVELS
    qf = q.astype(F32)
    kf = k.astype(F32)
    q_dec = f
    k_dec = jnp.ones_like(f)
    total = f
    scores = jnp.zeros((c, c), F32)
    for l in range(nl):
        h = 1 << l
        bit = (t_idx & h) != 0
        is_query = bit if direction == 0 else jnp.logical_not(bit)
        x = jnp.where(is_query, q_dec * qf, k_dec * kf).astype(BF16)
        scores = jnp.where(lidx == l, _dot_nt(x, x), scores)
        other = _swap_halves(total, h, t_idx)
        q_dec = q_dec * jnp.where(is_query, other, 1.0)
        k_dec = k_dec * jnp.where(is_query, 1.0, other)
        total = total * other
    scores = jnp.where(lidx == nl, _dot_nt(q, k), scores)
    q_in = (q_dec * qf).astype(BF16)
    k_st = (k_dec * kf).astype(BF16)
    st = st_ref[...]
    o = _dot(scores.astype(BF16), v) + _dot_nt(q_in, st.astype(BF16))
    st_ref[...] = total[0:1] * st + _dot_tn(v, k_st)
    return o


def _scan_kernel(ff_ref, fb_ref, kf_ref, kb_ref, i_ref, q_ref, g_ref, s0f_ref, s0b_ref, ng_ref,
                 lidx_ref, o_ref, stf_ref, stb_ref, part_ref):
    c = SCAN_CHUNK
    n_chunks = SEQ // c
    stf_ref[...] = s0f_ref[...]
    stb_ref[...] = s0b_ref[...]
    t_idx = lax.broadcasted_iota(I32, (c, HG_HEAD_DIM), 0)
    ng = ng_ref[...]

    def chunk(direction, r0):
        rows = pl.ds(r0, c)
        f_ref, k_ref, st_ref = (ff_ref, kf_ref, stf_ref) if direction == 0 else (fb_ref, kb_ref, stb_ref)
        return _scan_chunk(direction, f_ref[rows, :], k_ref[rows, :], q_ref[rows, :], i_ref[rows, :],
                           st_ref, lidx_ref[direction], t_idx)

    def finalize(o, r0):
        rows = pl.ds(r0, c)
        gate = g_ref[rows, :].astype(F32)
        ms = jnp.mean(o * o, axis=-1, keepdims=True)
        o_ref[rows, :] = (o * lax.rsqrt(ms + EPS) * ng * (gate * jax.nn.sigmoid(gate))).astype(BF16)

    def starts(n):
        return pl.multiple_of(n * c, c), pl.multiple_of((n_chunks - 1 - n) * c, c)

    def first_half(n, carry):
        rf, rb = starts(n)
        part_ref[pl.ds(rf, c), :] = chunk(0, rf)
        part_ref[pl.ds(rb, c), :] = chunk(1, rb)
        return carry

    def second_half(n, carry):
        rf, rb = starts(n)
        finalize(chunk(0, rf) + part_ref[pl.ds(rf, c), :], rf)
        finalize(chunk(1, rb) + part_ref[pl.ds(rb, c), :], rb)
        return carry

    lax.fori_loop(0, n_chunks // 2, first_half, 0, unroll=SCAN_UNROLL)
    lax.fori_loop(n_chunks // 2, n_chunks, second_half, 0, unroll=SCAN_UNROLL)


def _hgrn2_scan(fdec, kk, rest, s0f, s0b, norm_g):
    dh, nh = HG_HEAD_DIM, HG_HEADS
    lidx = _pair_levels()
    blk = lambda off: pl.BlockSpec((SEQ, dh), lambda b, h: (b, off + h))
    st = pl.BlockSpec((None, None, dh, dh), lambda b, h: (b, h, 0, 0))
    vmem = (2 * (2 * SEQ * dh * 4 + 6 * SEQ * dh * 2 + 2 * dh * dh * 4) + SEQ * dh * 4 + 2 * dh * dh * 4
            + 2 * 2 * SCAN_CHUNK * V7X_LANES * 4)
    return pl.pallas_call(
        _scan_kernel,
        grid=(BATCH, nh),
        in_specs=[blk(0), blk(nh), blk(0), blk(nh), blk(0), blk(nh), blk(2 * nh), st, st,
                  pl.BlockSpec((1, dh), lambda b, h: (0, h)),
                  pl.BlockSpec(lidx.shape, lambda b, h: (0, 0, 0))],
        out_specs=blk(0),
        out_shape=jax.ShapeDtypeStruct((BATCH * SEQ, HG_WIDTH), BF16),
        scratch_shapes=[pltpu.VMEM((dh, dh), F32), pltpu.VMEM((dh, dh), F32), pltpu.VMEM((SEQ, dh), F32)],
        compiler_params=_params(vmem, ("arbitrary", "arbitrary")),
        name="hgrn2_scan",
    )(fdec, fdec, kk, kk, rest, rest, rest, s0f, s0b, norm_g, lidx)


POOL_PIECE = 256


def _box_bounds(n, w):
    start = np.arange(n) - w // 2
    return np.clip(start, 0, n), np.clip(start + w, 0, n)


def _pool_consts():
    rows = SEQ // GRID_W
    col_box = np.zeros((len(POOL_WINDOWS), POOL_PIECE, POOL_PIECE), np.float32)
    cnt = np.zeros((len(POOL_WINDOWS), SEQ, 1), np.float32)
    cc = np.arange(GRID_W)[None, :]
    for gi, w in enumerate(POOL_WINDOWS):
        c0, c1 = _box_bounds(GRID_W, w)
        r0, r1 = _box_bounds(rows, w)
        wc = ((cc >= c0[:, None]) & (cc < c1[:, None])).astype(np.float32)
        col_box[gi] = np.kron(np.eye(POOL_PIECE // GRID_W, dtype=np.float32), wc)
        cnt[gi] = ((r1 - r0)[:, None] * (c1 - c0)[None, :]).reshape(SEQ, 1)
    return jnp.asarray(col_box, dtype=BF16), jnp.asarray(cnt)


def _pool_kernel(v_ref, box_ref, cnt_ref, pw_ref, ps_ref, o_ref, pre_ref, diff_ref):
    gi = pl.program_id(1)
    rows = SEQ // GRID_W
    gw = GRID_W
    box = box_ref[...]
    pre_ref[0:gw, :] = jnp.zeros((gw, POOL_GROUP), F32)
    for p in range(SEQ // POOL_PIECE):
        yc = _dot(box, v_ref[p * POOL_PIECE:(p + 1) * POOL_PIECE, :])
        for rr in range(POOL_PIECE // gw):
            r = p * (POOL_PIECE // gw) + rr
            pre_ref[(r + 1) * gw:(r + 2) * gw, :] = pre_ref[r * gw:(r + 1) * gw, :] + yc[rr * gw:(rr + 1) * gw]
    for k, w in enumerate(POOL_WINDOWS):
        @pl.when(gi == k)
        def _(w=w):
            r0, r1 = _box_bounds(rows, w)
            for r in range(rows):
                sl = slice(r * gw, (r + 1) * gw)
                box_sum = pre_ref[int(r1[r]) * gw:(int(r1[r]) + 1) * gw, :] - pre_ref[int(r0[r]) * gw:(int(r0[r]) + 1) * gw, :]
                diff_ref[sl, :] = (box_sum / cnt_ref[sl, :] - v_ref[sl, :].astype(F32)).astype(BF16)
    o_ref[...] = (_dot(diff_ref[...], pw_ref[...].astype(BF16)) * ps_ref[...]).astype(BF16)


def _grid_pool(rest, pool_w, pool_scale):
    ng, pg = len(POOL_WINDOWS), POOL_GROUP
    col_box, cnt = _pool_consts()
    v_off = 3 * HG_WIDTH // pg
    vmem = (2 * (2 * SEQ * pg * 2 + POOL_PIECE * POOL_PIECE * 2 + SEQ * 128 * 4 + pg * pg * 4)
            + (SEQ + GRID_W) * pg * 4 + SEQ * pg * 2 + SEQ * pg * 4)
    return pl.pallas_call(
        _pool_kernel,
        grid=(BATCH, ng),
        in_specs=[pl.BlockSpec((SEQ, pg), lambda b, k: (b, v_off + k)),
                  pl.BlockSpec((None, POOL_PIECE, POOL_PIECE), lambda b, k: (k, 0, 0)),
                  pl.BlockSpec((None, SEQ, 1), lambda b, k: (k, 0, 0)),
                  pl.BlockSpec((None, pg, pg), lambda b, k: (k, 0, 0)),
                  pl.BlockSpec((1, pg), lambda b, k: (0, k))],
        out_specs=pl.BlockSpec((SEQ, pg), lambda b, k: (b, k)),
        out_shape=jax.ShapeDtypeStruct((BATCH * SEQ, POOL_WIDTH), BF16),
        scratch_shapes=[pltpu.VMEM((SEQ + GRID_W, pg), F32), pltpu.VMEM((SEQ, pg), BF16)],
        compiler_params=_params(vmem, ("arbitrary", "arbitrary")),
        name="grid_pool",
    )(rest, col_box, cnt, pool_w, pool_scale)


def _out_proj_kernel(o_ref, pm_ref, wa_ref, wb_ref, x_ref, g1_ref, out_ref):
    acc = _dot(o_ref[...], wa_ref[...]) + _dot(pm_ref[...], wb_ref[...])
    out_ref[...] = x_ref[...] + g1_ref[...] * acc


def _out_proj(o, pm, w_bf16, x2d, g1):
    n, d = x2d.shape
    tm, tn = 1024, 512
    half = w_bf16.shape[0] // 2
    tiles_per_sample = SEQ // tm
    vmem = 2 * (2 * tm * half * 2 + 2 * half * tn * 2 + 2 * tm * tn * 4) + tm * tn * 4
    return pl.pallas_call(
        _out_proj_kernel,
        grid=(n // tm, d // tn),
        in_specs=[pl.BlockSpec((tm, half), lambda i, j: (i, 0)),
                  pl.BlockSpec((tm, half), lambda i, j: (i, 0)),
                  pl.BlockSpec((half, tn), lambda i, j: (0, j)),
                  pl.BlockSpec((half, tn), lambda i, j: (1, j)),
                  pl.BlockSpec((tm, tn), lambda i, j: (i, j)),
                  pl.BlockSpec((None, 1, tn), lambda i, j: (i // tiles_per_sample, 0, j))],
        out_specs=pl.BlockSpec((tm, tn), lambda i, j: (i, j)),
        out_shape=jax.ShapeDtypeStruct((n, d), F32),
        compiler_params=_params(vmem, ("arbitrary", "arbitrary")),
        name="out_proj",
    )(o, pm, w_bf16, w_bf16, x2d, g1)


def _norm2_router_kernel(x_ref, g_ref, sc_ref, sh_ref, rw_ref, h2_ref, aff_ref):
    h = _modulated_norm(x_ref[...], g_ref[...], sc_ref[...], sh_ref[...])
    h2_ref[...] = h
    h_hi, h_lo = _split_f32(h)
    r_hi, r_lo = _split_f32(rw_ref[...])
    h_hi, h_lo, r_hi, r_lo = (a.astype(BF16) for a in (h_hi, h_lo, r_hi, r_lo))
    logits = _dot_nt(r_hi, h_hi) + (_dot_nt(r_hi, h_lo) + _dot_nt(r_lo, h_hi))
    e = jnp.exp(logits - jnp.max(logits, axis=0, keepdims=True))
    aff_ref[...] = e / jnp.sum(e, axis=0, keepdims=True)


def _norm2_router(x1, g, sc, sh, router_wt):
    n, d = x1.shape
    tm = 256
    tiles_per_sample = SEQ // tm
    vmem = 2 * (2 * tm * d * 4 + N_EXPERTS * d * 4) + 6 * tm * d * 4
    return pl.pallas_call(
        _norm2_router_kernel,
        grid=(n // tm,),
        in_specs=[pl.BlockSpec((tm, d), lambda i: (i, 0)),
                  pl.BlockSpec((1, d), lambda i: (0, 0)),
                  pl.BlockSpec((None, 1, d), lambda i: (i // tiles_per_sample, 0, 0)),
                  pl.BlockSpec((None, 1, d), lambda i: (i // tiles_per_sample, 0, 0)),
                  pl.BlockSpec((N_EXPERTS, d), lambda i: (0, 0))],
        out_specs=[pl.BlockSpec((tm, d), lambda i: (i, 0)),
                   pl.BlockSpec((None, N_EXPERTS, tm), lambda i: (i // tiles_per_sample, 0, i % tiles_per_sample))],
        out_shape=[jax.ShapeDtypeStruct((n, d), F32),
                   jax.ShapeDtypeStruct((BATCH, N_EXPERTS, SEQ), F32)],
        compiler_params=_params(vmem, ("arbitrary",)),
        name="norm2_router",
    )(x1, g, sc, sh, router_wt)


ROUTE_COLS = 256
ROUTE_ROWS = 8
TOKEN_DIGIT_BITS = 6
TOKEN_DIGIT = 1 << TOKEN_DIGIT_BITS


def _route_kernel(aff_ref, slot_ref, gate_ref, idx_ref):
    a = aff_ref[...]
    rows, n = a.shape
    thr = jnp.zeros((rows, 1), I32)
    for b in range(30, -1, -1):
        cand = thr | (1 << b)
        cnt = jnp.sum((a >= pltpu.bitcast(cand, F32)).astype(I32), axis=-1, keepdims=True)
        thr = jnp.where(cnt >= CAPACITY, cand, thr)
    gt = a >= pltpu.bitcast(thr + 1, F32)
    eq = jnp.logical_and(a >= pltpu.bitcast(thr, F32), jnp.logical_not(gt))
    need = CAPACITY - jnp.sum(gt.astype(I32), axis=-1, keepdims=True)

    def prefix_count(mask):
        m = jnp.where(mask, 1.0, 0.0).astype(BF16)
        r = lax.broadcasted_iota(I32, (n, ROUTE_COLS), 0)
        c = lax.broadcasted_iota(I32, (n, ROUTE_COLS), 1)
        parts = [_dot(m, jnp.where(r < c + cb * ROUTE_COLS, 1.0, 0.0).astype(BF16))
                 for cb in range(n // ROUTE_COLS)]
        return jnp.concatenate(parts, axis=1).astype(I32)

    sel = jnp.logical_or(gt, jnp.logical_and(eq, prefix_count(eq) < need))
    slot = jnp.where(sel, prefix_count(sel), -1)
    slot_ref[...] = slot

    a_hi, a_rest = _split_f32(a)
    a_mid, a_lo = _split_f32(a_rest)
    tok = lax.broadcasted_iota(I32, (1, n), 1)
    tok_hi = (tok >> TOKEN_DIGIT_BITS).astype(F32)
    tok_lo = (tok & (TOKEN_DIGIT - 1)).astype(F32)
    slot_ids = lax.broadcasted_iota(I32, (CAPACITY, n), 0)
    pad = jnp.zeros((ROUTE_ROWS - 5, n), F32)
    for r in range(rows):
        onehot = jnp.where(slot[r:r + 1] == slot_ids, 1.0, 0.0).astype(BF16)
        pieces = jnp.concatenate([a_hi[r:r + 1], a_mid[r:r + 1], a_lo[r:r + 1], tok_hi, tok_lo, pad],
                                 axis=0).astype(BF16)
        res = _dot_nt(pieces, onehot)
        gate_ref[r:r + 1, :] = res[0:1] + res[1:2] + res[2:3]
        idx_ref[r:r + 1, :] = (res[3:4] * TOKEN_DIGIT + res[4:5]).astype(I32)


def _route(aff_rows):
    rows, n = aff_rows.shape
    vmem = 4 * ROUTE_ROWS * n * 4 + 4 * n * ROUTE_COLS * 4 + 4 * CAPACITY * n * 4
    return pl.pallas_call(
        _route_kernel,
        grid=(rows // ROUTE_ROWS,),
        in_specs=[pl.BlockSpec((ROUTE_ROWS, n), lambda i: (i, 0))],
        out_specs=[pl.BlockSpec((ROUTE_ROWS, n), lambda i: (i, 0)),
                   pl.BlockSpec((ROUTE_ROWS, CAPACITY), lambda i: (i, 0)),
                   pl.BlockSpec((ROUTE_ROWS, CAPACITY), lambda i: (i, 0))],
        out_shape=[jax.ShapeDtypeStruct((rows, n), I32),
                   jax.ShapeDtypeStruct((rows, CAPACITY), F32),
                   jax.ShapeDtypeStruct((rows, CAPACITY), I32)],
        compiler_params=_params(vmem, ("arbitrary",)),
        name="route",
    )(aff_rows)


GATHER_ISSUE_UNROLL = 8
CAST_ROWS = 64


def _moe_up_kernel(idx_ref, h2_hbm, w1_ref, w3_ref, hid_ref, rows32, rows16, sem):
    e, f = pl.program_id(0), pl.program_id(1)
    n_experts, n_ff = pl.num_programs(0), pl.num_programs(1)
    m = rows32.shape[0]
    share = m // n_ff

    def row_copy(src_row, dst_row):
        return pltpu.make_async_copy(h2_hbm.at[pl.ds(src_row, 1), :], rows32.at[pl.ds(dst_row, 1), :], sem)

    def start_rows(expert, first, count):
        def body(r, carry):
            row_copy(idx_ref[expert, first + r], first + r).start()
            return carry
        lax.fori_loop(0, count, body, 0, unroll=GATHER_ISSUE_UNROLL)

    @pl.when(jnp.logical_and(e == 0, f == 0))
    def _():
        start_rows(0, 0, m)

    @pl.when(f == 0)
    def _():
        def wait_row(r, carry):
            row_copy(0, r).wait()
            return carry
        lax.fori_loop(0, m, wait_row, 0, unroll=GATHER_ISSUE_UNROLL)

        def cast_rows(r, carry):
            rows = pl.ds(pl.multiple_of(r * CAST_ROWS, CAST_ROWS), CAST_ROWS)
            rows16[rows, :] = rows32[rows, :].astype(BF16)
            return carry
        lax.fori_loop(0, m // CAST_ROWS, cast_rows, 0)

    @pl.when(e + 1 < n_experts)
    def _():
        start_rows(e + 1, f * share, share)

    xg = rows16[...]
    a = _dot(xg, w1_ref[...].astype(BF16))
    b = _dot(xg, w3_ref[...].astype(BF16))
    hid_ref[...] = (a * jax.nn.sigmoid(a) * b).astype(BF16)


def _moe_up(idx_table, h2, w1, w3):
    ne, m = idx_table.shape
    d = h2.shape[1]
    ff = w1.shape[2]
    tf = 256
    vmem = m * d * (4 + 2) + 2 * 2 * d * tf * 4 + 2 * d * tf * 2 + 3 * m * tf * 4 + 2 * m * tf * 2
    w_spec = pl.BlockSpec((None, d, tf), lambda e, f, idx: (e, 0, f))
    return pl.pallas_call(
        _moe_up_kernel,
        grid_spec=pltpu.PrefetchScalarGridSpec(
            num_scalar_prefetch=1,
            grid=(ne, ff // tf),
            in_specs=[pl.BlockSpec(memory_space=pl.ANY), w_spec, w_spec],
            out_specs=pl.BlockSpec((None, m, tf), lambda e, f, idx: (e, 0, f)),
            scratch_shapes=[pltpu.VMEM((m, d), F32), pltpu.VMEM((m, d), BF16), pltpu.SemaphoreType.DMA(())]),
        out_shape=jax.ShapeDtypeStruct((ne, m, ff), BF16),
        compiler_params=_params(vmem, ("arbitrary", "arbitrary")),
        name="moe_up",
    )(idx_table, h2, w1, w3)


def _moe_down_kernel(hid_ref, w2_ref, gate_ref, y_ref):
    y = _dot(hid_ref[...], w2_ref[...].astype(BF16))
    y_ref[...] = (y * gate_ref[...]).astype(BF16)


def _moe_down(hid, w2, gate_col):
    ne, m, ff = hid.shape
    d = w2.shape[2]
    tn = 1024
    vmem = 2 * m * ff * 2 + 2 * ff * tn * 4 + ff * tn * 2 + 2 * m * 128 * 4 + 2 * m * tn * 4 + 2 * m * tn * 2
    return pl.pallas_call(
        _moe_down_kernel,
        grid=(ne, d // tn),
        in_specs=[pl.BlockSpec((None, m, ff), lambda e, j: (e, 0, 0)),
                  pl.BlockSpec((None, ff, tn), lambda e, j: (e, 0, j)),
                  pl.BlockSpec((None, m, 1), lambda e, j: (e, 0, 0))],
        out_specs=pl.BlockSpec((None, m, tn), lambda e, j: (e, 0, j)),
        out_shape=jax.ShapeDtypeStruct((ne, m, d), BF16),
        compiler_params=_params(vmem, ("arbitrary", "arbitrary")),
        name="moe_down",
    )(hid, w2, gate_col)


COMBINE_TOKENS = 128


def _moe_combine_kernel(slot_ref, y_ref, x1_ref, g2_ref, fg_ref, out_ref):
    st = slot_ref[...]
    slot_ids = lax.broadcasted_iota(I32, (COMBINE_TOKENS, CAPACITY), 1)
    onehot = jnp.concatenate(
        [jnp.where(st[:, e:e + 1] == slot_ids, 1.0, 0.0).astype(BF16) for e in range(N_EXPERTS)], axis=1)
    y = y_ref[...].reshape(N_EXPERTS * CAPACITY, y_ref.shape[-1])
    x2 = x1_ref[...] + g2_ref[...] * _dot(onehot, y)
    ms = jnp.mean(x2 * x2, axis=-1, keepdims=True)
    out_ref[...] = x2 * lax.rsqrt(ms + EPS) * fg_ref[...]


def _moe_combine(slot_ble, y, x1, g2, final_g):
    n, d = x1.shape
    tm = COMBINE_TOKENS
    tiles_per_sample = SEQ // tm
    vmem = (N_EXPERTS * CAPACITY * d * 2 + 2 * 2 * tm * d * 4 + 2 * tm * 128 * 4
            + tm * N_EXPERTS * CAPACITY * 2 + 3 * tm * d * 4)
    return pl.pallas_call(
        _moe_combine_kernel,
        grid=(BATCH, tiles_per_sample),
        in_specs=[pl.BlockSpec((None, tm, N_EXPERTS), lambda b, t: (b, t, 0)),
                  pl.BlockSpec((N_EXPERTS, None, CAPACITY, d), lambda b, t: (0, b, 0, 0),
                               pipeline_mode=pl.Buffered(1)),
                  pl.BlockSpec((tm, d), lambda b, t: (b * tiles_per_sample + t, 0)),
                  pl.BlockSpec((None, 1, d), lambda b, t: (b, 0, 0)),
                  pl.BlockSpec((1, d), lambda b, t: (0, 0))],
        out_specs=pl.BlockSpec((tm, d), lambda b, t: (b * tiles_per_sample + t, 0)),
        out_shape=jax.ShapeDtypeStruct((n, d), F32),
        compiler_params=_params(vmem, ("arbitrary", "arbitrary")),
        name="moe_combine",
    )(slot_ble, y, x1, g2, final_g)


def kernel(x, c, ctx, c_ctx, ada_w, ada_b, norm1_g, norm2_g, w_in, lb_param, hg_norm_g, pool_w, pool_scale,
           w_out, router_w, moe_w1, moe_w3, moe_w2, final_norm_g):
    nb, seq, d = x.shape
    assert (nb, seq, d) == (BATCH, SEQ, D_MODEL) and ctx.shape[1] == CTX_LEN
    assert ada_w.shape[0] == 1 and lb_param.shape[0] == 2, "single-layer block: layer 0 uses lower-bound row 0"
    x2d = x.reshape(nb * seq, d)
    ctx2d = ctx.reshape(nb * CTX_LEN, d)

    cvecs = jnp.zeros((ADA_ROWS, d), F32).at[:nb].set(c).at[nb].set(c_ctx)
    mod = _ada_mod(cvecs, ada_w[0], ada_b).reshape(ADA_ROWS, 6, d)
    sh1, sc1, g1, sh2, sc2, g2 = (mod[:nb, k][:, None, :] for k in range(6))
    csh1, csc1 = mod[nb:nb + 1, 0][:, None, :], mod[nb:nb + 1, 1][:, None, :]

    w_in_bf = w_in[0].astype(BF16)
    w_out_bf = w_out[0].astype(BF16)
    lbp = lb_param.reshape(lb_param.shape[0], 2 * HG_WIDTH)

    h_c = _norm1(ctx2d, norm1_g, csc1, csh1, rows_per_sample=nb * CTX_LEN)
    f_c, k_c, i_c = _in_proj(h_c, lbp, w_in_bf, n_cols=3 * HG_WIDTH)
    s_f, s_b = _ctx_states(f_c, k_c, i_c)

    h_x = _norm1(x2d, norm1_g, sc1, sh1, rows_per_sample=seq)
    fdec, kk, rest = _in_proj(h_x, lbp, w_in_bf, n_cols=w_in.shape[2])
    o = _hgrn2_scan(fdec, kk, rest, s_f, s_b, hg_norm_g)
    pm = _grid_pool(rest, pool_w[0], pool_scale)
    x1 = _out_proj(o, pm, w_out_bf, x2d, g1)

    h2, aff_t = _norm2_router(x1, norm2_g, sc2, sh2, router_w[0].T)
    slot, gates, tok = _route(aff_t.reshape(nb * N_EXPERTS, seq))
    slot = slot.reshape(nb, N_EXPERTS, seq)
    h2_rows = tok.reshape(nb, N_EXPERTS, CAPACITY) + (jnp.arange(nb, dtype=I32) * seq)[:, None, None]
    hid = _moe_up(h2_rows.transpose(1, 0, 2).reshape(N_EXPERTS, nb * CAPACITY), h2, moe_w1[0], moe_w3[0])
    gate_col = gates.reshape(nb, N_EXPERTS, CAPACITY).transpose(1, 0, 2).reshape(N_EXPERTS, nb * CAPACITY, 1)
    y = _moe_down(hid, moe_w2[0], gate_col).reshape(N_EXPERTS, nb, CAPACITY, d)
    out = _moe_combine(slot.transpose(0, 2, 1), y, x1, g2, final_norm_g[None, :])
    return out.reshape(nb, seq, d).astype(x.dtype)
```

```python
import jax
import jax.numpy as jnp
import numpy as np
from jax import lax
from jax.experimental import pallas as pl
from jax.experimental.pallas import tpu as pltpu

F32 = jnp.float32
BF16 = jnp.bfloat16
I32 = jnp.int32

D_MODEL = 4096
BATCH = 4
SEQ = 2048
GRID_W = 64
CTX_LEN = 256
HG_HEADS = 16
HG_HEAD_DIM = 128
HG_WIDTH = HG_HEADS * HG_HEAD_DIM
POOL_WIDTH = D_MODEL - HG_WIDTH
POOL_WINDOWS = (2, 4, 8, 16)
POOL_GROUP = POOL_WIDTH // len(POOL_WINDOWS)
N_EXPERTS = 16
EXPERT_FF = D_MODEL // 2
CAPACITY = 2 * SEQ // N_EXPERTS
EPS = 1e-6

V7X_VMEM_BYTES = 64 * 1024 * 1024
V7X_VMEM_HEADROOM_BYTES = 10 * 1024 * 1024
V7X_SUBLANES = 8
V7X_LANES = 128
NORM_ROWS = 16
ADA_ROWS = 8

SCAN_CHUNK = 64
SCAN_LEVELS = 6
SCAN_UNROLL = 4

_NT = (((1,), (1,)), ((), ()))
_TN = (((0,), (0,)), ((), ()))


def _dot(a, b):
    return jnp.dot(a, b, preferred_element_type=F32)


def _dot_nt(a, b):
    return lax.dot_general(a, b, _NT, preferred_element_type=F32)


def _dot_tn(a, b):
    return lax.dot_general(a, b, _TN, preferred_element_type=F32)


def _split_f32(x):
    hi = x.astype(BF16).astype(F32)
    return hi, x - hi


def _params(vmem_bytes, semantics):
    limit = min(int(vmem_bytes) + V7X_VMEM_HEADROOM_BYTES, V7X_VMEM_BYTES)
    return pltpu.CompilerParams(dimension_semantics=semantics, vmem_limit_bytes=limit)


def _ada_kernel(c_ref, w_ref, b_ref, o_ref):
    c = c_ref[...]
    s = c * jax.nn.sigmoid(c)
    hi, lo = _split_f32(s)
    lhs = jnp.concatenate([hi, lo], axis=0).astype(BF16)
    r = _dot(lhs, w_ref[...].astype(BF16))
    o_ref[...] = r[:ADA_ROWS] + r[ADA_ROWS:] + b_ref[...]


def _ada_mod(cvecs, w, b):
    d, n = w.shape
    tn = 512
    vmem = 2 * d * tn * 4 + d * tn * 2 + 4 * ADA_ROWS * d * 4
    return pl.pallas_call(
        _ada_kernel,
        grid=(n // tn,),
        in_specs=[pl.BlockSpec((ADA_ROWS, d), lambda j: (0, 0)),
                  pl.BlockSpec((d, tn), lambda j: (0, j)),
                  pl.BlockSpec((1, tn), lambda j: (0, j))],
        out_specs=pl.BlockSpec((ADA_ROWS, tn), lambda j: (0, j)),
        out_shape=jax.ShapeDtypeStruct((ADA_ROWS, n), F32),
        compiler_params=_params(vmem, ("arbitrary",)),
        name="ada_mod",
    )(cvecs, w, b)


def _modulated_norm(x, g, scale, shift):
    ms = jnp.mean(x * x, axis=-1, keepdims=True)
    return (x * lax.rsqrt(ms + EPS) * g) * (1.0 + scale) + shift


def _norm1_kernel(x_ref, g_ref, sc_ref, sh_ref, h_ref):
    g, sc, sh = g_ref[...], sc_ref[...], sh_ref[...]

    def norm_rows(r, carry):
        rows = pl.ds(pl.multiple_of(r * NORM_ROWS, NORM_ROWS), NORM_ROWS)
        h_ref[rows, :] = _modulated_norm(x_ref[rows, :], g, sc, sh).astype(BF16)
        return carry

    lax.fori_loop(0, x_ref.shape[0] // NORM_ROWS, norm_rows, 0, unroll=2)


def _norm1(x2d, g, sc, sh, *, rows_per_sample):
    n, d = x2d.shape
    tm = 256
    tiles_per_sample = rows_per_sample // tm
    mod = pl.BlockSpec((None, 1, d), lambda i: (i // tiles_per_sample, 0, 0))
    vmem = 2 * tm * d * (4 + 2) + 8 * NORM_ROWS * d * 4
    return pl.pallas_call(
        _norm1_kernel,
        grid=(n // tm,),
        in_specs=[pl.BlockSpec((tm, d), lambda i: (i, 0)), pl.BlockSpec((1, d), lambda i: (0, 0)), mod, mod],
        out_specs=pl.BlockSpec((tm, d), lambda i: (i, 0)),
        out_shape=jax.ShapeDtypeStruct((n, d), BF16),
        compiler_params=_params(vmem, ("arbitrary",)),
        name="norm1",
    )(x2d, g, sc, sh)


def _gate_proj_kernel(h_ref, lbp_ref, w_ref, f_ref, k_ref):
    z = _dot(h_ref[...], w_ref[...])
    p = lbp_ref[...]
    e = jnp.exp(p - jnp.max(p, axis=0, keepdims=True))
    lb = e[0:1] / jnp.sum(e, axis=0, keepdims=True)
    f = lb + (1.0 - lb) * jax.nn.sigmoid(z)
    f_ref[...] = f
    k_ref[...] = (1.0 - f).astype(BF16)


def _rest_proj_kernel(h_ref, w_ref, r_ref):
    r_ref[...] = _dot(h_ref[...], w_ref[...]).astype(BF16)


def _in_proj(h, lbp, w_bf16, *, n_cols):
    n, d = h.shape
    tm = 1024
    tn_gate, tn = 512, 1024
    n_gate = 2 * HG_WIDTH // tn
    vmem = 2 * tm * d * 2 + 2 * d * tn_gate * 2 + 2 * tm * tn_gate * (4 + 2) + 4 * tm * tn_gate * 4
    fdec, kk = pl.pallas_call(
        _gate_proj_kernel,
        grid=(n // tm, 2 * HG_WIDTH // tn_gate),
        in_specs=[pl.BlockSpec((tm, d), lambda i, j: (i, 0)),
                  pl.BlockSpec((2, tn_gate), lambda i, j: (0, j)),
                  pl.BlockSpec((d, tn_gate), lambda i, j: (0, j))],
        out_specs=[pl.BlockSpec((tm, tn_gate), lambda i, j: (i, j)),
                   pl.BlockSpec((tm, tn_gate), lambda i, j: (i, j))],
        out_shape=[jax.ShapeDtypeStruct((n, 2 * HG_WIDTH), F32),
                   jax.ShapeDtypeStruct((n, 2 * HG_WIDTH), BF16)],
        compiler_params=_params(vmem, ("arbitrary", "arbitrary")),
        name="gate_proj",
    )(h, lbp, w_bf16)
    vmem = 2 * tm * d * 2 + 2 * d * tn * 2 + 2 * tm * tn * 2 + 2 * tm * tn * 4
    rest = pl.pallas_call(
        _rest_proj_kernel,
        grid=(n // tm, (n_cols - 2 * HG_WIDTH) // tn),
        in_specs=[pl.BlockSpec((tm, d), lambda i, j: (i, 0)),
                  pl.BlockSpec((d, tn), lambda i, j: (0, n_gate + j))],
        out_specs=pl.BlockSpec((tm, tn), lambda i, j: (i, j)),
        out_shape=jax.ShapeDtypeStruct((n, n_cols - 2 * HG_WIDTH), BF16),
        compiler_params=_params(vmem, ("arbitrary", "arbitrary")),
        name="rest_proj",
    )(h, w_bf16)
    return fdec, kk, rest


def _ctx_tri():
    s = np.arange(CTX_LEN)[:, None]
    u = np.arange(CTX_LEN)[None, :]
    tri = np.stack([u > s, u < s]).astype(np.float32)
    return jnp.asarray(np.concatenate([tri, tri], axis=2), dtype=BF16)


def _ctx_state_kernel(ff_ref, fb_ref, kf_ref, kb_ref, i_ref, tri_ref, sf_ref, sb_ref):
    v = i_ref[...]

    def state(f, k, a):
        hi, lo = _split_f32(jnp.log(f))
        g = _dot(a, jnp.concatenate([hi, lo], axis=0).astype(BF16))
        kd = (k.astype(F32) * jnp.exp(g)).astype(BF16)
        return _dot_tn(v, kd)

    sf_ref[...] = state(ff_ref[...], kf_ref[...], tri_ref[0])
    sb_ref[...] = state(fb_ref[...], kb_ref[...], tri_ref[1])


def _ctx_states(fdec, kk, vi):
    t, dh, nh = CTX_LEN, HG_HEAD_DIM, HG_HEADS
    blk = lambda off: pl.BlockSpec((t, dh), lambda b, h: (b, off + h))
    st = pl.BlockSpec((None, None, dh, dh), lambda b, h: (b, h, 0, 0))
    shape = jax.ShapeDtypeStruct((BATCH, nh, dh, dh), F32)
    vmem = 2 * (2 * t * dh * 4 + 3 * t * dh * 2 + 2 * dh * dh * 4) + 2 * 2 * t * 2 * t * 2
    return pl.pallas_call(
        _ctx_state_kernel,
        grid=(BATCH, nh),
        in_specs=[blk(0), blk(nh), blk(0), blk(nh), blk(0),
                  pl.BlockSpec((2, t, 2 * t), lambda b, h: (0, 0, 0))],
        out_specs=[st, st],
        out_shape=[shape, shape],
        compiler_params=_params(vmem, ("arbitrary", "arbitrary")),
        name="ctx_state",
    )(fdec, fdec, kk, kk, vi, _ctx_tri())


def _pair_levels():
    c, nl = SCAN_CHUNK, SCAN_LEVELS
    t = np.arange(c)[:, None]
    u = np.arange(c)[None, :]
    lev = np.floor(np.log2(np.maximum(t ^ u, 1))).astype(np.int32)
    lidx = np.stack([np.where(t > u, lev, np.where(t == u, nl, -1)),
                     np.where(t < u, lev, np.where(t == u, nl, -1))]).astype(np.int32)
    return jnp.asarray(lidx)


def _swap_halves(x, h, t_idx):
    c = x.shape[0]
    if h >= V7X_SUBLANES:
        tiles = [x[i * h:(i + 1) * h] for i in range(c // h)]
        return jnp.concatenate([tiles[i ^ 1] for i in range(len(tiles))], axis=0)
    x3 = x.reshape(c // V7X_SUBLANES, V7X_SUBLANES, x.shape[1])
    if 2 * h == V7X_SUBLANES:
        return pltpu.roll(x3, h, axis=1).reshape(x.shape)
    up = pltpu.roll(x3, h, axis=1).reshape(x.shape)
    down = pltpu.roll(x3, V7X_SUBLANES - h, axis=1).reshape(x.shape)
    return jnp.where((t_idx & h) != 0, up, down)


def _scan_chunk(direction, f, k, q, v, st_ref, lidx, t_idx):
    c, nl = SCAN_CHUNK, SCAN_LEVELS
    qf = q.astype(F32)
    kf = k.astype(F32)
    q_dec = f
    k_dec = jnp.ones_like(f)
    total = f
    scores = jnp.zeros((c, c), F32)
    for l in range(nl):
        h = 1 << l
        bit = (t_idx & h) != 0
        is_query = bit if direction == 0 else jnp.logical_not(bit)
        x = jnp.where(is_query, q_dec * qf, k_dec * kf).astype(BF16)
        scores = jnp.where(lidx == l, _dot_nt(x, x), scores)
        other = _swap_halves(total, h, t_idx)
        q_dec = q_dec * jnp.where(is_query, other, 1.0)
        k_dec = k_dec * jnp.where(is_query, 1.0, other)
        total = total * other
    own = jnp.sum(qf * kf, axis=-1, keepdims=True)
    q_in = (q_dec * qf).astype(BF16)
    k_st = (k_dec * kf).astype(BF16)
    st = st_ref[...]
    o = _dot(scores.astype(BF16), v) + _dot_nt(q_in, st.astype(BF16)) + own * v.astype(F32)
    st_ref[...] = total[0:1] * st + _dot_tn(v, k_st)
    return o


def _scan_kernel(ff_ref, fb_ref, kf_ref, kb_ref, i_ref, q_ref, g_ref, s0f_ref, s0b_ref, ng_ref,
                 lidx_ref, o_ref, stf_ref, stb_ref, part_ref):
    c = SCAN_CHUNK
    n_chunks = SEQ // c
    stf_ref[...] = s0f_ref[...]
    stb_ref[...] = s0b_ref[...]
    t_idx = lax.broadcasted_iota(I32, (c, HG_HEAD_DIM), 0)
    ng = ng_ref[...]

    def chunk(direction, r0):
        rows = pl.ds(r0, c)
        f_ref, k_ref, st_ref = (ff_ref, kf_ref, stf_ref) if direction == 0 else (fb_ref, kb_ref, stb_ref)
        return _scan_chunk(direction, f_ref[rows, :], k_ref[rows, :], q_ref[rows, :], i_ref[rows, :],
                           st_ref, lidx_ref[direction], t_idx)

    def finalize(o, r0):
        rows = pl.ds(r0, c)
        gate = g_ref[rows, :].astype(F32)
        ms = jnp.mean(o * o, axis=-1, keepdims=True)
        o_ref[rows, :] = (o * lax.rsqrt(ms + EPS) * ng * (gate * jax.nn.sigmoid(gate))).astype(BF16)

    def starts(n):
        return pl.multiple_of(n * c, c), pl.multiple_of((n_chunks - 1 - n) * c, c)

    def first_half(n, carry):
        rf, rb = starts(n)
        part_ref[pl.ds(rf, c), :] = chunk(0, rf)
        part_ref[pl.ds(rb, c), :] = chunk(1, rb)
        return carry

    def second_half(n, carry):
        rf, rb = starts(n)
        finalize(chunk(0, rf) + part_ref[pl.ds(rf, c), :], rf)
        finalize(chunk(1, rb) + part_ref[pl.ds(rb, c), :], rb)
        return carry

    lax.fori_loop(0, n_chunks // 2, first_half, 0, unroll=SCAN_UNROLL)
    lax.fori_loop(n_chunks // 2, n_chunks, second_half, 0, unroll=SCAN_UNROLL)


def _hgrn2_scan(fdec, kk, rest, s0f, s0b, norm_g):
    dh, nh = HG_HEAD_DIM, HG_HEADS
    lidx = _pair_levels()
    blk = lambda off: pl.BlockSpec((SEQ, dh), lambda b, h: (b, off + h))
    st = pl.BlockSpec((None, None, dh, dh), lambda b, h: (b, h, 0, 0))
    vmem = (2 * (2 * SEQ * dh * 4 + 6 * SEQ * dh * 2 + 2 * dh * dh * 4) + SEQ * dh * 4 + 2 * dh * dh * 4
            + 2 * 2 * SCAN_CHUNK * V7X_LANES * 4)
    return pl.pallas_call(
        _scan_kernel,
        grid=(BATCH, nh),
        in_specs=[blk(0), blk(nh), blk(0), blk(nh), blk(0), blk(nh), blk(2 * nh), st, st,
                  pl.BlockSpec((1, dh), lambda b, h: (0, h)),
                  pl.BlockSpec(lidx.shape, lambda b, h: (0, 0, 0))],
        out_specs=blk(0),
        out_shape=jax.ShapeDtypeStruct((BATCH * SEQ, HG_WIDTH), BF16),
        scratch_shapes=[pltpu.VMEM((dh, dh), F32), pltpu.VMEM((dh, dh), F32), pltpu.VMEM((SEQ, dh), F32)],
        compiler_params=_params(vmem, ("arbitrary", "arbitrary")),
        name="hgrn2_scan",
    )(fdec, fdec, kk, kk, rest, rest, rest, s0f, s0b, norm_g, lidx)


POOL_PIECE = 256


def _box_bounds(n, w):
    start = np.arange(n) - w // 2
    return np.clip(start, 0, n), np.clip(start + w, 0, n)


def _pool_consts():
    rows = SEQ // GRID_W
    col_box = np.zeros((len(POOL_WINDOWS), POOL_PIECE, POOL_PIECE), np.float32)
    cnt = np.zeros((len(POOL_WINDOWS), SEQ, 1), np.float32)
    cc = np.arange(GRID_W)[None, :]
    for gi, w in enumerate(POOL_WINDOWS):
        c0, c1 = _box_bounds(GRID_W, w)
        r0, r1 = _box_bounds(rows, w)
        wc = ((cc >= c0[:, None]) & (cc < c1[:, None])).astype(np.float32)
        col_box[gi] = np.kron(np.eye(POOL_PIECE // GRID_W, dtype=np.float32), wc)
        cnt[gi] = ((r1 - r0)[:, None] * (c1 - c0)[None, :]).reshape(SEQ, 1)
    return jnp.asarray(col_box, dtype=BF16), jnp.asarray(cnt)


def _pool_kernel(v_ref, box_ref, cnt_ref, pw_ref, ps_ref, o_ref, pre_ref, diff_ref):
    gi = pl.program_id(1)
    rows = SEQ // GRID_W
    gw = GRID_W
    box = box_ref[...]
    pre_ref[0:gw, :] = jnp.zeros((gw, POOL_GROUP), F32)
    for p in range(SEQ // POOL_PIECE):
        yc = _dot(box, v_ref[p * POOL_PIECE:(p + 1) * POOL_PIECE, :])
        for rr in range(POOL_PIECE // gw):
            r = p * (POOL_PIECE // gw) + rr
            pre_ref[(r + 1) * gw:(r + 2) * gw, :] = pre_ref[r * gw:(r + 1) * gw, :] + yc[rr * gw:(rr + 1) * gw]
    for k, w in enumerate(POOL_WINDOWS):
        @pl.when(gi == k)
        def _(w=w):
            r0, r1 = _box_bounds(rows, w)
            for r in range(rows):
                sl = slice(r * gw, (r + 1) * gw)
                box_sum = pre_ref[int(r1[r]) * gw:(int(r1[r]) + 1) * gw, :] - pre_ref[int(r0[r]) * gw:(int(r0[r]) + 1) * gw, :]
                diff_ref[sl, :] = (box_sum / cnt_ref[sl, :] - v_ref[sl, :].astype(F32)).astype(BF16)
    o_ref[...] = (_dot(diff_ref[...], pw_ref[...].astype(BF16)) * ps_ref[...]).astype(BF16)


def _grid_pool(rest, pool_w, pool_scale):
    ng, pg = len(POOL_WINDOWS), POOL_GROUP
    col_box, cnt = _pool_consts()
    v_off = 3 * HG_WIDTH // pg
    vmem = (2 * (2 * SEQ * pg * 2 + POOL_PIECE * POOL_PIECE * 2 + SEQ * 128 * 4 + pg * pg * 4)
            + (SEQ + GRID_W) * pg * 4 + SEQ * pg * 2 + SEQ * pg * 4)
    return pl.pallas_call(
        _pool_kernel,
        grid=(BATCH, ng),
        in_specs=[pl.BlockSpec((SEQ, pg), lambda b, k: (b, v_off + k)),
                  pl.BlockSpec((None, POOL_PIECE, POOL_PIECE), lambda b, k: (k, 0, 0)),
                  pl.BlockSpec((None, SEQ, 1), lambda b, k: (k, 0, 0)),
                  pl.BlockSpec((None, pg, pg), lambda b, k: (k, 0, 0)),
                  pl.BlockSpec((1, pg), lambda b, k: (0, k))],
        out_specs=pl.BlockSpec((SEQ, pg), lambda b, k: (b, k)),
        out_shape=jax.ShapeDtypeStruct((BATCH * SEQ, POOL_WIDTH), BF16),
        scratch_shapes=[pltpu.VMEM((SEQ + GRID_W, pg), F32), pltpu.VMEM((SEQ, pg), BF16)],
        compiler_params=_params(vmem, ("arbitrary", "arbitrary")),
        name="grid_pool",
    )(rest, col_box, cnt, pool_w, pool_scale)


def _out_proj_kernel(o_ref, pm_ref, wa_ref, wb_ref, x_ref, g1_ref, out_ref):
    acc = _dot(o_ref[...], wa_ref[...]) + _dot(pm_ref[...], wb_ref[...])
    out_ref[...] = x_ref[...] + g1_ref[...] * acc


def _out_proj(o, pm, w_bf16, x2d, g1):
    n, d = x2d.shape
    tm, tn = 1024, 512
    half = w_bf16.shape[0] // 2
    tiles_per_sample = SEQ // tm
    vmem = 2 * (2 * tm * half * 2 + 2 * half * tn * 2 + 2 * tm * tn * 4) + tm * tn * 4
    return pl.pallas_call(
        _out_proj_kernel,
        grid=(n // tm, d // tn),
        in_specs=[pl.BlockSpec((tm, half), lambda i, j: (i, 0)),
                  pl.BlockSpec((tm, half), lambda i, j: (i, 0)),
                  pl.BlockSpec((half, tn), lambda i, j: (0, j)),
                  pl.BlockSpec((half, tn), lambda i, j: (1, j)),
                  pl.BlockSpec((tm, tn), lambda i, j: (i, j)),
                  pl.BlockSpec((None, 1, tn), lambda i, j: (i // tiles_per_sample, 0, j))],
        out_specs=pl.BlockSpec((tm, tn), lambda i, j: (i, j)),
        out_shape=jax.ShapeDtypeStruct((n, d), F32),
        compiler_params=_params(vmem, ("arbitrary", "arbitrary")),
        name="out_proj",
    )(o, pm, w_bf16, w_bf16, x2d, g1)


def _norm2_router_kernel(x_ref, g_ref, sc_ref, sh_ref, rw_ref, h2_ref, aff_ref):
    h = _modulated_norm(x_ref[...], g_ref[...], sc_ref[...], sh_ref[...])
    h2_ref[...] = h
    h_hi, h_lo = _split_f32(h)
    r_hi, r_lo = _split_f32(rw_ref[...])
    h_hi, h_lo, r_hi, r_lo = (a.astype(BF16) for a in (h_hi, h_lo, r_hi, r_lo))
    logits = _dot_nt(r_hi, h_hi) + (_dot_nt(r_hi, h_lo) + _dot_nt(r_lo, h_hi))
    e = jnp.exp(logits - jnp.max(logits, axis=0, keepdims=True))
    aff_ref[...] = e / jnp.sum(e, axis=0, keepdims=True)


def _norm2_router(x1, g, sc, sh, router_wt):
    n, d = x1.shape
    tm = 256
    tiles_per_sample = SEQ // tm
    vmem = 2 * (2 * tm * d * 4 + N_EXPERTS * d * 4) + 6 * tm * d * 4
    return pl.pallas_call(
        _norm2_router_kernel,
        grid=(n // tm,),
        in_specs=[pl.BlockSpec((tm, d), lambda i: (i, 0)),
                  pl.BlockSpec((1, d), lambda i: (0, 0)),
                  pl.BlockSpec((None, 1, d), lambda i: (i // tiles_per_sample, 0, 0)),
                  pl.BlockSpec((None, 1, d), lambda i: (i // tiles_per_sample, 0, 0)),
                  pl.BlockSpec((N_EXPERTS, d), lambda i: (0, 0))],
        out_specs=[pl.BlockSpec((tm, d), lambda i: (i, 0)),
                   pl.BlockSpec((None, N_EXPERTS, tm), lambda i: (i // tiles_per_sample, 0, i % tiles_per_sample))],
        out_shape=[jax.ShapeDtypeStruct((n, d), F32),
                   jax.ShapeDtypeStruct((BATCH, N_EXPERTS, SEQ), F32)],
        compiler_params=_params(vmem, ("arbitrary",)),
        name="norm2_router",
    )(x1, g, sc, sh, router_wt)


ROUTE_COLS = 256
ROUTE_ROWS = 8
TOKEN_DIGIT_BITS = 6
TOKEN_DIGIT = 1 << TOKEN_DIGIT_BITS


def _route_kernel(aff_ref, slot_ref, gate_ref, idx_ref):
    a = aff_ref[...]
    rows, n = a.shape
    thr = jnp.zeros((rows, 1), I32)
    for b in range(30, -1, -1):
        cand = thr | (1 << b)
        cnt = jnp.sum((a >= pltpu.bitcast(cand, F32)).astype(I32), axis=-1, keepdims=True)
        thr = jnp.where(cnt >= CAPACITY, cand, thr)
    gt = a >= pltpu.bitcast(thr + 1, F32)
    eq = jnp.logical_and(a >= pltpu.bitcast(thr, F32), jnp.logical_not(gt))
    need = CAPACITY - jnp.sum(gt.astype(I32), axis=-1, keepdims=True)

    def prefix_count(mask):
        m = jnp.where(mask, 1.0, 0.0).astype(BF16)
        r = lax.broadcasted_iota(I32, (n, ROUTE_COLS), 0)
        c = lax.broadcasted_iota(I32, (n, ROUTE_COLS), 1)
        parts = [_dot(m, jnp.where(r < c + cb * ROUTE_COLS, 1.0, 0.0).astype(BF16))
                 for cb in range(n // ROUTE_COLS)]
        return jnp.concatenate(parts, axis=1).astype(I32)

    sel = jnp.logical_or(gt, jnp.logical_and(eq, prefix_count(eq) < need))
    slot = jnp.where(sel, prefix_count(sel), -1)
    slot_ref[...] = slot

    a_hi, a_rest = _split_f32(a)
    a_mid, a_lo = _split_f32(a_rest)
    tok = lax.broadcasted_iota(I32, (1, n), 1)
    tok_hi = (tok >> TOKEN_DIGIT_BITS).astype(F32)
    tok_lo = (tok & (TOKEN_DIGIT - 1)).astype(F32)
    slot_ids = lax.broadcasted_iota(I32, (CAPACITY, n), 0)
    pad = jnp.zeros((ROUTE_ROWS - 5, n), F32)
    for r in range(rows):
        onehot = jnp.where(slot[r:r + 1] == slot_ids, 1.0, 0.0).astype(BF16)
        pieces = jnp.concatenate([a_hi[r:r + 1], a_mid[r:r + 1], a_lo[r:r + 1], tok_hi, tok_lo, pad],
                                 axis=0).astype(BF16)
        res = _dot_nt(pieces, onehot)
        gate_ref[r:r + 1, :] = res[0:1] + res[1:2] + res[2:3]
        idx_ref[r:r + 1, :] = (res[3:4] * TOKEN_DIGIT + res[4:5]).astype(I32)


def _route(aff_rows):
    rows, n = aff_rows.shape
    vmem = 4 * ROUTE_ROWS * n * 4 + 4 * n * ROUTE_COLS * 4 + 4 * CAPACITY * n * 4
    return pl.pallas_call(
        _route_kernel,
        grid=(rows // ROUTE_ROWS,),
        in_specs=[pl.BlockSpec((ROUTE_ROWS, n), lambda i: (i, 0))],
        out_specs=[pl.BlockSpec((ROUTE_ROWS, n), lambda i: (i, 0)),
                   pl.BlockSpec((ROUTE_ROWS, CAPACITY), lambda i: (i, 0)),
                   pl.BlockSpec((ROUTE_ROWS, CAPACITY), lambda i: (i, 0))],
        out_shape=[jax.ShapeDtypeStruct((rows, n), I32),
                   jax.ShapeDtypeStruct((rows, CAPACITY), F32),
                   jax.ShapeDtypeStruct((rows, CAPACITY), I32)],
        compiler_params=_params(vmem, ("arbitrary",)),
        name="route",
    )(aff_rows)


CAST_ROWS = 64


def _moe_up_kernel(idx_ref, h2_hbm, w1_ref, w3_ref, hid_ref, rows32, rows16, sem):
    e, f = pl.program_id(0), pl.program_id(1)
    n_experts = pl.num_programs(0)
    n_ff, share, _ = rows32.shape
    m = n_ff * share

    def row_copy(src_row, group, r):
        return pltpu.make_async_copy(h2_hbm.at[pl.ds(src_row, 1), :], rows32.at[group, pl.ds(r, 1), :], sem)

    def start_group(expert, group):
        first = expert * m + group * share
        for r in range(share):
            row_copy(idx_ref[first + r], group, r).start()

    @pl.when(jnp.logical_and(e == 0, f == 0))
    def _():
        lax.fori_loop(0, n_ff, lambda grp, carry: (start_group(0, grp), carry)[1], 0)

    @pl.when(f == 0)
    def _():
        def wait_group(grp, carry):
            for r in range(share):
                row_copy(0, grp, r).wait()
            return carry
        lax.fori_loop(0, n_ff, wait_group, 0)

        def cast_group(grp, carry):
            for c0 in range(0, share, CAST_ROWS):
                dst = pl.ds(pl.multiple_of(grp * share + c0, CAST_ROWS), CAST_ROWS)
                rows16[dst, :] = rows32[grp, c0:c0 + CAST_ROWS, :].astype(BF16)
            return carry
        lax.fori_loop(0, n_ff, cast_group, 0)

    @pl.when(e + 1 < n_experts)
    def _():
        start_group(e + 1, f)

    xg = rows16[...]
    a = _dot(xg, w1_ref[...].astype(BF16))
    b = _dot(xg, w3_ref[...].astype(BF16))
    hid_ref[...] = (a * jax.nn.sigmoid(a) * b).astype(BF16)


def _moe_up(idx_table, h2, w1, w3):
    ne, m = idx_table.shape
    d = h2.shape[1]
    ff = w1.shape[2]
    tf = 256
    vmem = m * d * (4 + 2) + 2 * 2 * d * tf * 4 + 2 * d * tf * 2 + 3 * m * tf * 4 + 2 * m * tf * 2
    n_ff = ff // tf
    w_spec = pl.BlockSpec((None, d, tf), lambda e, f, idx: (e, 0, f))
    return pl.pallas_call(
        _moe_up_kernel,
        grid_spec=pltpu.PrefetchScalarGridSpec(
            num_scalar_prefetch=1,
            grid=(ne, n_ff),
            in_specs=[pl.BlockSpec(memory_space=pl.ANY), w_spec, w_spec],
            out_specs=pl.BlockSpec((None, m, tf), lambda e, f, idx: (e, 0, f)),
            scratch_shapes=[pltpu.VMEM((n_ff, m // n_ff, d), F32), pltpu.VMEM((m, d), BF16),
                            pltpu.SemaphoreType.DMA(())]),
        out_shape=jax.ShapeDtypeStruct((ne, m, ff), BF16),
        compiler_params=_params(vmem, ("arbitrary", "arbitrary")),
        name="moe_up",
    )(idx_table.reshape(ne * m), h2, w1, w3)


def _moe_down_kernel(hid_ref, w2_ref, gate_ref, y_ref):
    y = _dot(hid_ref[...], w2_ref[...].astype(BF16))
    y_ref[...] = (y * gate_ref[...]).astype(BF16)


def _moe_down(hid, w2, gate_col):
    ne, m, ff = hid.shape
    d = w2.shape[2]
    tn = 1024
    vmem = 2 * m * ff * 2 + 2 * ff * tn * 4 + ff * tn * 2 + 2 * m * 128 * 4 + 2 * m * tn * 4 + 2 * m * tn * 2
    return pl.pallas_call(
        _moe_down_kernel,
        grid=(ne, d // tn),
        in_specs=[pl.BlockSpec((None, m, ff), lambda e, j: (e, 0, 0)),
                  pl.BlockSpec((None, ff, tn), lambda e, j: (e, 0, j)),
                  pl.BlockSpec((None, m, 1), lambda e, j: (e, 0, 0))],
        out_specs=pl.BlockSpec((None, m, tn), lambda e, j: (e, 0, j)),
        out_shape=jax.ShapeDtypeStruct((ne, m, d), BF16),
        compiler_params=_params(vmem, ("arbitrary", "arbitrary")),
        name="moe_down",
    )(hid, w2, gate_col)


COMBINE_TOKENS = 128


def _moe_combine_kernel(slot_ref, y_ref, x1_ref, g2_ref, fg_ref, out_ref):
    st = slot_ref[...]
    slot_ids = lax.broadcasted_iota(I32, (COMBINE_TOKENS, CAPACITY), 1)
    onehot = jnp.concatenate(
        [jnp.where(st[:, e:e + 1] == slot_ids, 1.0, 0.0).astype(BF16) for e in range(N_EXPERTS)], axis=1)
    y = y_ref[...].reshape(N_EXPERTS * CAPACITY, y_ref.shape[-1])
    x2 = x1_ref[...] + g2_ref[...] * _dot(onehot, y)
    ms = jnp.mean(x2 * x2, axis=-1, keepdims=True)
    out_ref[...] = x2 * lax.rsqrt(ms + EPS) * fg_ref[...]


def _moe_combine(slot_ble, y, x1, g2, final_g):
    n, d = x1.shape
    tm = COMBINE_TOKENS
    tiles_per_sample = SEQ // tm
    vmem = (N_EXPERTS * CAPACITY * d * 2 + 2 * 2 * tm * d * 4 + 2 * tm * 128 * 4
            + tm * N_EXPERTS * CAPACITY * 2 + 3 * tm * d * 4)
    return pl.pallas_call(
        _moe_combine_kernel,
        grid=(BATCH, tiles_per_sample),
        in_specs=[pl.BlockSpec((None, tm, N_EXPERTS), lambda b, t: (b, t, 0)),
                  pl.BlockSpec((N_EXPERTS, None, CAPACITY, d), lambda b, t: (0, b, 0, 0),
                               pipeline_mode=pl.Buffered(1)),
                  pl.BlockSpec((tm, d), lambda b, t: (b * tiles_per_sample + t, 0)),
                  pl.BlockSpec((None, 1, d), lambda b, t: (b, 0, 0)),
                  pl.BlockSpec((1, d), lambda b, t: (0, 0))],
        out_specs=pl.BlockSpec((tm, d), lambda b, t: (b * tiles_per_sample + t, 0)),
        out_shape=jax.ShapeDtypeStruct((n, d), F32),
        compiler_params=_params(vmem, ("arbitrary", "arbitrary")),
        name="moe_combine",
    )(slot_ble, y, x1, g2, final_g)


def kernel(x, c, ctx, c_ctx, ada_w, ada_b, norm1_g, norm2_g, w_in, lb_param, hg_norm_g, pool_w, pool_scale,
           w_out, router_w, moe_w1, moe_w3, moe_w2, final_norm_g):
    nb, seq, d = x.shape
    assert (nb, seq, d) == (BATCH, SEQ, D_MODEL) and ctx.shape[1] == CTX_LEN
    assert ada_w.shape[0] == 1 and lb_param.shape[0] == 2, "single-layer block: layer 0 uses lower-bound row 0"
    x2d = x.reshape(nb * seq, d)
    ctx2d = ctx.reshape(nb * CTX_LEN, d)

    cvecs = jnp.zeros((ADA_ROWS, d), F32).at[:nb].set(c).at[nb].set(c_ctx)
    mod = _ada_mod(cvecs, ada_w[0], ada_b).reshape(ADA_ROWS, 6, d)
    sh1, sc1, g1, sh2, sc2, g2 = (mod[:nb, k][:, None, :] for k in range(6))
    csh1, csc1 = mod[nb:nb + 1, 0][:, None, :], mod[nb:nb + 1, 1][:, None, :]

    w_in_bf = w_in[0].astype(BF16)
    w_out_bf = w_out[0].astype(BF16)
    lbp = lb_param.reshape(lb_param.shape[0], 2 * HG_WIDTH)

    h_c = _norm1(ctx2d, norm1_g, csc1, csh1, rows_per_sample=nb * CTX_LEN)
    f_c, k_c, i_c = _in_proj(h_c, lbp, w_in_bf, n_cols=3 * HG_WIDTH)
    s_f, s_b = _ctx_states(f_c, k_c, i_c)

    h_x = _norm1(x2d, norm1_g, sc1, sh1, rows_per_sample=seq)
    fdec, kk, rest = _in_proj(h_x, lbp, w_in_bf, n_cols=w_in.shape[2])
    o = _hgrn2_scan(fdec, kk, rest, s_f, s_b, hg_norm_g)
    pm = _grid_pool(rest, pool_w[0], pool_scale)
    x1 = _out_proj(o, pm, w_out_bf, x2d, g1)

    h2, aff_t = _norm2_router(x1, norm2_g, sc2, sh2, router_w[0].T)
    slot, gates, tok = _route(aff_t.reshape(nb * N_EXPERTS, seq))
    slot = slot.reshape(nb, N_EXPERTS, seq)
    h2_rows = tok.reshape(nb, N_EXPERTS, CAPACITY) + (jnp.arange(nb, dtype=I32) * seq)[:, None, None]
    hid = _moe_up(h2_rows.transpose(1, 0, 2).reshape(N_EXPERTS, nb * CAPACITY), h2, moe_w1[0], moe_w3[0])
    gate_col = gates.reshape(nb, N_EXPERTS, CAPACITY).transpose(1, 0, 2).reshape(N_EXPERTS, nb * CAPACITY, 1)
    y = _moe_down(hid, moe_w2[0], gate_col).reshape(N_EXPERTS, nb, CAPACITY, d)
    out = _moe_combine(slot.transpose(0, 2, 1), y, x1, g2, final_norm_g[None, :])
    return out.reshape(nb, seq, d).astype(x.dtype)
```

```python
import jax
import jax.numpy as jnp
import numpy as np
from jax import lax
from jax.experimental import pallas as pl
from jax.experimental.pallas import tpu as pltpu

F32 = jnp.float32
BF16 = jnp.bfloat16
I32 = jnp.int32

D_MODEL = 4096
BATCH = 4
SEQ = 2048
GRID_W = 64
CTX_LEN = 256
HG_HEADS = 16
HG_HEAD_DIM = 128
HG_WIDTH = HG_HEADS * HG_HEAD_DIM
POOL_WIDTH = D_MODEL - HG_WIDTH
POOL_WINDOWS = (2, 4, 8, 16)
POOL_GROUP = POOL_WIDTH // len(POOL_WINDOWS)
N_EXPERTS = 16
EXPERT_FF = D_MODEL // 2
CAPACITY = 2 * SEQ // N_EXPERTS
EPS = 1e-6

V7X_VMEM_BYTES = 64 * 1024 * 1024
V7X_VMEM_HEADROOM_BYTES = 10 * 1024 * 1024
V7X_SUBLANES = 8
V7X_LANES = 128
NORM_ROWS = 16
ADA_ROWS = 8

SCAN_CHUNK = 128
SCAN_LEVELS = 7
SCAN_UNROLL = 2

_NT = (((1,), (1,)), ((), ()))
_TN = (((0,), (0,)), ((), ()))


def _dot(a, b):
    return jnp.dot(a, b, preferred_element_type=F32)


def _dot_nt(a, b):
    return lax.dot_general(a, b, _NT, preferred_element_type=F32)


def _dot_tn(a, b):
    return lax.dot_general(a, b, _TN, preferred_element_type=F32)


def _split_f32(x):
    hi = x.astype(BF16).astype(F32)
    return hi, x - hi


def _params(vmem_bytes, semantics):
    limit = min(int(vmem_bytes) + V7X_VMEM_HEADROOM_BYTES, V7X_VMEM_BYTES)
    return pltpu.CompilerParams(dimension_semantics=semantics, vmem_limit_bytes=limit)


def _ada_kernel(c_ref, w_ref, b_ref, o_ref):
    c = c_ref[...]
    s = c * jax.nn.sigmoid(c)
    hi, lo = _split_f32(s)
    lhs = jnp.concatenate([hi, lo], axis=0).astype(BF16)
    r = _dot(lhs, w_ref[...].astype(BF16))
    o_ref[...] = r[:ADA_ROWS] + r[ADA_ROWS:] + b_ref[...]


def _ada_mod(cvecs, w, b):
    d, n = w.shape
    tn = 512
    vmem = 2 * d * tn * 4 + d * tn * 2 + 4 * ADA_ROWS * d * 4
    return pl.pallas_call(
        _ada_kernel,
        grid=(n // tn,),
        in_specs=[pl.BlockSpec((ADA_ROWS, d), lambda j: (0, 0)),
                  pl.BlockSpec((d, tn), lambda j: (0, j)),
                  pl.BlockSpec((1, tn), lambda j: (0, j))],
        out_specs=pl.BlockSpec((ADA_ROWS, tn), lambda j: (0, j)),
        out_shape=jax.ShapeDtypeStruct((ADA_ROWS, n), F32),
        compiler_params=_params(vmem, ("arbitrary",)),
        name="ada_mod",
    )(cvecs, w, b)


def _modulated_norm(x, g, scale, shift):
    ms = jnp.mean(x * x, axis=-1, keepdims=True)
    return (x * lax.rsqrt(ms + EPS) * g) * (1.0 + scale) + shift


def _norm1_kernel(x_ref, g_ref, sc_ref, sh_ref, h_ref):
    g, sc, sh = g_ref[...], sc_ref[...], sh_ref[...]

    def norm_rows(r, carry):
        rows = pl.ds(pl.multiple_of(r * NORM_ROWS, NORM_ROWS), NORM_ROWS)
        h_ref[rows, :] = _modulated_norm(x_ref[rows, :], g, sc, sh).astype(BF16)
        return carry

    lax.fori_loop(0, x_ref.shape[0] // NORM_ROWS, norm_rows, 0, unroll=2)


def _norm1(x2d, g, sc, sh, *, rows_per_sample):
    n, d = x2d.shape
    tm = 256
    tiles_per_sample = rows_per_sample // tm
    mod = pl.BlockSpec((None, 1, d), lambda i: (i // tiles_per_sample, 0, 0))
    vmem = 2 * tm * d * (4 + 2) + 8 * NORM_ROWS * d * 4
    return pl.pallas_call(
        _norm1_kernel,
        grid=(n // tm,),
        in_specs=[pl.BlockSpec((tm, d), lambda i: (i, 0)), pl.BlockSpec((1, d), lambda i: (0, 0)), mod, mod],
        out_specs=pl.BlockSpec((tm, d), lambda i: (i, 0)),
        out_shape=jax.ShapeDtypeStruct((n, d), BF16),
        compiler_params=_params(vmem, ("arbitrary",)),
        name="norm1",
    )(x2d, g, sc, sh)


def _gate_proj_kernel(h_ref, lbp_ref, w_ref, f_ref, k_ref, w_scr):
    @pl.when(pl.program_id(1) == 0)
    def _():
        w_scr[...] = w_ref[...].astype(BF16)

    z = _dot(h_ref[...], w_scr[...])
    p = lbp_ref[...]
    e = jnp.exp(p - jnp.max(p, axis=0, keepdims=True))
    lb = e[0:1] / jnp.sum(e, axis=0, keepdims=True)
    f = lb + (1.0 - lb) * jax.nn.sigmoid(z)
    f_ref[...] = f
    k_ref[...] = (1.0 - f).astype(BF16)


def _rest_proj_kernel(h_ref, w_ref, r_ref, w_scr):
    @pl.when(pl.program_id(1) == 0)
    def _():
        w_scr[...] = w_ref[...].astype(BF16)

    r_ref[...] = _dot(h_ref[...], w_scr[...]).astype(BF16)


def _in_proj(h, lbp, w, *, n_cols):
    n, d = h.shape
    tm, tn = 1024, 512
    n_gate = 2 * HG_WIDTH // tn
    w_bytes = 2 * d * tn * 4 + d * tn * 2
    vmem = 2 * tm * d * 2 + w_bytes + 2 * tm * tn * (4 + 2) + 4 * tm * tn * 4
    fdec, kk = pl.pallas_call(
        _gate_proj_kernel,
        grid=(n_gate, n // tm),
        in_specs=[pl.BlockSpec((tm, d), lambda j, i: (i, 0)),
                  pl.BlockSpec((2, tn), lambda j, i: (0, j)),
                  pl.BlockSpec((d, tn), lambda j, i: (0, j))],
        out_specs=[pl.BlockSpec((tm, tn), lambda j, i: (i, j)),
                   pl.BlockSpec((tm, tn), lambda j, i: (i, j))],
        out_shape=[jax.ShapeDtypeStruct((n, 2 * HG_WIDTH), F32),
                   jax.ShapeDtypeStruct((n, 2 * HG_WIDTH), BF16)],
        scratch_shapes=[pltpu.VMEM((d, tn), BF16)],
        compiler_params=_params(vmem, ("arbitrary", "arbitrary")),
        name="gate_proj",
    )(h, lbp, w)
    vmem = 2 * tm * d * 2 + w_bytes + 2 * tm * tn * 2 + 2 * tm * tn * 4
    rest = pl.pallas_call(
        _rest_proj_kernel,
        grid=((n_cols - 2 * HG_WIDTH) // tn, n // tm),
        in_specs=[pl.BlockSpec((tm, d), lambda j, i: (i, 0)),
                  pl.BlockSpec((d, tn), lambda j, i: (0, n_gate + j))],
        out_specs=pl.BlockSpec((tm, tn), lambda j, i: (i, j)),
        out_shape=jax.ShapeDtypeStruct((n, n_cols - 2 * HG_WIDTH), BF16),
        scratch_shapes=[pltpu.VMEM((d, tn), BF16)],
        compiler_params=_params(vmem, ("arbitrary", "arbitrary")),
        name="rest_proj",
    )(h, w)
    return fdec, kk, rest


def _ctx_tri():
    s = np.arange(CTX_LEN)[:, None]
    u = np.arange(CTX_LEN)[None, :]
    tri = np.stack([u > s, u < s]).astype(np.float32)
    return jnp.asarray(np.concatenate([tri, tri], axis=2), dtype=BF16)


CTX_HEADS = 4


def _ctx_state_kernel(ff_ref, fb_ref, kf_ref, kb_ref, i_ref, tri_ref, sf_ref, sb_ref):
    dh = HG_HEAD_DIM

    def state(f, k, v, a):
        hi, lo = _split_f32(jnp.log(f))
        g = _dot(a, jnp.concatenate([hi, lo], axis=0).astype(BF16))
        kd = (k.astype(F32) * jnp.exp(g)).astype(BF16)
        return _dot_tn(v, kd)

    for j in range(CTX_HEADS):
        cols = slice(j * dh, (j + 1) * dh)
        v = i_ref[:, cols]
        sf_ref[j] = state(ff_ref[:, cols], kf_ref[:, cols], v, tri_ref[0])
        sb_ref[j] = state(fb_ref[:, cols], kb_ref[:, cols], v, tri_ref[1])


def _ctx_states(fdec, kk, vi):
    t, dh, nh = CTX_LEN, HG_HEAD_DIM, HG_HEADS
    groups = nh // CTX_HEADS
    blk = lambda off: pl.BlockSpec((t, CTX_HEADS * dh), lambda b, h: (b, off + h))
    st = pl.BlockSpec((None, CTX_HEADS, dh, dh), lambda b, h: (b, h, 0, 0))
    shape = jax.ShapeDtypeStruct((BATCH, nh, dh, dh), F32)
    vmem = CTX_HEADS * 2 * (2 * t * dh * 4 + 3 * t * dh * 2 + 2 * dh * dh * 4) + 2 * 2 * t * 2 * t * 2
    return pl.pallas_call(
        _ctx_state_kernel,
        grid=(BATCH, groups),
        in_specs=[blk(0), blk(groups), blk(0), blk(groups), blk(0),
                  pl.BlockSpec((2, t, 2 * t), lambda b, h: (0, 0, 0))],
        out_specs=[st, st],
        out_shape=[shape, shape],
        compiler_params=_params(vmem, ("arbitrary", "arbitrary")),
        name="ctx_state",
    )(fdec, fdec, kk, kk, vi, _ctx_tri())


def _pair_levels():
    c, nl = SCAN_CHUNK, SCAN_LEVELS
    t = np.arange(c)[:, None]
    u = np.arange(c)[None, :]
    lev = np.floor(np.log2(np.maximum(t ^ u, 1))).astype(np.int32)
    lidx = np.stack([np.where(t > u, lev, np.where(t == u, nl, -1)),
                     np.where(t < u, lev, np.where(t == u, nl, -1))]).astype(np.int32)
    return jnp.asarray(lidx)


def _swap_halves(x, h, t_idx):
    c = x.shape[0]
    if h >= V7X_SUBLANES:
        tiles = [x[i * h:(i + 1) * h] for i in range(c // h)]
        return jnp.concatenate([tiles[i ^ 1] for i in range(len(tiles))], axis=0)
    x3 = x.reshape(c // V7X_SUBLANES, V7X_SUBLANES, x.shape[1])
    if 2 * h == V7X_SUBLANES:
        return pltpu.roll(x3, h, axis=1).reshape(x.shape)
    up = pltpu.roll(x3, h, axis=1).reshape(x.shape)
    down = pltpu.roll(x3, V7X_SUBLANES - h, axis=1).reshape(x.shape)
    return jnp.where((t_idx & h) != 0, up, down)


def _scan_chunk(direction, f, k, q, v, st_ref, lidx, t_idx):
    c, nl = SCAN_CHUNK, SCAN_LEVELS
    qf = q.astype(F32)
    kf = k.astype(F32)
    q_dec = f
    k_dec = jnp.ones_like(f)
    total = f
    scores = jnp.zeros((c, c), F32)
    for l in range(nl):
        h = 1 << l
        bit = (t_idx & h) != 0
        is_query = bit if direction == 0 else jnp.logical_not(bit)
        x = jnp.where(is_query, q_dec * qf, k_dec * kf).astype(BF16)
        scores = jnp.where(lidx == l, _dot_nt(x, x), scores)
        other = _swap_halves(total, h, t_idx)
        q_dec = q_dec * jnp.where(is_query, other, 1.0)
        k_dec = k_dec * jnp.where(is_query, 1.0, other)
        total = total * other
    own = jnp.sum(qf * kf, axis=-1, keepdims=True)
    q_in = (q_dec * qf).astype(BF16)
    k_st = (k_dec * kf).astype(BF16)
    st = st_ref[...]
    o = _dot(scores.astype(BF16), v) + _dot_nt(q_in, st.astype(BF16)) + own * v.astype(F32)
    st_ref[...] = total[0:1] * st + _dot_tn(v, k_st)
    return o


def _scan_kernel(ff_ref, fb_ref, kf_ref, kb_ref, i_ref, q_ref, g_ref, s0f_ref, s0b_ref, ng_ref,
                 lidx_ref, o_ref, stf_ref, stb_ref, part_ref):
    c = SCAN_CHUNK
    n_chunks = SEQ // c
    stf_ref[...] = s0f_ref[...]
    stb_ref[...] = s0b_ref[...]
    t_idx = lax.broadcasted_iota(I32, (c, HG_HEAD_DIM), 0)
    ng = ng_ref[...]

    def chunk(direction, r0):
        rows = pl.ds(r0, c)
        f_ref, k_ref, st_ref = (ff_ref, kf_ref, stf_ref) if direction == 0 else (fb_ref, kb_ref, stb_ref)
        return _scan_chunk(direction, f_ref[rows, :], k_ref[rows, :], q_ref[rows, :], i_ref[rows, :],
                           st_ref, lidx_ref[direction], t_idx)

    def finalize(o, r0):
        rows = pl.ds(r0, c)
        gate = g_ref[rows, :].astype(F32)
        ms = jnp.mean(o * o, axis=-1, keepdims=True)
        o_ref[rows, :] = (o * lax.rsqrt(ms + EPS) * ng * (gate * jax.nn.sigmoid(gate))).astype(BF16)

    def starts(n):
        return pl.multiple_of(n * c, c), pl.multiple_of((n_chunks - 1 - n) * c, c)

    def first_half(n, carry):
        rf, rb = starts(n)
        part_ref[pl.ds(rf, c), :] = chunk(0, rf)
        part_ref[pl.ds(rb, c), :] = chunk(1, rb)
        return carry

    def second_half(n, carry):
        rf, rb = starts(n)
        finalize(chunk(0, rf) + part_ref[pl.ds(rf, c), :], rf)
        finalize(chunk(1, rb) + part_ref[pl.ds(rb, c), :], rb)
        return carry

    lax.fori_loop(0, n_chunks // 2, first_half, 0, unroll=SCAN_UNROLL)
    lax.fori_loop(n_chunks // 2, n_chunks, second_half, 0, unroll=SCAN_UNROLL)


def _hgrn2_scan(fdec, kk, rest, s0f, s0b, norm_g):
    dh, nh = HG_HEAD_DIM, HG_HEADS
    lidx = _pair_levels()
    blk = lambda off: pl.BlockSpec((SEQ, dh), lambda b, h: (b, off + h))
    st = pl.BlockSpec((None, None, dh, dh), lambda b, h: (b, h, 0, 0))
    vmem = (2 * (2 * SEQ * dh * 4 + 6 * SEQ * dh * 2 + 2 * dh * dh * 4) + SEQ * dh * 4 + 2 * dh * dh * 4
            + 2 * 2 * SCAN_CHUNK * V7X_LANES * 4)
    return pl.pallas_call(
        _scan_kernel,
        grid=(BATCH, nh),
        in_specs=[blk(0), blk(nh), blk(0), blk(nh), blk(0), blk(nh), blk(2 * nh), st, st,
                  pl.BlockSpec((1, dh), lambda b, h: (0, h)),
                  pl.BlockSpec(lidx.shape, lambda b, h: (0, 0, 0))],
        out_specs=blk(0),
        out_shape=jax.ShapeDtypeStruct((BATCH * SEQ, HG_WIDTH), BF16),
        scratch_shapes=[pltpu.VMEM((dh, dh), F32), pltpu.VMEM((dh, dh), F32), pltpu.VMEM((SEQ, dh), F32)],
        compiler_params=_params(vmem, ("arbitrary", "arbitrary")),
        name="hgrn2_scan",
    )(fdec, fdec, kk, kk, rest, rest, rest, s0f, s0b, norm_g, lidx)


POOL_PIECE = 256


def _box_bounds(n, w):
    start = np.arange(n) - w // 2
    return np.clip(start, 0, n), np.clip(start + w, 0, n)


def _pool_consts():
    rows = SEQ // GRID_W
    col_box = np.zeros((len(POOL_WINDOWS), POOL_PIECE, POOL_PIECE), np.float32)
    cnt = np.zeros((len(POOL_WINDOWS), SEQ, 1), np.float32)
    cc = np.arange(GRID_W)[None, :]
    for gi, w in enumerate(POOL_WINDOWS):
        c0, c1 = _box_bounds(GRID_W, w)
        r0, r1 = _box_bounds(rows, w)
        wc = ((cc >= c0[:, None]) & (cc < c1[:, None])).astype(np.float32)
        col_box[gi] = np.kron(np.eye(POOL_PIECE // GRID_W, dtype=np.float32), wc)
        cnt[gi] = ((r1 - r0)[:, None] * (c1 - c0)[None, :]).reshape(SEQ, 1)
    return jnp.asarray(col_box, dtype=BF16), jnp.asarray(cnt)


def _pool_kernel(v_ref, box_ref, cnt_ref, pw_ref, ps_ref, o_ref, pre_ref, diff_ref):
    gi = pl.program_id(1)
    rows = SEQ // GRID_W
    gw = GRID_W
    box = box_ref[...]
    pre_ref[0:gw, :] = jnp.zeros((gw, POOL_GROUP), F32)
    for p in range(SEQ // POOL_PIECE):
        yc = _dot(box, v_ref[p * POOL_PIECE:(p + 1) * POOL_PIECE, :])
        for rr in range(POOL_PIECE // gw):
            r = p * (POOL_PIECE // gw) + rr
            pre_ref[(r + 1) * gw:(r + 2) * gw, :] = pre_ref[r * gw:(r + 1) * gw, :] + yc[rr * gw:(rr + 1) * gw]
    for k, w in enumerate(POOL_WINDOWS):
        @pl.when(gi == k)
        def _(w=w):
            r0, r1 = _box_bounds(rows, w)
            for r in range(rows):
                sl = slice(r * gw, (r + 1) * gw)
                box_sum = pre_ref[int(r1[r]) * gw:(int(r1[r]) + 1) * gw, :] - pre_ref[int(r0[r]) * gw:(int(r0[r]) + 1) * gw, :]
                diff_ref[sl, :] = (box_sum / cnt_ref[sl, :] - v_ref[sl, :].astype(F32)).astype(BF16)
    o_ref[...] = (_dot(diff_ref[...], pw_ref[...].astype(BF16)) * ps_ref[...]).astype(BF16)


def _grid_pool(rest, pool_w, pool_scale):
    ng, pg = len(POOL_WINDOWS), POOL_GROUP
    col_box, cnt = _pool_consts()
    v_off = 3 * HG_WIDTH // pg
    vmem = (2 * (2 * SEQ * pg * 2 + POOL_PIECE * POOL_PIECE * 2 + SEQ * 128 * 4 + pg * pg * 4)
            + (SEQ + GRID_W) * pg * 4 + SEQ * pg * 2 + SEQ * pg * 4)
    return pl.pallas_call(
        _pool_kernel,
        grid=(BATCH, ng),
        in_specs=[pl.BlockSpec((SEQ, pg), lambda b, k: (b, v_off + k)),
                  pl.BlockSpec((None, POOL_PIECE, POOL_PIECE), lambda b, k: (k, 0, 0)),
                  pl.BlockSpec((None, SEQ, 1), lambda b, k: (k, 0, 0)),
                  pl.BlockSpec((None, pg, pg), lambda b, k: (k, 0, 0)),
                  pl.BlockSpec((1, pg), lambda b, k: (0, k))],
        out_specs=pl.BlockSpec((SEQ, pg), lambda b, k: (b, k)),
        out_shape=jax.ShapeDtypeStruct((BATCH * SEQ, POOL_WIDTH), BF16),
        scratch_shapes=[pltpu.VMEM((SEQ + GRID_W, pg), F32), pltpu.VMEM((SEQ, pg), BF16)],
        compiler_params=_params(vmem, ("arbitrary", "arbitrary")),
        name="grid_pool",
    )(rest, col_box, cnt, pool_w, pool_scale)


def _out_proj_kernel(o_ref, pm_ref, wa_ref, wb_ref, x_ref, g1_ref, out_ref, wa_scr, wb_scr):
    @pl.when(pl.program_id(1) == 0)
    def _():
        wa_scr[...] = wa_ref[...].astype(BF16)
        wb_scr[...] = wb_ref[...].astype(BF16)

    acc = _dot(o_ref[...], wa_scr[...]) + _dot(pm_ref[...], wb_scr[...])
    out_ref[...] = x_ref[...] + g1_ref[...] * acc


def _out_proj(o, pm, w, x2d, g1):
    n, d = x2d.shape
    tm, tn = 1024, 512
    half = w.shape[0] // 2
    tiles_per_sample = SEQ // tm
    vmem = 2 * (2 * tm * half * 2 + 2 * half * tn * 4 + 2 * tm * tn * 4) + 2 * half * tn * 2 + tm * tn * 4
    return pl.pallas_call(
        _out_proj_kernel,
        grid=(d // tn, n // tm),
        in_specs=[pl.BlockSpec((tm, half), lambda j, i: (i, 0)),
                  pl.BlockSpec((tm, half), lambda j, i: (i, 0)),
                  pl.BlockSpec((half, tn), lambda j, i: (0, j)),
                  pl.BlockSpec((half, tn), lambda j, i: (1, j)),
                  pl.BlockSpec((tm, tn), lambda j, i: (i, j)),
                  pl.BlockSpec((None, 1, tn), lambda j, i: (i // tiles_per_sample, 0, j))],
        out_specs=pl.BlockSpec((tm, tn), lambda j, i: (i, j)),
        out_shape=jax.ShapeDtypeStruct((n, d), F32),
        scratch_shapes=[pltpu.VMEM((half, tn), BF16), pltpu.VMEM((half, tn), BF16)],
        compiler_params=_params(vmem, ("arbitrary", "arbitrary")),
        name="out_proj",
    )(o, pm, w, w, x2d, g1)


def _norm2_router_kernel(x_ref, g_ref, sc_ref, sh_ref, rw_ref, h2_ref, aff_ref):
    h = _modulated_norm(x_ref[...], g_ref[...], sc_ref[...], sh_ref[...])
    h2_ref[...] = h
    h_hi, h_lo = _split_f32(h)
    r_hi, r_lo = _split_f32(rw_ref[...])
    h_hi, h_lo, r_hi, r_lo = (a.astype(BF16) for a in (h_hi, h_lo, r_hi, r_lo))
    logits = _dot_nt(r_hi, h_hi) + (_dot_nt(r_hi, h_lo) + _dot_nt(r_lo, h_hi))
    e = jnp.exp(logits - jnp.max(logits, axis=0, keepdims=True))
    aff_ref[...] = e / jnp.sum(e, axis=0, keepdims=True)


def _norm2_router(x1, g, sc, sh, router_wt):
    n, d = x1.shape
    tm = 256
    tiles_per_sample = SEQ // tm
    vmem = 2 * (2 * tm * d * 4 + N_EXPERTS * d * 4) + 6 * tm * d * 4
    return pl.pallas_call(
        _norm2_router_kernel,
        grid=(n // tm,),
        in_specs=[pl.BlockSpec((tm, d), lambda i: (i, 0)),
                  pl.BlockSpec((1, d), lambda i: (0, 0)),
                  pl.BlockSpec((None, 1, d), lambda i: (i // tiles_per_sample, 0, 0)),
                  pl.BlockSpec((None, 1, d), lambda i: (i // tiles_per_sample, 0, 0)),
                  pl.BlockSpec((N_EXPERTS, d), lambda i: (0, 0))],
        out_specs=[pl.BlockSpec((tm, d), lambda i: (i, 0)),
                   pl.BlockSpec((None, N_EXPERTS, tm), lambda i: (i // tiles_per_sample, 0, i % tiles_per_sample))],
        out_shape=[jax.ShapeDtypeStruct((n, d), F32),
                   jax.ShapeDtypeStruct((BATCH, N_EXPERTS, SEQ), F32)],
        compiler_params=_params(vmem, ("arbitrary",)),
        name="norm2_router",
    )(x1, g, sc, sh, router_wt)


ROUTE_COLS = 256
ROUTE_ROWS = 8
TOKEN_DIGIT_BITS = 6
TOKEN_DIGIT = 1 << TOKEN_DIGIT_BITS


def _route_kernel(aff_ref, slot_ref, gate_ref, idx_ref):
    a = aff_ref[...]
    rows, n = a.shape
    thr = jnp.zeros((rows, 1), I32)
    for b in range(30, -1, -1):
        cand = thr | (1 << b)
        cnt = jnp.sum((a >= pltpu.bitcast(cand, F32)).astype(I32), axis=-1, keepdims=True)
        thr = jnp.where(cnt >= CAPACITY, cand, thr)
    gt = a >= pltpu.bitcast(thr + 1, F32)
    eq = jnp.logical_and(a >= pltpu.bitcast(thr, F32), jnp.logical_not(gt))
    need = CAPACITY - jnp.sum(gt.astype(I32), axis=-1, keepdims=True)

    def prefix_count(mask):
        m = jnp.where(mask, 1.0, 0.0).astype(BF16)
        r = lax.broadcasted_iota(I32, (n, ROUTE_COLS), 0)
        c = lax.broadcasted_iota(I32, (n, ROUTE_COLS), 1)
        parts = [_dot(m, jnp.where(r < c + cb * ROUTE_COLS, 1.0, 0.0).astype(BF16))
                 for cb in range(n // ROUTE_COLS)]
        return jnp.concatenate(parts, axis=1).astype(I32)

    sel = jnp.logical_or(gt, jnp.logical_and(eq, prefix_count(eq) < need))
    slot = jnp.where(sel, prefix_count(sel), -1)
    slot_ref[...] = slot

    a_hi, a_rest = _split_f32(a)
    a_mid, a_lo = _split_f32(a_rest)
    tok = lax.broadcasted_iota(I32, (1, n), 1)
    tok_hi = (tok >> TOKEN_DIGIT_BITS).astype(F32)
    tok_lo = (tok & (TOKEN_DIGIT - 1)).astype(F32)
    slot_ids = lax.broadcasted_iota(I32, (CAPACITY, n), 0)
    pad = jnp.zeros((ROUTE_ROWS - 5, n), F32)
    for r in range(rows):
        onehot = jnp.where(slot[r:r + 1] == slot_ids, 1.0, 0.0).astype(BF16)
        pieces = jnp.concatenate([a_hi[r:r + 1], a_mid[r:r + 1], a_lo[r:r + 1], tok_hi, tok_lo, pad],
                                 axis=0).astype(BF16)
        res = _dot_nt(pieces, onehot)
        gate_ref[r:r + 1, :] = res[0:1] + res[1:2] + res[2:3]
        idx_ref[r:r + 1, :] = (res[3:4] * TOKEN_DIGIT + res[4:5]).astype(I32)


def _route(aff_rows):
    rows, n = aff_rows.shape
    vmem = 4 * ROUTE_ROWS * n * 4 + 4 * n * ROUTE_COLS * 4 + 4 * CAPACITY * n * 4
    return pl.pallas_call(
        _route_kernel,
        grid=(rows // ROUTE_ROWS,),
        in_specs=[pl.BlockSpec((ROUTE_ROWS, n), lambda i: (i, 0))],
        out_specs=[pl.BlockSpec((ROUTE_ROWS, n), lambda i: (i, 0)),
                   pl.BlockSpec((ROUTE_ROWS, CAPACITY), lambda i: (i, 0)),
                   pl.BlockSpec((ROUTE_ROWS, CAPACITY), lambda i: (i, 0))],
        out_shape=[jax.ShapeDtypeStruct((rows, n), I32),
                   jax.ShapeDtypeStruct((rows, CAPACITY), F32),
                   jax.ShapeDtypeStruct((rows, CAPACITY), I32)],
        compiler_params=_params(vmem, ("arbitrary",)),
        name="route",
    )(aff_rows)


CAST_ROWS = 64


def _moe_up_kernel(idx_ref, h2_hbm, w1_ref, w3_ref, hid_ref, rows32, rows16, sem):
    e, f = pl.program_id(0), pl.program_id(1)
    n_experts = pl.num_programs(0)
    n_ff, share, _ = rows32.shape
    m = n_ff * share

    def row_copy(src_row, group, r):
        return pltpu.make_async_copy(h2_hbm.at[pl.ds(src_row, 1), :], rows32.at[group, pl.ds(r, 1), :], sem)

    def start_group(expert, group):
        first = expert * m + group * share
        for r in range(share):
            row_copy(idx_ref[first + r], group, r).start()

    @pl.when(jnp.logical_and(e == 0, f == 0))
    def _():
        lax.fori_loop(0, n_ff, lambda grp, carry: (start_group(0, grp), carry)[1], 0)

    @pl.when(f == 0)
    def _():
        def wait_group(grp, carry):
            for r in range(share):
                row_copy(0, grp, r).wait()
            return carry
        lax.fori_loop(0, n_ff, wait_group, 0)

        def cast_group(grp, carry):
            for c0 in range(0, share, CAST_ROWS):
                dst = pl.ds(pl.multiple_of(grp * share + c0, CAST_ROWS), CAST_ROWS)
                rows16[dst, :] = rows32[grp, c0:c0 + CAST_ROWS, :].astype(BF16)
            return carry
        lax.fori_loop(0, n_ff, cast_group, 0)

    @pl.when(e + 1 < n_experts)
    def _():
        start_group(e + 1, f)

    xg = rows16[...]
    a = _dot(xg, w1_ref[...].astype(BF16))
    b = _dot(xg, w3_ref[...].astype(BF16))
    hid_ref[...] = (a * jax.nn.sigmoid(a) * b).astype(BF16)


def _moe_up(idx_table, h2, w1, w3):
    ne, m = idx_table.shape
    d = h2.shape[1]
    ff = w1.shape[2]
    tf = 256
    vmem = m * d * (4 + 2) + 2 * 2 * d * tf * 4 + 2 * d * tf * 2 + 3 * m * tf * 4 + 2 * m * tf * 2
    n_ff = ff // tf
    w_spec = pl.BlockSpec((None, d, tf), lambda e, f, idx: (e, 0, f))
    return pl.pallas_call(
        _moe_up_kernel,
        grid_spec=pltpu.PrefetchScalarGridSpec(
            num_scalar_prefetch=1,
            grid=(ne, n_ff),
            in_specs=[pl.BlockSpec(memory_space=pl.ANY), w_spec, w_spec],
            out_specs=pl.BlockSpec((None, m, tf), lambda e, f, idx: (e, 0, f)),
            scratch_shapes=[pltpu.VMEM((n_ff, m // n_ff, d), F32), pltpu.VMEM((m, d), BF16),
                            pltpu.SemaphoreType.DMA(())]),
        out_shape=jax.ShapeDtypeStruct((ne, m, ff), BF16),
        compiler_params=_params(vmem, ("arbitrary", "arbitrary")),
        name="moe_up",
    )(idx_table.reshape(ne * m), h2, w1, w3)


def _moe_down_kernel(hid_ref, w2_ref, gate_ref, y_ref):
    y = _dot(hid_ref[...], w2_ref[...].astype(BF16))
    y_ref[...] = (y * gate_ref[...]).astype(BF16)


def _moe_down(hid, w2, gate_col):
    ne, m, ff = hid.shape
    d = w2.shape[2]
    tn = 1024
    vmem = 2 * m * ff * 2 + 2 * ff * tn * 4 + ff * tn * 2 + 2 * m * 128 * 4 + 2 * m * tn * 4 + 2 * m * tn * 2
    return pl.pallas_call(
        _moe_down_kernel,
        grid=(ne, d // tn),
        in_specs=[pl.BlockSpec((None, m, ff), lambda e, j: (e, 0, 0)),
                  pl.BlockSpec((None, ff, tn), lambda e, j: (e, 0, j)),
                  pl.BlockSpec((None, m, 1), lambda e, j: (e, 0, 0))],
        out_specs=pl.BlockSpec((None, m, tn), lambda e, j: (e, 0, j)),
        out_shape=jax.ShapeDtypeStruct((ne, m, d), BF16),
        compiler_params=_params(vmem, ("arbitrary", "arbitrary")),
        name="moe_down",
    )(hid, w2, gate_col)


COMBINE_TOKENS = 128


def _moe_combine_kernel(slot_ref, y_ref, x1_ref, g2_ref, fg_ref, out_ref):
    st = slot_ref[...]
    slot_ids = lax.broadcasted_iota(I32, (COMBINE_TOKENS, CAPACITY), 1)
    onehot = jnp.concatenate(
        [jnp.where(st[:, e:e + 1] == slot_ids, 1.0, 0.0).astype(BF16) for e in range(N_EXPERTS)], axis=1)
    y = y_ref[...].reshape(N_EXPERTS * CAPACITY, y_ref.shape[-1])
    x2 = x1_ref[...] + g2_ref[...] * _dot(onehot, y)
    ms = jnp.mean(x2 * x2, axis=-1, keepdims=True)
    out_ref[...] = x2 * lax.rsqrt(ms + EPS) * fg_ref[...]


def _moe_combine(slot_ble, y, x1, g2, final_g):
    n, d = x1.shape
    tm = COMBINE_TOKENS
    tiles_per_sample = SEQ // tm
    vmem = (N_EXPERTS * CAPACITY * d * 2 + 2 * 2 * tm * d * 4 + 2 * tm * 128 * 4
            + tm * N_EXPERTS * CAPACITY * 2 + 3 * tm * d * 4)
    return pl.pallas_call(
        _moe_combine_kernel,
        grid=(BATCH, tiles_per_sample),
        in_specs=[pl.BlockSpec((None, tm, N_EXPERTS), lambda b, t: (b, t, 0)),
                  pl.BlockSpec((N_EXPERTS, None, CAPACITY, d), lambda b, t: (0, b, 0, 0),
                               pipeline_mode=pl.Buffered(1)),
                  pl.BlockSpec((tm, d), lambda b, t: (b * tiles_per_sample + t, 0)),
                  pl.BlockSpec((None, 1, d), lambda b, t: (b, 0, 0)),
                  pl.BlockSpec((1, d), lambda b, t: (0, 0))],
        out_specs=pl.BlockSpec((tm, d), lambda b, t: (b * tiles_per_sample + t, 0)),
        out_shape=jax.ShapeDtypeStruct((n, d), F32),
        compiler_params=_params(vmem, ("arbitrary", "arbitrary")),
        name="moe_combine",
    )(slot_ble, y, x1, g2, final_g)


def kernel(x, c, ctx, c_ctx, ada_w, ada_b, norm1_g, norm2_g, w_in, lb_param, hg_norm_g, pool_w, pool_scale,
           w_out, router_w, moe_w1, moe_w3, moe_w2, final_norm_g):
    nb, seq, d = x.shape
    assert (nb, seq, d) == (BATCH, SEQ, D_MODEL) and ctx.shape[1] == CTX_LEN
    assert ada_w.shape[0] == 1 and lb_param.shape[0] == 2, "single-layer block: layer 0 uses lower-bound row 0"
    x2d = x.reshape(nb * seq, d)
    ctx2d = ctx.reshape(nb * CTX_LEN, d)

    cvecs = jnp.zeros((ADA_ROWS, d), F32).at[:nb].set(c).at[nb].set(c_ctx)
    mod = _ada_mod(cvecs, ada_w[0], ada_b).reshape(ADA_ROWS, 6, d)
    sh1, sc1, g1, sh2, sc2, g2 = (mod[:nb, k][:, None, :] for k in range(6))
    csh1, csc1 = mod[nb:nb + 1, 0][:, None, :], mod[nb:nb + 1, 1][:, None, :]

    lbp = lb_param.reshape(lb_param.shape[0], 2 * HG_WIDTH)

    h_c = _norm1(ctx2d, norm1_g, csc1, csh1, rows_per_sample=nb * CTX_LEN)
    f_c, k_c, i_c = _in_proj(h_c, lbp, w_in[0], n_cols=3 * HG_WIDTH)
    s_f, s_b = _ctx_states(f_c, k_c, i_c)

    h_x = _norm1(x2d, norm1_g, sc1, sh1, rows_per_sample=seq)
    fdec, kk, rest = _in_proj(h_x, lbp, w_in[0], n_cols=w_in.shape[2])
    o = _hgrn2_scan(fdec, kk, rest, s_f, s_b, hg_norm_g)
    pm = _grid_pool(rest, pool_w[0], pool_scale)
    x1 = _out_proj(o, pm, w_out[0], x2d, g1)

    h2, aff_t = _norm2_router(x1, norm2_g, sc2, sh2, router_w[0].T)
    slot, gates, tok = _route(aff_t.reshape(nb * N_EXPERTS, seq))
    slot = slot.reshape(nb, N_EXPERTS, seq)
    h2_rows = tok.reshape(nb, N_EXPERTS, CAPACITY) + (jnp.arange(nb, dtype=I32) * seq)[:, None, None]
    hid = _moe_up(h2_rows.transpose(1, 0, 2).reshape(N_EXPERTS, nb * CAPACITY), h2, moe_w1[0], moe_w3[0])
    gate_col = gates.reshape(nb, N_EXPERTS, CAPACITY).transpose(1, 0, 2).reshape(N_EXPERTS, nb * CAPACITY, 1)
    y = _moe_down(hid, moe_w2[0], gate_col).reshape(N_EXPERTS, nb, CAPACITY, d)
    out = _moe_combine(slot.transpose(0, 2, 1), y, x1, g2, final_norm_g[None, :])
    return out.reshape(nb, seq, d).astype(x.dtype)
```

```python
import jax
import jax.numpy as jnp
import numpy as np
from jax import lax
from jax.experimental import pallas as pl
from jax.experimental.pallas import tpu as pltpu

F32 = jnp.float32
BF16 = jnp.bfloat16
I32 = jnp.int32

D_MODEL = 4096
BATCH = 4
SEQ = 2048
GRID_W = 64
CTX_LEN = 256
HG_HEADS = 16
HG_HEAD_DIM = 128
HG_WIDTH = HG_HEADS * HG_HEAD_DIM
POOL_WIDTH = D_MODEL - HG_WIDTH
POOL_WINDOWS = (2, 4, 8, 16)
POOL_GROUP = POOL_WIDTH // len(POOL_WINDOWS)
N_EXPERTS = 16
EXPERT_FF = D_MODEL // 2
CAPACITY = 2 * SEQ // N_EXPERTS
EPS = 1e-6

V7X_VMEM_BYTES = 64 * 1024 * 1024
V7X_VMEM_HEADROOM_BYTES = 10 * 1024 * 1024
V7X_SUBLANES = 8
V7X_LANES = 128
NORM_ROWS = 16
ADA_ROWS = 8

SCAN_CHUNK = 128
SCAN_LEVELS = 7
SCAN_UNROLL = 2

_NT = (((1,), (1,)), ((), ()))
_TN = (((0,), (0,)), ((), ()))


def _dot(a, b):
    return jnp.dot(a, b, preferred_element_type=F32)


def _dot_nt(a, b):
    return lax.dot_general(a, b, _NT, preferred_element_type=F32)


def _dot_tn(a, b):
    return lax.dot_general(a, b, _TN, preferred_element_type=F32)


def _split_f32(x):
    hi = x.astype(BF16).astype(F32)
    return hi, x - hi


def _params(vmem_bytes, semantics):
    limit = min(int(vmem_bytes) + V7X_VMEM_HEADROOM_BYTES, V7X_VMEM_BYTES)
    return pltpu.CompilerParams(dimension_semantics=semantics, vmem_limit_bytes=limit)


def _ada_kernel(c_ref, w_ref, b_ref, o_ref):
    c = c_ref[...]
    s = c * jax.nn.sigmoid(c)
    hi, lo = _split_f32(s)
    lhs = jnp.concatenate([hi, lo], axis=0).astype(BF16)
    r = _dot(lhs, w_ref[...].astype(BF16))
    o_ref[...] = r[:ADA_ROWS] + r[ADA_ROWS:] + b_ref[...]


def _ada_mod(cvecs, w, b):
    d, n = w.shape
    tn = 512
    vmem = 2 * d * tn * 4 + d * tn * 2 + 4 * ADA_ROWS * d * 4
    return pl.pallas_call(
        _ada_kernel,
        grid=(n // tn,),
        in_specs=[pl.BlockSpec((ADA_ROWS, d), lambda j: (0, 0)),
                  pl.BlockSpec((d, tn), lambda j: (0, j)),
                  pl.BlockSpec((1, tn), lambda j: (0, j))],
        out_specs=pl.BlockSpec((ADA_ROWS, tn), lambda j: (0, j)),
        out_shape=jax.ShapeDtypeStruct((ADA_ROWS, n), F32),
        compiler_params=_params(vmem, ("arbitrary",)),
        name="ada_mod",
    )(cvecs, w, b)


def _modulated_norm(x, g, scale, shift):
    ms = jnp.mean(x * x, axis=-1, keepdims=True)
    return (x * lax.rsqrt(ms + EPS) * g) * (1.0 + scale) + shift


def _norm1_kernel(x_ref, g_ref, sc_ref, sh_ref, h_ref):
    g, sc, sh = g_ref[...], sc_ref[...], sh_ref[...]

    def norm_rows(r, carry):
        rows = pl.ds(pl.multiple_of(r * NORM_ROWS, NORM_ROWS), NORM_ROWS)
        h_ref[rows, :] = _modulated_norm(x_ref[rows, :], g, sc, sh).astype(BF16)
        return carry

    lax.fori_loop(0, x_ref.shape[0] // NORM_ROWS, norm_rows, 0, unroll=2)


def _norm1(x2d, g, sc, sh, *, rows_per_sample):
    n, d = x2d.shape
    tm = 256
    tiles_per_sample = rows_per_sample // tm
    mod = pl.BlockSpec((None, 1, d), lambda i: (i // tiles_per_sample, 0, 0))
    vmem = 2 * tm * d * (4 + 2) + 8 * NORM_ROWS * d * 4
    return pl.pallas_call(
        _norm1_kernel,
        grid=(n // tm,),
        in_specs=[pl.BlockSpec((tm, d), lambda i: (i, 0)), pl.BlockSpec((1, d), lambda i: (0, 0)), mod, mod],
        out_specs=pl.BlockSpec((tm, d), lambda i: (i, 0)),
        out_shape=jax.ShapeDtypeStruct((n, d), BF16),
        compiler_params=_params(vmem, ("arbitrary",)),
        name="norm1",
    )(x2d, g, sc, sh)


def _gate_proj_kernel(h_ref, lbp_ref, w_ref, f_ref, k_ref, w_scr):
    @pl.when(pl.program_id(1) == 0)
    def _():
        w_scr[...] = w_ref[...].astype(BF16)

    z = _dot(h_ref[...], w_scr[...])
    p = lbp_ref[...]
    e = jnp.exp(p - jnp.max(p, axis=0, keepdims=True))
    lb = e[0:1] / jnp.sum(e, axis=0, keepdims=True)
    f = lb + (1.0 - lb) * jax.nn.sigmoid(z)
    f_ref[...] = f
    k_ref[...] = (1.0 - f).astype(BF16)


def _rest_proj_kernel(h_ref, w_ref, r_ref, w_scr):
    @pl.when(pl.program_id(1) == 0)
    def _():
        w_scr[...] = w_ref[...].astype(BF16)

    r_ref[...] = _dot(h_ref[...], w_scr[...]).astype(BF16)


def _in_proj(h, lbp, w, *, n_cols):
    n, d = h.shape
    tm, tn = 1024, 512
    n_gate = 2 * HG_WIDTH // tn
    w_bytes = 2 * d * tn * 4 + d * tn * 2
    vmem = 2 * tm * d * 2 + w_bytes + 2 * tm * tn * (4 + 2) + 4 * tm * tn * 4
    fdec, kk = pl.pallas_call(
        _gate_proj_kernel,
        grid=(n_gate, n // tm),
        in_specs=[pl.BlockSpec((tm, d), lambda j, i: (i, 0)),
                  pl.BlockSpec((2, tn), lambda j, i: (0, j)),
                  pl.BlockSpec((d, tn), lambda j, i: (0, j))],
        out_specs=[pl.BlockSpec((tm, tn), lambda j, i: (i, j)),
                   pl.BlockSpec((tm, tn), lambda j, i: (i, j))],
        out_shape=[jax.ShapeDtypeStruct((n, 2 * HG_WIDTH), F32),
                   jax.ShapeDtypeStruct((n, 2 * HG_WIDTH), BF16)],
        scratch_shapes=[pltpu.VMEM((d, tn), BF16)],
        compiler_params=_params(vmem, ("arbitrary", "arbitrary")),
        name="gate_proj",
    )(h, lbp, w)
    vmem = 2 * tm * d * 2 + w_bytes + 2 * tm * tn * 2 + 2 * tm * tn * 4
    rest = pl.pallas_call(
        _rest_proj_kernel,
        grid=((n_cols - 2 * HG_WIDTH) // tn, n // tm),
        in_specs=[pl.BlockSpec((tm, d), lambda j, i: (i, 0)),
                  pl.BlockSpec((d, tn), lambda j, i: (0, n_gate + j))],
        out_specs=pl.BlockSpec((tm, tn), lambda j, i: (i, j)),
        out_shape=jax.ShapeDtypeStruct((n, n_cols - 2 * HG_WIDTH), BF16),
        scratch_shapes=[pltpu.VMEM((d, tn), BF16)],
        compiler_params=_params(vmem, ("arbitrary", "arbitrary")),
        name="rest_proj",
    )(h, w)
    return fdec, kk, rest


def _ctx_tri():
    s = np.arange(CTX_LEN)[:, None]
    u = np.arange(CTX_LEN)[None, :]
    tri = np.stack([u > s, u < s]).astype(np.float32)
    return jnp.asarray(np.concatenate([tri, tri], axis=2), dtype=BF16)


CTX_HEADS = 4


def _ctx_state_kernel(ff_ref, fb_ref, kf_ref, kb_ref, i_ref, tri_ref, sf_ref, sb_ref):
    dh = HG_HEAD_DIM

    def state(f, k, v, a):
        hi, lo = _split_f32(jnp.log(f))
        g = _dot(a, jnp.concatenate([hi, lo], axis=0).astype(BF16))
        kd = (k.astype(F32) * jnp.exp(g)).astype(BF16)
        return _dot_tn(v, kd)

    for j in range(CTX_HEADS):
        cols = slice(j * dh, (j + 1) * dh)
        v = i_ref[:, cols]
        sf_ref[j] = state(ff_ref[:, cols], kf_ref[:, cols], v, tri_ref[0])
        sb_ref[j] = state(fb_ref[:, cols], kb_ref[:, cols], v, tri_ref[1])


def _ctx_states(fdec, kk, vi):
    t, dh, nh = CTX_LEN, HG_HEAD_DIM, HG_HEADS
    groups = nh // CTX_HEADS
    blk = lambda off: pl.BlockSpec((t, CTX_HEADS * dh), lambda b, h: (b, off + h))
    st = pl.BlockSpec((None, CTX_HEADS, dh, dh), lambda b, h: (b, h, 0, 0))
    shape = jax.ShapeDtypeStruct((BATCH, nh, dh, dh), F32)
    vmem = CTX_HEADS * 2 * (2 * t * dh * 4 + 3 * t * dh * 2 + 2 * dh * dh * 4) + 2 * 2 * t * 2 * t * 2
    return pl.pallas_call(
        _ctx_state_kernel,
        grid=(BATCH, groups),
        in_specs=[blk(0), blk(groups), blk(0), blk(groups), blk(0),
                  pl.BlockSpec((2, t, 2 * t), lambda b, h: (0, 0, 0))],
        out_specs=[st, st],
        out_shape=[shape, shape],
        compiler_params=_params(vmem, ("arbitrary", "arbitrary")),
        name="ctx_state",
    )(fdec, fdec, kk, kk, vi, _ctx_tri())


def _pair_levels():
    c, nl = SCAN_CHUNK, SCAN_LEVELS
    t = np.arange(c)[:, None]
    u = np.arange(c)[None, :]
    lev = np.floor(np.log2(np.maximum(t ^ u, 1))).astype(np.int32)
    lidx = np.stack([np.where(t > u, lev, np.where(t == u, nl, -1)),
                     np.where(t < u, lev, np.where(t == u, nl, -1))]).astype(np.int32)
    return jnp.asarray(lidx)


def _swap_halves(x, h, t_idx):
    c = x.shape[0]
    assert 2 * h <= V7X_SUBLANES
    x3 = x.reshape(c // V7X_SUBLANES, V7X_SUBLANES, x.shape[1])
    if 2 * h == V7X_SUBLANES:
        return pltpu.roll(x3, h, axis=1).reshape(x.shape)
    up = pltpu.roll(x3, h, axis=1).reshape(x.shape)
    down = pltpu.roll(x3, V7X_SUBLANES - h, axis=1).reshape(x.shape)
    return jnp.where((t_idx & h) != 0, up, down)


def _scan_chunk(direction, f, k, q, v, st_ref, lidx, t_idx):
    c, nl = SCAN_CHUNK, SCAN_LEVELS
    qf = q.astype(F32)
    kf = k.astype(F32)
    q_dec = f
    k_dec = jnp.ones_like(f)
    total = f
    scores = jnp.zeros((c, c), F32)
    sublane = lax.broadcasted_iota(I32, (1, V7X_SUBLANES, f.shape[1]), 1)
    tiles = lambda a: a.reshape(c // V7X_SUBLANES, V7X_SUBLANES, a.shape[1])
    blocks = lambda a, h: [a[i * h:(i + 1) * h] for i in range(c // h)]
    for l in range(nl):
        h = 1 << l
        if h < V7X_SUBLANES:
            bit = (sublane & h) != 0
            is_query = bit if direction == 0 else jnp.logical_not(bit)
            x = jnp.where(is_query, tiles(q_dec * qf), tiles(k_dec * kf)).reshape(c, -1).astype(BF16)
            scores = jnp.where(lidx == l, _dot_nt(x, x), scores)
            other = _swap_halves(total, h, t_idx)
            q_dec = (tiles(q_dec) * jnp.where(is_query, tiles(other), 1.0)).reshape(c, -1)
            k_dec = (tiles(k_dec) * jnp.where(is_query, 1.0, tiles(other))).reshape(c, -1)
            total = total * other
        else:
            n_blocks = c // h
            is_query = [((i & 1) == 1) == (direction == 0) for i in range(n_blocks)]
            qd, kd, qb, kb, tb = (blocks(a, h) for a in (q_dec, k_dec, qf, kf, total))
            x = jnp.concatenate([qd[i] * qb[i] if is_query[i] else kd[i] * kb[i] for i in range(n_blocks)],
                                axis=0).astype(BF16)
            pb, sb, lb = (blocks(a, h) for a in (_dot_nt(x, x), scores, lidx))
            scores = jnp.concatenate([jnp.where(lb[i] == l, pb[i], sb[i]) if is_query[i] else sb[i]
                                      for i in range(n_blocks)], axis=0)
            ob = [tb[i ^ 1] for i in range(n_blocks)]
            q_dec = jnp.concatenate([qd[i] * ob[i] if is_query[i] else qd[i] for i in range(n_blocks)], axis=0)
            k_dec = jnp.concatenate([kd[i] if is_query[i] else kd[i] * ob[i] for i in range(n_blocks)], axis=0)
            total = jnp.concatenate([tb[i] * ob[i] for i in range(n_blocks)], axis=0)
    own = jnp.sum(qf * kf, axis=-1, keepdims=True)
    q_in = (q_dec * qf).astype(BF16)
    k_st = (k_dec * kf).astype(BF16)
    st = st_ref[...]
    o = _dot(scores.astype(BF16), v) + _dot_nt(q_in, st.astype(BF16)) + own * v.astype(F32)
    st_ref[...] = total[0:1] * st + _dot_tn(v, k_st)
    return o


def _scan_kernel(ff_ref, fb_ref, kf_ref, kb_ref, i_ref, q_ref, g_ref, s0f_ref, s0b_ref, ng_ref,
                 lidx_ref, o_ref, stf_ref, stb_ref, part_ref):
    c = SCAN_CHUNK
    n_chunks = SEQ // c
    stf_ref[...] = s0f_ref[...]
    stb_ref[...] = s0b_ref[...]
    t_idx = lax.broadcasted_iota(I32, (c, HG_HEAD_DIM), 0)
    ng = ng_ref[...]

    def chunk(direction, r0):
        rows = pl.ds(r0, c)
        f_ref, k_ref, st_ref = (ff_ref, kf_ref, stf_ref) if direction == 0 else (fb_ref, kb_ref, stb_ref)
        return _scan_chunk(direction, f_ref[rows, :], k_ref[rows, :], q_ref[rows, :], i_ref[rows, :],
                           st_ref, lidx_ref[direction], t_idx)

    def finalize(o, r0):
        rows = pl.ds(r0, c)
        gate = g_ref[rows, :].astype(F32)
        ms = jnp.mean(o * o, axis=-1, keepdims=True)
        o_ref[rows, :] = (o * lax.rsqrt(ms + EPS) * ng * (gate * jax.nn.sigmoid(gate))).astype(BF16)

    def starts(n):
        return pl.multiple_of(n * c, c), pl.multiple_of((n_chunks - 1 - n) * c, c)

    def first_half(n, carry):
        rf, rb = starts(n)
        part_ref[pl.ds(rf, c), :] = chunk(0, rf)
        part_ref[pl.ds(rb, c), :] = chunk(1, rb)
        return carry

    def second_half(n, carry):
        rf, rb = starts(n)
        finalize(chunk(0, rf) + part_ref[pl.ds(rf, c), :], rf)
        finalize(chunk(1, rb) + part_ref[pl.ds(rb, c), :], rb)
        return carry

    lax.fori_loop(0, n_chunks // 2, first_half, 0, unroll=SCAN_UNROLL)
    lax.fori_loop(n_chunks // 2, n_chunks, second_half, 0, unroll=SCAN_UNROLL)


def _hgrn2_scan(fdec, kk, rest, s0f, s0b, norm_g):
    dh, nh = HG_HEAD_DIM, HG_HEADS
    lidx = _pair_levels()
    blk = lambda off: pl.BlockSpec((SEQ, dh), lambda b, h: (b, off + h))
    st = pl.BlockSpec((None, None, dh, dh), lambda b, h: (b, h, 0, 0))
    vmem = (2 * (2 * SEQ * dh * 4 + 6 * SEQ * dh * 2 + 2 * dh * dh * 4) + SEQ * dh * 4 + 2 * dh * dh * 4
            + 2 * 2 * SCAN_CHUNK * V7X_LANES * 4)
    return pl.pallas_call(
        _scan_kernel,
        grid=(BATCH, nh),
        in_specs=[blk(0), blk(nh), blk(0), blk(nh), blk(0), blk(nh), blk(2 * nh), st, st,
                  pl.BlockSpec((1, dh), lambda b, h: (0, h)),
                  pl.BlockSpec(lidx.shape, lambda b, h: (0, 0, 0))],
        out_specs=blk(0),
        out_shape=jax.ShapeDtypeStruct((BATCH * SEQ, HG_WIDTH), BF16),
        scratch_shapes=[pltpu.VMEM((dh, dh), F32), pltpu.VMEM((dh, dh), F32), pltpu.VMEM((SEQ, dh), F32)],
        compiler_params=_params(vmem, ("arbitrary", "arbitrary")),
        name="hgrn2_scan",
    )(fdec, fdec, kk, kk, rest, rest, rest, s0f, s0b, norm_g, lidx)


POOL_PIECE = 256


def _box_bounds(n, w):
    start = np.arange(n) - w // 2
    return np.clip(start, 0, n), np.clip(start + w, 0, n)


def _pool_consts():
    rows = SEQ // GRID_W
    col_box = np.zeros((len(POOL_WINDOWS), POOL_PIECE, POOL_PIECE), np.float32)
    cnt = np.zeros((len(POOL_WINDOWS), SEQ, 1), np.float32)
    cc = np.arange(GRID_W)[None, :]
    for gi, w in enumerate(POOL_WINDOWS):
        c0, c1 = _box_bounds(GRID_W, w)
        r0, r1 = _box_bounds(rows, w)
        wc = ((cc >= c0[:, None]) & (cc < c1[:, None])).astype(np.float32)
        col_box[gi] = np.kron(np.eye(POOL_PIECE // GRID_W, dtype=np.float32), wc)
        cnt[gi] = ((r1 - r0)[:, None] * (c1 - c0)[None, :]).reshape(SEQ, 1)
    return jnp.asarray(col_box, dtype=BF16), jnp.asarray(cnt)


def _pool_kernel(v_ref, box_ref, cnt_ref, pw_ref, ps_ref, o_ref, pre_ref, diff_ref):
    gi = pl.program_id(1)
    rows = SEQ // GRID_W
    gw = GRID_W
    box = box_ref[...]
    pre_ref[0:gw, :] = jnp.zeros((gw, POOL_GROUP), F32)
    for p in range(SEQ // POOL_PIECE):
        yc = _dot(box, v_ref[p * POOL_PIECE:(p + 1) * POOL_PIECE, :])
        for rr in range(POOL_PIECE // gw):
            r = p * (POOL_PIECE // gw) + rr
            pre_ref[(r + 1) * gw:(r + 2) * gw, :] = pre_ref[r * gw:(r + 1) * gw, :] + yc[rr * gw:(rr + 1) * gw]
    for k, w in enumerate(POOL_WINDOWS):
        @pl.when(gi == k)
        def _(w=w):
            r0, r1 = _box_bounds(rows, w)
            for r in range(rows):
                sl = slice(r * gw, (r + 1) * gw)
                box_sum = pre_ref[int(r1[r]) * gw:(int(r1[r]) + 1) * gw, :] - pre_ref[int(r0[r]) * gw:(int(r0[r]) + 1) * gw, :]
                diff_ref[sl, :] = (box_sum / cnt_ref[sl, :] - v_ref[sl, :].astype(F32)).astype(BF16)
    o_ref[...] = (_dot(diff_ref[...], pw_ref[...].astype(BF16)) * ps_ref[...]).astype(BF16)


def _grid_pool(rest, pool_w, pool_scale):
    ng, pg = len(POOL_WINDOWS), POOL_GROUP
    col_box, cnt = _pool_consts()
    v_off = 3 * HG_WIDTH // pg
    vmem = (2 * (2 * SEQ * pg * 2 + POOL_PIECE * POOL_PIECE * 2 + SEQ * 128 * 4 + pg * pg * 4)
            + (SEQ + GRID_W) * pg * 4 + SEQ * pg * 2 + SEQ * pg * 4)
    return pl.pallas_call(
        _pool_kernel,
        grid=(BATCH, ng),
        in_specs=[pl.BlockSpec((SEQ, pg), lambda b, k: (b, v_off + k)),
                  pl.BlockSpec((None, POOL_PIECE, POOL_PIECE), lambda b, k: (k, 0, 0)),
                  pl.BlockSpec((None, SEQ, 1), lambda b, k: (k, 0, 0)),
                  pl.BlockSpec((None, pg, pg), lambda b, k: (k, 0, 0)),
                  pl.BlockSpec((1, pg), lambda b, k: (0, k))],
        out_specs=pl.BlockSpec((SEQ, pg), lambda b, k: (b, k)),
        out_shape=jax.ShapeDtypeStruct((BATCH * SEQ, POOL_WIDTH), BF16),
        scratch_shapes=[pltpu.VMEM((SEQ + GRID_W, pg), F32), pltpu.VMEM((SEQ, pg), BF16)],
        compiler_params=_params(vmem, ("arbitrary", "arbitrary")),
        name="grid_pool",
    )(rest, col_box, cnt, pool_w, pool_scale)


def _out_proj_kernel(o_ref, pm_ref, wa_ref, wb_ref, x_ref, g1_ref, out_ref, wa_scr, wb_scr):
    @pl.when(pl.program_id(1) == 0)
    def _():
        wa_scr[...] = wa_ref[...].astype(BF16)
        wb_scr[...] = wb_ref[...].astype(BF16)

    acc = _dot(o_ref[...], wa_scr[...]) + _dot(pm_ref[...], wb_scr[...])
    out_ref[...] = x_ref[...] + g1_ref[...] * acc


def _out_proj(o, pm, w, x2d, g1):
    n, d = x2d.shape
    tm, tn = 1024, 512
    half = w.shape[0] // 2
    tiles_per_sample = SEQ // tm
    vmem = 2 * (2 * tm * half * 2 + 2 * half * tn * 4 + 2 * tm * tn * 4) + 2 * half * tn * 2 + tm * tn * 4
    return pl.pallas_call(
        _out_proj_kernel,
        grid=(d // tn, n // tm),
        in_specs=[pl.BlockSpec((tm, half), lambda j, i: (i, 0)),
                  pl.BlockSpec((tm, half), lambda j, i: (i, 0)),
                  pl.BlockSpec((half, tn), lambda j, i: (0, j)),
                  pl.BlockSpec((half, tn), lambda j, i: (1, j)),
                  pl.BlockSpec((tm, tn), lambda j, i: (i, j)),
                  pl.BlockSpec((None, 1, tn), lambda j, i: (i // tiles_per_sample, 0, j))],
        out_specs=pl.BlockSpec((tm, tn), lambda j, i: (i, j)),
        out_shape=jax.ShapeDtypeStruct((n, d), F32),
        scratch_shapes=[pltpu.VMEM((half, tn), BF16), pltpu.VMEM((half, tn), BF16)],
        compiler_params=_params(vmem, ("arbitrary", "arbitrary")),
        name="out_proj",
    )(o, pm, w, w, x2d, g1)


def _norm2_router_kernel(x_ref, g_ref, sc_ref, sh_ref, rw_ref, h2_ref, aff_ref):
    h = _modulated_norm(x_ref[...], g_ref[...], sc_ref[...], sh_ref[...])
    h2_ref[...] = h
    h_hi, h_lo = _split_f32(h)
    r_hi, r_lo = _split_f32(rw_ref[...])
    h_hi, h_lo, r_hi, r_lo = (a.astype(BF16) for a in (h_hi, h_lo, r_hi, r_lo))
    logits = _dot_nt(r_hi, h_hi) + (_dot_nt(r_hi, h_lo) + _dot_nt(r_lo, h_hi))
    e = jnp.exp(logits - jnp.max(logits, axis=0, keepdims=True))
    aff_ref[...] = e / jnp.sum(e, axis=0, keepdims=True)


def _norm2_router(x1, g, sc, sh, router_wt):
    n, d = x1.shape
    tm = 256
    tiles_per_sample = SEQ // tm
    vmem = 2 * (2 * tm * d * 4 + N_EXPERTS * d * 4) + 6 * tm * d * 4
    return pl.pallas_call(
        _norm2_router_kernel,
        grid=(n // tm,),
        in_specs=[pl.BlockSpec((tm, d), lambda i: (i, 0)),
                  pl.BlockSpec((1, d), lambda i: (0, 0)),
                  pl.BlockSpec((None, 1, d), lambda i: (i // tiles_per_sample, 0, 0)),
                  pl.BlockSpec((None, 1, d), lambda i: (i // tiles_per_sample, 0, 0)),
                  pl.BlockSpec((N_EXPERTS, d), lambda i: (0, 0))],
        out_specs=[pl.BlockSpec((tm, d), lambda i: (i, 0)),
                   pl.BlockSpec((None, N_EXPERTS, tm), lambda i: (i // tiles_per_sample, 0, i % tiles_per_sample))],
        out_shape=[jax.ShapeDtypeStruct((n, d), F32),
                   jax.ShapeDtypeStruct((BATCH, N_EXPERTS, SEQ), F32)],
        compiler_params=_params(vmem, ("arbitrary",)),
        name="norm2_router",
    )(x1, g, sc, sh, router_wt)


ROUTE_COLS = 256
ROUTE_ROWS = 8
TOKEN_DIGIT_BITS = 6
TOKEN_DIGIT = 1 << TOKEN_DIGIT_BITS


def _route_kernel(aff_ref, slot_ref, gate_ref, idx_ref):
    a = aff_ref[...]
    rows, n = a.shape
    thr = jnp.zeros((rows, 1), I32)
    for b in range(30, -1, -1):
        cand = thr | (1 << b)
        cnt = jnp.sum((a >= pltpu.bitcast(cand, F32)).astype(I32), axis=-1, keepdims=True)
        thr = jnp.where(cnt >= CAPACITY, cand, thr)
    gt = a >= pltpu.bitcast(thr + 1, F32)
    eq = jnp.logical_and(a >= pltpu.bitcast(thr, F32), jnp.logical_not(gt))
    need = CAPACITY - jnp.sum(gt.astype(I32), axis=-1, keepdims=True)

    def prefix_count(mask):
        m = jnp.where(mask, 1.0, 0.0).astype(BF16)
        r = lax.broadcasted_iota(I32, (n, ROUTE_COLS), 0)
        c = lax.broadcasted_iota(I32, (n, ROUTE_COLS), 1)
        parts = [_dot(m, jnp.where(r < c + cb * ROUTE_COLS, 1.0, 0.0).astype(BF16))
                 for cb in range(n // ROUTE_COLS)]
        return jnp.concatenate(parts, axis=1).astype(I32)

    sel = jnp.logical_or(gt, jnp.logical_and(eq, prefix_count(eq) < need))
    slot = jnp.where(sel, prefix_count(sel), -1)
    slot_ref[...] = slot

    a_hi, a_rest = _split_f32(a)
    a_mid, a_lo = _split_f32(a_rest)
    tok = lax.broadcasted_iota(I32, (1, n), 1)
    tok_hi = (tok >> TOKEN_DIGIT_BITS).astype(F32)
    tok_lo = (tok & (TOKEN_DIGIT - 1)).astype(F32)
    slot_ids = lax.broadcasted_iota(I32, (CAPACITY, n), 0)
    pad = jnp.zeros((ROUTE_ROWS - 5, n), F32)
    for r in range(rows):
        onehot = jnp.where(slot[r:r + 1] == slot_ids, 1.0, 0.0).astype(BF16)
        pieces = jnp.concatenate([a_hi[r:r + 1], a_mid[r:r + 1], a_lo[r:r + 1], tok_hi, tok_lo, pad],
                                 axis=0).astype(BF16)
        res = _dot_nt(pieces, onehot)
        gate_ref[r:r + 1, :] = res[0:1] + res[1:2] + res[2:3]
        idx_ref[r:r + 1, :] = (res[3:4] * TOKEN_DIGIT + res[4:5]).astype(I32)


def _route(aff_rows):
    rows, n = aff_rows.shape
    vmem = 4 * ROUTE_ROWS * n * 4 + 4 * n * ROUTE_COLS * 4 + 4 * CAPACITY * n * 4
    return pl.pallas_call(
        _route_kernel,
        grid=(rows // ROUTE_ROWS,),
        in_specs=[pl.BlockSpec((ROUTE_ROWS, n), lambda i: (i, 0))],
        out_specs=[pl.BlockSpec((ROUTE_ROWS, n), lambda i: (i, 0)),
                   pl.BlockSpec((ROUTE_ROWS, CAPACITY), lambda i: (i, 0)),
                   pl.BlockSpec((ROUTE_ROWS, CAPACITY), lambda i: (i, 0))],
        out_shape=[jax.ShapeDtypeStruct((rows, n), I32),
                   jax.ShapeDtypeStruct((rows, CAPACITY), F32),
                   jax.ShapeDtypeStruct((rows, CAPACITY), I32)],
        compiler_params=_params(vmem, ("arbitrary",)),
        name="route",
    )(aff_rows)


CAST_ROWS = 64


def _moe_up_kernel(idx_ref, h2_hbm, w1_ref, w3_ref, hid_ref, rows32, rows16, sem):
    e, f = pl.program_id(0), pl.program_id(1)
    n_experts = pl.num_programs(0)
    n_ff, share, _ = rows32.shape
    m = n_ff * share

    def row_copy(src_row, group, r):
        return pltpu.make_async_copy(h2_hbm.at[pl.ds(src_row, 1), :], rows32.at[group, pl.ds(r, 1), :], sem)

    def start_group(expert, group):
        first = expert * m + group * share
        for r in range(share):
            row_copy(idx_ref[first + r], group, r).start()

    @pl.when(jnp.logical_and(e == 0, f == 0))
    def _():
        lax.fori_loop(0, n_ff, lambda grp, carry: (start_group(0, grp), carry)[1], 0)

    @pl.when(f == 0)
    def _():
        def wait_group(grp, carry):
            for r in range(share):
                row_copy(0, grp, r).wait()
            return carry
        lax.fori_loop(0, n_ff, wait_group, 0)

        def cast_group(grp, carry):
            for c0 in range(0, share, CAST_ROWS):
                dst = pl.ds(pl.multiple_of(grp * share + c0, CAST_ROWS), CAST_ROWS)
                rows16[dst, :] = rows32[grp, c0:c0 + CAST_ROWS, :].astype(BF16)
            return carry
        lax.fori_loop(0, n_ff, cast_group, 0)

    @pl.when(e + 1 < n_experts)
    def _():
        start_group(e + 1, f)

    xg = rows16[...]
    a = _dot(xg, w1_ref[...].astype(BF16))
    b = _dot(xg, w3_ref[...].astype(BF16))
    hid_ref[...] = (a * jax.nn.sigmoid(a) * b).astype(BF16)


def _moe_up(idx_table, h2, w1, w3):
    ne, m = idx_table.shape
    d = h2.shape[1]
    ff = w1.shape[2]
    tf = 256
    vmem = m * d * (4 + 2) + 2 * 2 * d * tf * 4 + 2 * d * tf * 2 + 3 * m * tf * 4 + 2 * m * tf * 2
    n_ff = ff // tf
    w_spec = pl.BlockSpec((None, d, tf), lambda e, f, idx: (e, 0, f))
    return pl.pallas_call(
        _moe_up_kernel,
        grid_spec=pltpu.PrefetchScalarGridSpec(
            num_scalar_prefetch=1,
            grid=(ne, n_ff),
            in_specs=[pl.BlockSpec(memory_space=pl.ANY), w_spec, w_spec],
            out_specs=pl.BlockSpec((None, m, tf), lambda e, f, idx: (e, 0, f)),
            scratch_shapes=[pltpu.VMEM((n_ff, m // n_ff, d), F32), pltpu.VMEM((m, d), BF16),
                            pltpu.SemaphoreType.DMA(())]),
        out_shape=jax.ShapeDtypeStruct((ne, m, ff), BF16),
        compiler_params=_params(vmem, ("arbitrary", "arbitrary")),
        name="moe_up",
    )(idx_table.reshape(ne * m), h2, w1, w3)


def _moe_down_kernel(hid_ref, w2_ref, gate_ref, y_ref):
    y = _dot(hid_ref[...], w2_ref[...].astype(BF16))
    y_ref[...] = (y * gate_ref[...]).astype(BF16)


def _moe_down(hid, w2, gate_col):
    ne, m, ff = hid.shape
    d = w2.shape[2]
    tn = 1024
    vmem = 2 * m * ff * 2 + 2 * ff * tn * 4 + ff * tn * 2 + 2 * m * 128 * 4 + 2 * m * tn * 4 + 2 * m * tn * 2
    return pl.pallas_call(
        _moe_down_kernel,
        grid=(ne, d // tn),
        in_specs=[pl.BlockSpec((None, m, ff), lambda e, j: (e, 0, 0)),
                  pl.BlockSpec((None, ff, tn), lambda e, j: (e, 0, j)),
                  pl.BlockSpec((None, m, 1), lambda e, j: (e, 0, 0))],
        out_specs=pl.BlockSpec((None, m, tn), lambda e, j: (e, 0, j)),
        out_shape=jax.ShapeDtypeStruct((ne, m, d), BF16),
        compiler_params=_params(vmem, ("arbitrary", "arbitrary")),
        name="moe_down",
    )(hid, w2, gate_col)


COMBINE_TOKENS = 128


def _moe_combine_kernel(slot_ref, y_ref, x1_ref, g2_ref, fg_ref, out_ref):
    st = slot_ref[...]
    slot_ids = lax.broadcasted_iota(I32, (COMBINE_TOKENS, CAPACITY), 1)
    onehot = jnp.concatenate(
        [jnp.where(st[:, e:e + 1] == slot_ids, 1.0, 0.0).astype(BF16) for e in range(N_EXPERTS)], axis=1)
    y = y_ref[...].reshape(N_EXPERTS * CAPACITY, y_ref.shape[-1])
    x2 = x1_ref[...] + g2_ref[...] * _dot(onehot, y)
    ms = jnp.mean(x2 * x2, axis=-1, keepdims=True)
    out_ref[...] = x2 * lax.rsqrt(ms + EPS) * fg_ref[...]


def _moe_combine(slot_ble, y, x1, g2, final_g):
    n, d = x1.shape
    tm = COMBINE_TOKENS
    tiles_per_sample = SEQ // tm
    vmem = (N_EXPERTS * CAPACITY * d * 2 + 2 * 2 * tm * d * 4 + 2 * tm * 128 * 4
            + tm * N_EXPERTS * CAPACITY * 2 + 3 * tm * d * 4)
    return pl.pallas_call(
        _moe_combine_kernel,
        grid=(BATCH, tiles_per_sample),
        in_specs=[pl.BlockSpec((None, tm, N_EXPERTS), lambda b, t: (b, t, 0)),
                  pl.BlockSpec((N_EXPERTS, None, CAPACITY, d), lambda b, t: (0, b, 0, 0),
                               pipeline_mode=pl.Buffered(1)),
                  pl.BlockSpec((tm, d), lambda b, t: (b * tiles_per_sample + t, 0)),
                  pl.BlockSpec((None, 1, d), lambda b, t: (b, 0, 0)),
                  pl.BlockSpec((1, d), lambda b, t: (0, 0))],
        out_specs=pl.BlockSpec((tm, d), lambda b, t: (b * tiles_per_sample + t, 0)),
        out_shape=jax.ShapeDtypeStruct((n, d), F32),
        compiler_params=_params(vmem, ("arbitrary", "arbitrary")),
        name="moe_combine",
    )(slot_ble, y, x1, g2, final_g)


def kernel(x, c, ctx, c_ctx, ada_w, ada_b, norm1_g, norm2_g, w_in, lb_param, hg_norm_g, pool_w, pool_scale,
           w_out, router_w, moe_w1, moe_w3, moe_w2, final_norm_g):
    nb, seq, d = x.shape
    assert (nb, seq, d) == (BATCH, SEQ, D_MODEL) and ctx.shape[1] == CTX_LEN
    assert ada_w.shape[0] == 1 and lb_param.shape[0] == 2, "single-layer block: layer 0 uses lower-bound row 0"
    x2d = x.reshape(nb * seq, d)
    ctx2d = ctx.reshape(nb * CTX_LEN, d)

    cvecs = jnp.zeros((ADA_ROWS, d), F32).at[:nb].set(c).at[nb].set(c_ctx)
    mod = _ada_mod(cvecs, ada_w[0], ada_b).reshape(ADA_ROWS, 6, d)
    sh1, sc1, g1, sh2, sc2, g2 = (mod[:nb, k][:, None, :] for k in range(6))
    csh1, csc1 = mod[nb:nb + 1, 0][:, None, :], mod[nb:nb + 1, 1][:, None, :]

    lbp = lb_param.reshape(lb_param.shape[0], 2 * HG_WIDTH)

    h_c = _norm1(ctx2d, norm1_g, csc1, csh1, rows_per_sample=nb * CTX_LEN)
    f_c, k_c, i_c = _in_proj(h_c, lbp, w_in[0], n_cols=3 * HG_WIDTH)
    s_f, s_b = _ctx_states(f_c, k_c, i_c)

    h_x = _norm1(x2d, norm1_g, sc1, sh1, rows_per_sample=seq)
    fdec, kk, rest = _in_proj(h_x, lbp, w_in[0], n_cols=w_in.shape[2])
    o = _hgrn2_scan(fdec, kk, rest, s_f, s_b, hg_norm_g)
    pm = _grid_pool(rest, pool_w[0], pool_scale)
    x1 = _out_proj(o, pm, w_out[0], x2d, g1)

    h2, aff_t = _norm2_router(x1, norm2_g, sc2, sh2, router_w[0].T)
    slot, gates, tok = _route(aff_t.reshape(nb * N_EXPERTS, seq))
    slot = slot.reshape(nb, N_EXPERTS, seq)
    h2_rows = tok.reshape(nb, N_EXPERTS, CAPACITY) + (jnp.arange(nb, dtype=I32) * seq)[:, None, None]
    hid = _moe_up(h2_rows.transpose(1, 0, 2).reshape(N_EXPERTS, nb * CAPACITY), h2, moe_w1[0], moe_w3[0])
    gate_col = gates.reshape(nb, N_EXPERTS, CAPACITY).transpose(1, 0, 2).reshape(N_EXPERTS, nb * CAPACITY, 1)
    y = _moe_down(hid, moe_w2[0], gate_col).reshape(N_EXPERTS, nb, CAPACITY, d)
    out = _moe_combine(slot.transpose(0, 2, 1), y, x1, g2, final_norm_g[None, :])
    return out.reshape(nb, seq, d).astype(x.dtype)
```

```python
import jax
import jax.numpy as jnp
import numpy as np
from jax import lax
from jax.experimental import pallas as pl
from jax.experimental.pallas import tpu as pltpu

F32 = jnp.float32
BF16 = jnp.bfloat16
I32 = jnp.int32

D_MODEL = 4096
BATCH = 4
SEQ = 2048
GRID_W = 64
CTX_LEN = 256
HG_HEADS = 16
HG_HEAD_DIM = 128
HG_WIDTH = HG_HEADS * HG_HEAD_DIM
POOL_WIDTH = D_MODEL - HG_WIDTH
POOL_WINDOWS = (2, 4, 8, 16)
POOL_GROUP = POOL_WIDTH // len(POOL_WINDOWS)
N_EXPERTS = 16
EXPERT_FF = D_MODEL // 2
CAPACITY = 2 * SEQ // N_EXPERTS
EPS = 1e-6

V7X_VMEM_BYTES = 64 * 1024 * 1024
V7X_VMEM_HEADROOM_BYTES = 10 * 1024 * 1024
V7X_SUBLANES = 8
V7X_LANES = 128
NORM_ROWS = 16
ADA_ROWS = 8

SCAN_CHUNK = 128
SCAN_LEVELS = 7
SCAN_UNROLL = 4

_NT = (((1,), (1,)), ((), ()))
_TN = (((0,), (0,)), ((), ()))


def _dot(a, b):
    return jnp.dot(a, b, preferred_element_type=F32)


def _dot_nt(a, b):
    return lax.dot_general(a, b, _NT, preferred_element_type=F32)


def _dot_tn(a, b):
    return lax.dot_general(a, b, _TN, preferred_element_type=F32)


def _split_f32(x):
    hi = x.astype(BF16).astype(F32)
    return hi, x - hi


def _params(vmem_bytes, semantics):
    limit = min(int(vmem_bytes) + V7X_VMEM_HEADROOM_BYTES, V7X_VMEM_BYTES)
    return pltpu.CompilerParams(dimension_semantics=semantics, vmem_limit_bytes=limit)


def _ada_kernel(c_ref, w_ref, b_ref, o_ref):
    c = c_ref[...]
    s = c * jax.nn.sigmoid(c)
    hi, lo = _split_f32(s)
    lhs = jnp.concatenate([hi, lo], axis=0).astype(BF16)
    r = _dot(lhs, w_ref[...].astype(BF16))
    o_ref[...] = r[:ADA_ROWS] + r[ADA_ROWS:] + b_ref[...]


def _ada_mod(cvecs, w, b):
    d, n = w.shape
    tn = 512
    vmem = 2 * d * tn * 4 + d * tn * 2 + 4 * ADA_ROWS * d * 4
    return pl.pallas_call(
        _ada_kernel,
        grid=(n // tn,),
        in_specs=[pl.BlockSpec((ADA_ROWS, d), lambda j: (0, 0)),
                  pl.BlockSpec((d, tn), lambda j: (0, j)),
                  pl.BlockSpec((1, tn), lambda j: (0, j))],
        out_specs=pl.BlockSpec((ADA_ROWS, tn), lambda j: (0, j)),
        out_shape=jax.ShapeDtypeStruct((ADA_ROWS, n), F32),
        compiler_params=_params(vmem, ("arbitrary",)),
        name="ada_mod",
    )(cvecs, w, b)


def _modulated_norm(x, g, scale, shift):
    ms = jnp.mean(x * x, axis=-1, keepdims=True)
    return (x * lax.rsqrt(ms + EPS) * g) * (1.0 + scale) + shift


def _norm1_kernel(x_ref, g_ref, sc_ref, sh_ref, h_ref):
    g, sc, sh = g_ref[...], sc_ref[...], sh_ref[...]

    def norm_rows(r, carry):
        rows = pl.ds(pl.multiple_of(r * NORM_ROWS, NORM_ROWS), NORM_ROWS)
        h_ref[rows, :] = _modulated_norm(x_ref[rows, :], g, sc, sh).astype(BF16)
        return carry

    lax.fori_loop(0, x_ref.shape[0] // NORM_ROWS, norm_rows, 0, unroll=2)


def _norm1(x2d, g, sc, sh, *, rows_per_sample):
    n, d = x2d.shape
    tm = 512
    tiles_per_sample = rows_per_sample // tm
    mod = pl.BlockSpec((None, 1, d), lambda i: (i // tiles_per_sample, 0, 0))
    vmem = 2 * tm * d * (4 + 2) + 8 * NORM_ROWS * d * 4
    return pl.pallas_call(
        _norm1_kernel,
        grid=(n // tm,),
        in_specs=[pl.BlockSpec((tm, d), lambda i: (i, 0)), pl.BlockSpec((1, d), lambda i: (0, 0)), mod, mod],
        out_specs=pl.BlockSpec((tm, d), lambda i: (i, 0)),
        out_shape=jax.ShapeDtypeStruct((n, d), BF16),
        compiler_params=_params(vmem, ("arbitrary",)),
        name="norm1",
    )(x2d, g, sc, sh)


def _gate_proj_kernel(h_ref, lbp_ref, w_ref, f_ref, k_ref, w_scr):
    @pl.when(pl.program_id(1) == 0)
    def _():
        w_scr[...] = w_ref[...].astype(BF16)

    z = _dot(h_ref[...], w_scr[...])
    p = lbp_ref[...]
    e = jnp.exp(p - jnp.max(p, axis=0, keepdims=True))
    lb = e[0:1] / jnp.sum(e, axis=0, keepdims=True)
    f = lb + (1.0 - lb) * jax.nn.sigmoid(z)
    f_ref[...] = f
    k_ref[...] = (1.0 - f).astype(BF16)


def _rest_proj_kernel(h_ref, w_ref, r_ref, w_scr):
    @pl.when(pl.program_id(1) == 0)
    def _():
        w_scr[...] = w_ref[...].astype(BF16)

    r_ref[...] = _dot(h_ref[...], w_scr[...]).astype(BF16)


def _in_proj(h, lbp, w, *, n_cols):
    n, d = h.shape
    tm, tn = 1024, 512
    n_gate = 2 * HG_WIDTH // tn
    w_bytes = 2 * d * tn * 4 + d * tn * 2
    vmem = 2 * tm * d * 2 + w_bytes + 2 * tm * tn * (4 + 2) + 4 * tm * tn * 4
    fdec, kk = pl.pallas_call(
        _gate_proj_kernel,
        grid=(n_gate, n // tm),
        in_specs=[pl.BlockSpec((tm, d), lambda j, i: (i, 0)),
                  pl.BlockSpec((2, tn), lambda j, i: (0, j)),
                  pl.BlockSpec((d, tn), lambda j, i: (0, j))],
        out_specs=[pl.BlockSpec((tm, tn), lambda j, i: (i, j)),
                   pl.BlockSpec((tm, tn), lambda j, i: (i, j))],
        out_shape=[jax.ShapeDtypeStruct((n, 2 * HG_WIDTH), F32),
                   jax.ShapeDtypeStruct((n, 2 * HG_WIDTH), BF16)],
        scratch_shapes=[pltpu.VMEM((d, tn), BF16)],
        compiler_params=_params(vmem, ("arbitrary", "arbitrary")),
        name="gate_proj",
    )(h, lbp, w)
    vmem = 2 * tm * d * 2 + w_bytes + 2 * tm * tn * 2 + 2 * tm * tn * 4
    rest = pl.pallas_call(
        _rest_proj_kernel,
        grid=((n_cols - 2 * HG_WIDTH) // tn, n // tm),
        in_specs=[pl.BlockSpec((tm, d), lambda j, i: (i, 0)),
                  pl.BlockSpec((d, tn), lambda j, i: (0, n_gate + j))],
        out_specs=pl.BlockSpec((tm, tn), lambda j, i: (i, j)),
        out_shape=jax.ShapeDtypeStruct((n, n_cols - 2 * HG_WIDTH), BF16),
        scratch_shapes=[pltpu.VMEM((d, tn), BF16)],
        compiler_params=_params(vmem, ("arbitrary", "arbitrary")),
        name="rest_proj",
    )(h, w)
    return fdec, kk, rest


def _ctx_tri():
    s = np.arange(CTX_LEN)[:, None]
    u = np.arange(CTX_LEN)[None, :]
    tri = np.stack([u > s, u < s]).astype(np.float32)
    return jnp.asarray(np.concatenate([tri, tri], axis=2), dtype=BF16)


CTX_HEADS = 4


def _ctx_state_kernel(ff_ref, fb_ref, kf_ref, kb_ref, i_ref, tri_ref, sf_ref, sb_ref):
    dh = HG_HEAD_DIM

    def state(f, k, v, a):
        hi, lo = _split_f32(jnp.log(f))
        g = _dot(a, jnp.concatenate([hi, lo], axis=0).astype(BF16))
        kd = (k.astype(F32) * jnp.exp(g)).astype(BF16)
        return _dot_tn(v, kd)

    for j in range(CTX_HEADS):
        cols = slice(j * dh, (j + 1) * dh)
        v = i_ref[:, cols]
        sf_ref[j] = state(ff_ref[:, cols], kf_ref[:, cols], v, tri_ref[0])
        sb_ref[j] = state(fb_ref[:, cols], kb_ref[:, cols], v, tri_ref[1])


def _ctx_states(fdec, kk, vi):
    t, dh, nh = CTX_LEN, HG_HEAD_DIM, HG_HEADS
    groups = nh // CTX_HEADS
    blk = lambda off: pl.BlockSpec((t, CTX_HEADS * dh), lambda b, h: (b, off + h))
    st = pl.BlockSpec((None, CTX_HEADS, dh, dh), lambda b, h: (b, h, 0, 0))
    shape = jax.ShapeDtypeStruct((BATCH, nh, dh, dh), F32)
    vmem = CTX_HEADS * 2 * (2 * t * dh * 4 + 3 * t * dh * 2 + 2 * dh * dh * 4) + 2 * 2 * t * 2 * t * 2
    return pl.pallas_call(
        _ctx_state_kernel,
        grid=(BATCH, groups),
        in_specs=[blk(0), blk(groups), blk(0), blk(groups), blk(0),
                  pl.BlockSpec((2, t, 2 * t), lambda b, h: (0, 0, 0))],
        out_specs=[st, st],
        out_shape=[shape, shape],
        compiler_params=_params(vmem, ("arbitrary", "arbitrary")),
        name="ctx_state",
    )(fdec, fdec, kk, kk, vi, _ctx_tri())


def _pair_levels():
    c, nl = SCAN_CHUNK, SCAN_LEVELS
    t = np.arange(c)[:, None]
    u = np.arange(c)[None, :]
    lev = np.floor(np.log2(np.maximum(t ^ u, 1))).astype(np.int32)
    lidx = np.stack([np.where(t > u, lev, np.where(t == u, nl, -1)),
                     np.where(t < u, lev, np.where(t == u, nl, -1))]).astype(np.int32)
    return jnp.asarray(lidx)


def _swap_halves(x, h, t_idx):
    c = x.shape[0]
    assert 2 * h <= V7X_SUBLANES
    x3 = x.reshape(c // V7X_SUBLANES, V7X_SUBLANES, x.shape[1])
    if 2 * h == V7X_SUBLANES:
        return pltpu.roll(x3, h, axis=1).reshape(x.shape)
    up = pltpu.roll(x3, h, axis=1).reshape(x.shape)
    down = pltpu.roll(x3, V7X_SUBLANES - h, axis=1).reshape(x.shape)
    return jnp.where((t_idx & h) != 0, up, down)


def _scan_chunk(direction, f, k, q, v, st_ref, lidx, t_idx):
    c, nl = SCAN_CHUNK, SCAN_LEVELS
    qf = q.astype(F32)
    kf = k.astype(F32)
    q_dec = f
    k_dec = jnp.ones_like(f)
    total = f
    scores = jnp.zeros((c, c), F32)
    sublane = lax.broadcasted_iota(I32, (1, V7X_SUBLANES, f.shape[1]), 1)
    tiles = lambda a: a.reshape(c // V7X_SUBLANES, V7X_SUBLANES, a.shape[1])
    blocks = lambda a, h: [a[i * h:(i + 1) * h] for i in range(c // h)]
    for l in range(nl):
        h = 1 << l
        if h < V7X_SUBLANES:
            bit = (sublane & h) != 0
            is_query = bit if direction == 0 else jnp.logical_not(bit)
            x = jnp.where(is_query, tiles(q_dec * qf), tiles(k_dec * kf)).reshape(c, -1).astype(BF16)
            scores = jnp.where(lidx == l, _dot_nt(x, x), scores)
            other = _swap_halves(total, h, t_idx)
            q_dec = (tiles(q_dec) * jnp.where(is_query, tiles(other), 1.0)).reshape(c, -1)
            k_dec = (tiles(k_dec) * jnp.where(is_query, 1.0, tiles(other))).reshape(c, -1)
            total = total * other
        else:
            n_blocks = c // h
            is_query = [((i & 1) == 1) == (direction == 0) for i in range(n_blocks)]
            qd, kd, qb, kb, tb = (blocks(a, h) for a in (q_dec, k_dec, qf, kf, total))
            x = jnp.concatenate([qd[i] * qb[i] if is_query[i] else kd[i] * kb[i] for i in range(n_blocks)],
                                axis=0).astype(BF16)
            pb, sb, lb = (blocks(a, h) for a in (_dot_nt(x, x), scores, lidx))
            scores = jnp.concatenate([jnp.where(lb[i] == l, pb[i], sb[i]) if is_query[i] else sb[i]
                                      for i in range(n_blocks)], axis=0)
            ob = [tb[i ^ 1] for i in range(n_blocks)]
            q_dec = jnp.concatenate([qd[i] * ob[i] if is_query[i] else qd[i] for i in range(n_blocks)], axis=0)
            k_dec = jnp.concatenate([kd[i] if is_query[i] else kd[i] * ob[i] for i in range(n_blocks)], axis=0)
            total = jnp.concatenate([tb[i] * ob[i] for i in range(n_blocks)], axis=0)
    own = jnp.sum(qf * kf, axis=-1, keepdims=True)
    q_in = (q_dec * qf).astype(BF16)
    k_st = (k_dec * kf).astype(BF16)
    st = st_ref[...]
    o = _dot(scores.astype(BF16), v) + _dot_nt(q_in, st.astype(BF16)) + own * v.astype(F32)
    st_ref[...] = total[0:1] * st + _dot_tn(v, k_st)
    return o


def _scan_kernel(ff_ref, fb_ref, kf_ref, kb_ref, i_ref, q_ref, g_ref, s0f_ref, s0b_ref, ng_ref,
                 lidx_ref, o_ref, stf_ref, stb_ref, part_ref):
    c = SCAN_CHUNK
    n_chunks = SEQ // c
    stf_ref[...] = s0f_ref[...]
    stb_ref[...] = s0b_ref[...]
    t_idx = lax.broadcasted_iota(I32, (c, HG_HEAD_DIM), 0)
    ng = ng_ref[...]

    def chunk(direction, r0):
        rows = pl.ds(r0, c)
        f_ref, k_ref, st_ref = (ff_ref, kf_ref, stf_ref) if direction == 0 else (fb_ref, kb_ref, stb_ref)
        return _scan_chunk(direction, f_ref[rows, :], k_ref[rows, :], q_ref[rows, :], i_ref[rows, :],
                           st_ref, lidx_ref[direction], t_idx)

    def finalize(o, r0):
        rows = pl.ds(r0, c)
        gate = g_ref[rows, :].astype(F32)
        ms = jnp.mean(o * o, axis=-1, keepdims=True)
        o_ref[rows, :] = (o * lax.rsqrt(ms + EPS) * ng * (gate * jax.nn.sigmoid(gate))).astype(BF16)

    def starts(n):
        return pl.multiple_of(n * c, c), pl.multiple_of((n_chunks - 1 - n) * c, c)

    def first_half(n, carry):
        rf, rb = starts(n)
        part_ref[pl.ds(rf, c), :] = chunk(0, rf)
        part_ref[pl.ds(rb, c), :] = chunk(1, rb)
        return carry

    def second_half(n, carry):
        rf, rb = starts(n)
        finalize(chunk(0, rf) + part_ref[pl.ds(rf, c), :], rf)
        finalize(chunk(1, rb) + part_ref[pl.ds(rb, c), :], rb)
        return carry

    lax.fori_loop(0, n_chunks // 2, first_half, 0, unroll=SCAN_UNROLL)
    lax.fori_loop(n_chunks // 2, n_chunks, second_half, 0, unroll=SCAN_UNROLL)


def _hgrn2_scan(fdec, kk, rest, s0f, s0b, norm_g):
    dh, nh = HG_HEAD_DIM, HG_HEADS
    lidx = _pair_levels()
    blk = lambda off: pl.BlockSpec((SEQ, dh), lambda b, h: (b, off + h))
    st = pl.BlockSpec((None, None, dh, dh), lambda b, h: (b, h, 0, 0))
    vmem = (2 * (2 * SEQ * dh * 4 + 6 * SEQ * dh * 2 + 2 * dh * dh * 4) + SEQ * dh * 4 + 2 * dh * dh * 4
            + 2 * 2 * SCAN_CHUNK * V7X_LANES * 4)
    return pl.pallas_call(
        _scan_kernel,
        grid=(BATCH, nh),
        in_specs=[blk(0), blk(nh), blk(0), blk(nh), blk(0), blk(nh), blk(2 * nh), st, st,
                  pl.BlockSpec((1, dh), lambda b, h: (0, h)),
                  pl.BlockSpec(lidx.shape, lambda b, h: (0, 0, 0))],
        out_specs=blk(0),
        out_shape=jax.ShapeDtypeStruct((BATCH * SEQ, HG_WIDTH), BF16),
        scratch_shapes=[pltpu.VMEM((dh, dh), F32), pltpu.VMEM((dh, dh), F32), pltpu.VMEM((SEQ, dh), F32)],
        compiler_params=_params(vmem, ("arbitrary", "arbitrary")),
        name="hgrn2_scan",
    )(fdec, fdec, kk, kk, rest, rest, rest, s0f, s0b, norm_g, lidx)


POOL_PIECE = 256


def _box_bounds(n, w):
    start = np.arange(n) - w // 2
    return np.clip(start, 0, n), np.clip(start + w, 0, n)


def _pool_consts():
    rows = SEQ // GRID_W
    col_box = np.zeros((len(POOL_WINDOWS), POOL_PIECE, POOL_PIECE), np.float32)
    cnt = np.zeros((len(POOL_WINDOWS), SEQ, 1), np.float32)
    cc = np.arange(GRID_W)[None, :]
    for gi, w in enumerate(POOL_WINDOWS):
        c0, c1 = _box_bounds(GRID_W, w)
        r0, r1 = _box_bounds(rows, w)
        wc = ((cc >= c0[:, None]) & (cc < c1[:, None])).astype(np.float32)
        col_box[gi] = np.kron(np.eye(POOL_PIECE // GRID_W, dtype=np.float32), wc)
        cnt[gi] = ((r1 - r0)[:, None] * (c1 - c0)[None, :]).reshape(SEQ, 1)
    return jnp.asarray(col_box, dtype=BF16), jnp.asarray(cnt)


def _pool_kernel(v_ref, box_ref, cnt_ref, pw_ref, ps_ref, o_ref, pre_ref, diff_ref):
    gi = pl.program_id(1)
    rows = SEQ // GRID_W
    gw = GRID_W
    box = box_ref[...]
    pre_ref[0:gw, :] = jnp.zeros((gw, POOL_GROUP), F32)
    for p in range(SEQ // POOL_PIECE):
        yc = _dot(box, v_ref[p * POOL_PIECE:(p + 1) * POOL_PIECE, :])
        for rr in range(POOL_PIECE // gw):
            r = p * (POOL_PIECE // gw) + rr
            pre_ref[(r + 1) * gw:(r + 2) * gw, :] = pre_ref[r * gw:(r + 1) * gw, :] + yc[rr * gw:(rr + 1) * gw]
    for k, w in enumerate(POOL_WINDOWS):
        @pl.when(gi == k)
        def _(w=w):
            r0, r1 = _box_bounds(rows, w)
            for r in range(rows):
                sl = slice(r * gw, (r + 1) * gw)
                box_sum = pre_ref[int(r1[r]) * gw:(int(r1[r]) + 1) * gw, :] - pre_ref[int(r0[r]) * gw:(int(r0[r]) + 1) * gw, :]
                diff_ref[sl, :] = (box_sum / cnt_ref[sl, :] - v_ref[sl, :].astype(F32)).astype(BF16)
    o_ref[...] = (_dot(diff_ref[...], pw_ref[...].astype(BF16)) * ps_ref[...]).astype(BF16)


def _grid_pool(rest, pool_w, pool_scale):
    ng, pg = len(POOL_WINDOWS), POOL_GROUP
    col_box, cnt = _pool_consts()
    v_off = 3 * HG_WIDTH // pg
    vmem = (2 * (2 * SEQ * pg * 2 + POOL_PIECE * POOL_PIECE * 2 + SEQ * 128 * 4 + pg * pg * 4)
            + (SEQ + GRID_W) * pg * 4 + SEQ * pg * 2 + SEQ * pg * 4)
    return pl.pallas_call(
        _pool_kernel,
        grid=(BATCH, ng),
        in_specs=[pl.BlockSpec((SEQ, pg), lambda b, k: (b, v_off + k)),
                  pl.BlockSpec((None, POOL_PIECE, POOL_PIECE), lambda b, k: (k, 0, 0)),
                  pl.BlockSpec((None, SEQ, 1), lambda b, k: (k, 0, 0)),
                  pl.BlockSpec((None, pg, pg), lambda b, k: (k, 0, 0)),
                  pl.BlockSpec((1, pg), lambda b, k: (0, k))],
        out_specs=pl.BlockSpec((SEQ, pg), lambda b, k: (b, k)),
        out_shape=jax.ShapeDtypeStruct((BATCH * SEQ, POOL_WIDTH), BF16),
        scratch_shapes=[pltpu.VMEM((SEQ + GRID_W, pg), F32), pltpu.VMEM((SEQ, pg), BF16)],
        compiler_params=_params(vmem, ("arbitrary", "arbitrary")),
        name="grid_pool",
    )(rest, col_box, cnt, pool_w, pool_scale)


def _out_proj_kernel(o_ref, pm_ref, wa_ref, wb_ref, x_ref, g1_ref, out_ref, wa_scr, wb_scr):
    @pl.when(pl.program_id(1) == 0)
    def _():
        wa_scr[...] = wa_ref[...].astype(BF16)
        wb_scr[...] = wb_ref[...].astype(BF16)

    acc = _dot(o_ref[...], wa_scr[...]) + _dot(pm_ref[...], wb_scr[...])
    out_ref[...] = x_ref[...] + g1_ref[...] * acc


def _out_proj(o, pm, w, x2d, g1):
    n, d = x2d.shape
    tm, tn = 1024, 512
    half = w.shape[0] // 2
    tiles_per_sample = SEQ // tm
    vmem = 2 * (2 * tm * half * 2 + 2 * half * tn * 4 + 2 * tm * tn * 4) + 2 * half * tn * 2 + tm * tn * 4
    return pl.pallas_call(
        _out_proj_kernel,
        grid=(d // tn, n // tm),
        in_specs=[pl.BlockSpec((tm, half), lambda j, i: (i, 0)),
                  pl.BlockSpec((tm, half), lambda j, i: (i, 0)),
                  pl.BlockSpec((half, tn), lambda j, i: (0, j)),
                  pl.BlockSpec((half, tn), lambda j, i: (1, j)),
                  pl.BlockSpec((tm, tn), lambda j, i: (i, j)),
                  pl.BlockSpec((None, 1, tn), lambda j, i: (i // tiles_per_sample, 0, j))],
        out_specs=pl.BlockSpec((tm, tn), lambda j, i: (i, j)),
        out_shape=jax.ShapeDtypeStruct((n, d), F32),
        scratch_shapes=[pltpu.VMEM((half, tn), BF16), pltpu.VMEM((half, tn), BF16)],
        compiler_params=_params(vmem, ("arbitrary", "arbitrary")),
        name="out_proj",
    )(o, pm, w, w, x2d, g1)


def _norm2_router_kernel(x_ref, g_ref, sc_ref, sh_ref, rw_ref, h2_ref, aff_ref):
    h = _modulated_norm(x_ref[...], g_ref[...], sc_ref[...], sh_ref[...])
    h2_ref[...] = h
    h_hi, h_lo = _split_f32(h)
    r_hi, r_lo = _split_f32(rw_ref[...])
    h_hi, h_lo, r_hi, r_lo = (a.astype(BF16) for a in (h_hi, h_lo, r_hi, r_lo))
    logits = _dot_nt(r_hi, h_hi) + (_dot_nt(r_hi, h_lo) + _dot_nt(r_lo, h_hi))
    e = jnp.exp(logits - jnp.max(logits, axis=0, keepdims=True))
    aff_ref[...] = e / jnp.sum(e, axis=0, keepdims=True)


def _norm2_router(x1, g, sc, sh, router_wt):
    n, d = x1.shape
    tm = 256
    tiles_per_sample = SEQ // tm
    vmem = 2 * (2 * tm * d * 4 + N_EXPERTS * d * 4) + 6 * tm * d * 4
    return pl.pallas_call(
        _norm2_router_kernel,
        grid=(n // tm,),
        in_specs=[pl.BlockSpec((tm, d), lambda i: (i, 0)),
                  pl.BlockSpec((1, d), lambda i: (0, 0)),
                  pl.BlockSpec((None, 1, d), lambda i: (i // tiles_per_sample, 0, 0)),
                  pl.BlockSpec((None, 1, d), lambda i: (i // tiles_per_sample, 0, 0)),
                  pl.BlockSpec((N_EXPERTS, d), lambda i: (0, 0))],
        out_specs=[pl.BlockSpec((tm, d), lambda i: (i, 0)),
                   pl.BlockSpec((None, N_EXPERTS, tm), lambda i: (i // tiles_per_sample, 0, i % tiles_per_sample))],
        out_shape=[jax.ShapeDtypeStruct((n, d), F32),
                   jax.ShapeDtypeStruct((BATCH, N_EXPERTS, SEQ), F32)],
        compiler_params=_params(vmem, ("arbitrary",)),
        name="norm2_router",
    )(x1, g, sc, sh, router_wt)


ROUTE_COLS = 256
ROUTE_ROWS = 8
TOKEN_DIGIT_BITS = 6
TOKEN_DIGIT = 1 << TOKEN_DIGIT_BITS


def _route_kernel(aff_ref, slot_ref, gate_ref, idx_ref):
    a = aff_ref[...]
    rows, n = a.shape
    thr = jnp.zeros((rows, 1), I32)
    for b in range(30, -1, -1):
        cand = thr | (1 << b)
        cnt = jnp.sum((a >= pltpu.bitcast(cand, F32)).astype(I32), axis=-1, keepdims=True)
        thr = jnp.where(cnt >= CAPACITY, cand, thr)
    gt = a >= pltpu.bitcast(thr + 1, F32)
    eq = jnp.logical_and(a >= pltpu.bitcast(thr, F32), jnp.logical_not(gt))
    need = CAPACITY - jnp.sum(gt.astype(I32), axis=-1, keepdims=True)

    def prefix_count(mask):
        m = jnp.where(mask, 1.0, 0.0).astype(BF16)
        r = lax.broadcasted_iota(I32, (n, ROUTE_COLS), 0)
        c = lax.broadcasted_iota(I32, (n, ROUTE_COLS), 1)
        parts = [_dot(m, jnp.where(r < c + cb * ROUTE_COLS, 1.0, 0.0).astype(BF16))
                 for cb in range(n // ROUTE_COLS)]
        return jnp.concatenate(parts, axis=1).astype(I32)

    sel = jnp.logical_or(gt, jnp.logical_and(eq, prefix_count(eq) < need))
    slot = jnp.where(sel, prefix_count(sel), -1)
    slot_ref[...] = slot

    a_hi, a_rest = _split_f32(a)
    a_mid, a_lo = _split_f32(a_rest)
    tok = lax.broadcasted_iota(I32, (1, n), 1)
    tok_hi = (tok >> TOKEN_DIGIT_BITS).astype(F32)
    tok_lo = (tok & (TOKEN_DIGIT - 1)).astype(F32)
    slot_ids = lax.broadcasted_iota(I32, (CAPACITY, n), 0)
    pad = jnp.zeros((ROUTE_ROWS - 5, n), F32)
    for r in range(rows):
        onehot = jnp.where(slot[r:r + 1] == slot_ids, 1.0, 0.0).astype(BF16)
        pieces = jnp.concatenate([a_hi[r:r + 1], a_mid[r:r + 1], a_lo[r:r + 1], tok_hi, tok_lo, pad],
                                 axis=0).astype(BF16)
        res = _dot_nt(pieces, onehot)
        gate_ref[r:r + 1, :] = res[0:1] + res[1:2] + res[2:3]
        idx_ref[r:r + 1, :] = (res[3:4] * TOKEN_DIGIT + res[4:5]).astype(I32)


def _route(aff_rows):
    rows, n = aff_rows.shape
    vmem = 4 * ROUTE_ROWS * n * 4 + 4 * n * ROUTE_COLS * 4 + 4 * CAPACITY * n * 4
    return pl.pallas_call(
        _route_kernel,
        grid=(rows // ROUTE_ROWS,),
        in_specs=[pl.BlockSpec((ROUTE_ROWS, n), lambda i: (i, 0))],
        out_specs=[pl.BlockSpec((ROUTE_ROWS, n), lambda i: (i, 0)),
                   pl.BlockSpec((ROUTE_ROWS, CAPACITY), lambda i: (i, 0)),
                   pl.BlockSpec((ROUTE_ROWS, CAPACITY), lambda i: (i, 0))],
        out_shape=[jax.ShapeDtypeStruct((rows, n), I32),
                   jax.ShapeDtypeStruct((rows, CAPACITY), F32),
                   jax.ShapeDtypeStruct((rows, CAPACITY), I32)],
        compiler_params=_params(vmem, ("arbitrary",)),
        name="route",
    )(aff_rows)


CAST_ROWS = 64


def _moe_up_kernel(idx_ref, h2_hbm, w1_ref, w3_ref, hid_ref, rows32, rows16, sem):
    e, f = pl.program_id(0), pl.program_id(1)
    n_experts = pl.num_programs(0)
    n_ff, share, _ = rows32.shape
    m = n_ff * share

    def row_copy(src_row, group, r):
        return pltpu.make_async_copy(h2_hbm.at[pl.ds(src_row, 1), :], rows32.at[group, pl.ds(r, 1), :], sem)

    def start_group(expert, group):
        first = expert * m + group * share
        for r in range(share):
            row_copy(idx_ref[first + r], group, r).start()

    @pl.when(jnp.logical_and(e == 0, f == 0))
    def _():
        lax.fori_loop(0, n_ff, lambda grp, carry: (start_group(0, grp), carry)[1], 0)

    @pl.when(f == 0)
    def _():
        def wait_group(grp, carry):
            for r in range(share):
                row_copy(0, grp, r).wait()
            return carry
        lax.fori_loop(0, n_ff, wait_group, 0)

        def cast_group(grp, carry):
            for c0 in range(0, share, CAST_ROWS):
                dst = pl.ds(pl.multiple_of(grp * share + c0, CAST_ROWS), CAST_ROWS)
                rows16[dst, :] = rows32[grp, c0:c0 + CAST_ROWS, :].astype(BF16)
            return carry
        lax.fori_loop(0, n_ff, cast_group, 0)

    @pl.when(e + 1 < n_experts)
    def _():
        start_group(e + 1, f)

    xg = rows16[...]
    a = _dot(xg, w1_ref[...].astype(BF16))
    b = _dot(xg, w3_ref[...].astype(BF16))
    hid_ref[...] = (a * jax.nn.sigmoid(a) * b).astype(BF16)


def _moe_up(idx_table, h2, w1, w3):
    ne, m = idx_table.shape
    d = h2.shape[1]
    ff = w1.shape[2]
    tf = 256
    vmem = m * d * (4 + 2) + 2 * 2 * d * tf * 4 + 2 * d * tf * 2 + 3 * m * tf * 4 + 2 * m * tf * 2
    n_ff = ff // tf
    w_spec = pl.BlockSpec((None, d, tf), lambda e, f, idx: (e, 0, f))
    return pl.pallas_call(
        _moe_up_kernel,
        grid_spec=pltpu.PrefetchScalarGridSpec(
            num_scalar_prefetch=1,
            grid=(ne, n_ff),
            in_specs=[pl.BlockSpec(memory_space=pl.ANY), w_spec, w_spec],
            out_specs=pl.BlockSpec((None, m, tf), lambda e, f, idx: (e, 0, f)),
            scratch_shapes=[pltpu.VMEM((n_ff, m // n_ff, d), F32), pltpu.VMEM((m, d), BF16),
                            pltpu.SemaphoreType.DMA(())]),
        out_shape=jax.ShapeDtypeStruct((ne, m, ff), BF16),
        compiler_params=_params(vmem, ("arbitrary", "arbitrary")),
        name="moe_up",
    )(idx_table.reshape(ne * m), h2, w1, w3)


def _moe_down_kernel(hid_ref, w2_ref, gate_ref, y_ref):
    y = _dot(hid_ref[...], w2_ref[...].astype(BF16))
    y_ref[...] = (y * gate_ref[...]).astype(BF16)


def _moe_down(hid, w2, gate_col):
    ne, m, ff = hid.shape
    d = w2.shape[2]
    tn = 1024
    vmem = 2 * m * ff * 2 + 2 * ff * tn * 4 + ff * tn * 2 + 2 * m * 128 * 4 + 2 * m * tn * 4 + 2 * m * tn * 2
    return pl.pallas_call(
        _moe_down_kernel,
        grid=(ne, d // tn),
        in_specs=[pl.BlockSpec((None, m, ff), lambda e, j: (e, 0, 0)),
                  pl.BlockSpec((None, ff, tn), lambda e, j: (e, 0, j)),
                  pl.BlockSpec((None, m, 1), lambda e, j: (e, 0, 0))],
        out_specs=pl.BlockSpec((None, m, tn), lambda e, j: (e, 0, j)),
        out_shape=jax.ShapeDtypeStruct((ne, m, d), BF16),
        compiler_params=_params(vmem, ("arbitrary", "arbitrary")),
        name="moe_down",
    )(hid, w2, gate_col)


COMBINE_TOKENS = 128


def _moe_combine_kernel(slot_ref, y_ref, x1_ref, g2_ref, fg_ref, out_ref):
    st = slot_ref[...]
    slot_ids = lax.broadcasted_iota(I32, (COMBINE_TOKENS, CAPACITY), 1)
    onehot = jnp.concatenate(
        [jnp.where(st[:, e:e + 1] == slot_ids, 1.0, 0.0).astype(BF16) for e in range(N_EXPERTS)], axis=1)
    y = y_ref[...].reshape(N_EXPERTS * CAPACITY, y_ref.shape[-1])
    x2 = x1_ref[...] + g2_ref[...] * _dot(onehot, y)
    ms = jnp.mean(x2 * x2, axis=-1, keepdims=True)
    out_ref[...] = x2 * lax.rsqrt(ms + EPS) * fg_ref[...]


def _moe_combine(slot_ble, y, x1, g2, final_g):
    n, d = x1.shape
    tm = COMBINE_TOKENS
    tiles_per_sample = SEQ // tm
    vmem = (N_EXPERTS * CAPACITY * d * 2 + 2 * 2 * tm * d * 4 + 2 * tm * 128 * 4
            + tm * N_EXPERTS * CAPACITY * 2 + 3 * tm * d * 4)
    return pl.pallas_call(
        _moe_combine_kernel,
        grid=(BATCH, tiles_per_sample),
        in_specs=[pl.BlockSpec((None, tm, N_EXPERTS), lambda b, t: (b, t, 0)),
                  pl.BlockSpec((N_EXPERTS, None, CAPACITY, d), lambda b, t: (0, b, 0, 0),
                               pipeline_mode=pl.Buffered(1)),
                  pl.BlockSpec((tm, d), lambda b, t: (b * tiles_per_sample + t, 0)),
                  pl.BlockSpec((None, 1, d), lambda b, t: (b, 0, 0)),
                  pl.BlockSpec((1, d), lambda b, t: (0, 0))],
        out_specs=pl.BlockSpec((tm, d), lambda b, t: (b * tiles_per_sample + t, 0)),
        out_shape=jax.ShapeDtypeStruct((n, d), F32),
        compiler_params=_params(vmem, ("arbitrary", "arbitrary")),
        name="moe_combine",
    )(slot_ble, y, x1, g2, final_g)


def kernel(x, c, ctx, c_ctx, ada_w, ada_b, norm1_g, norm2_g, w_in, lb_param, hg_norm_g, pool_w, pool_scale,
           w_out, router_w, moe_w1, moe_w3, moe_w2, final_norm_g):
    nb, seq, d = x.shape
    assert (nb, seq, d) == (BATCH, SEQ, D_MODEL) and ctx.shape[1] == CTX_LEN
    assert ada_w.shape[0] == 1 and lb_param.shape[0] == 2, "single-layer block: layer 0 uses lower-bound row 0"
    x2d = x.reshape(nb * seq, d)
    ctx2d = ctx.reshape(nb * CTX_LEN, d)

    cvecs = jnp.zeros((ADA_ROWS, d), F32).at[:nb].set(c).at[nb].set(c_ctx)
    mod = _ada_mod(cvecs, ada_w[0], ada_b).reshape(ADA_ROWS, 6, d)
    sh1, sc1, g1, sh2, sc2, g2 = (mod[:nb, k][:, None, :] for k in range(6))
    csh1, csc1 = mod[nb:nb + 1, 0][:, None, :], mod[nb:nb + 1, 1][:, None, :]

    lbp = lb_param.reshape(lb_param.shape[0], 2 * HG_WIDTH)

    h_c = _norm1(ctx2d, norm1_g, csc1, csh1, rows_per_sample=nb * CTX_LEN)
    f_c, k_c, i_c = _in_proj(h_c, lbp, w_in[0], n_cols=3 * HG_WIDTH)
    s_f, s_b = _ctx_states(f_c, k_c, i_c)

    h_x = _norm1(x2d, norm1_g, sc1, sh1, rows_per_sample=seq)
    fdec, kk, rest = _in_proj(h_x, lbp, w_in[0], n_cols=w_in.shape[2])
    o = _hgrn2_scan(fdec, kk, rest, s_f, s_b, hg_norm_g)
    pm = _grid_pool(rest, pool_w[0], pool_scale)
    x1 = _out_proj(o, pm, w_out[0], x2d, g1)

    h2, aff_t = _norm2_router(x1, norm2_g, sc2, sh2, router_w[0].T)
    slot, gates, tok = _route(aff_t.reshape(nb * N_EXPERTS, seq))
    slot = slot.reshape(nb, N_EXPERTS, seq)
    h2_rows = tok.reshape(nb, N_EXPERTS, CAPACITY) + (jnp.arange(nb, dtype=I32) * seq)[:, None, None]
    hid = _moe_up(h2_rows.transpose(1, 0, 2).reshape(N_EXPERTS, nb * CAPACITY), h2, moe_w1[0], moe_w3[0])
    gate_col = gates.reshape(nb, N_EXPERTS, CAPACITY).transpose(1, 0, 2).reshape(N_EXPERTS, nb * CAPACITY, 1)
    y = _moe_down(hid, moe_w2[0], gate_col).reshape(N_EXPERTS, nb, CAPACITY, d)
    out = _moe_combine(slot.transpose(0, 2, 1), y, x1, g2, final_norm_g[None, :])
    return out.reshape(nb, seq, d).astype(x.dtype)
```

```python
import jax
import jax.numpy as jnp
import numpy as np
from jax import lax
from jax.experimental import pallas as pl
from jax.experimental.pallas import tpu as pltpu

F32 = jnp.float32
BF16 = jnp.bfloat16
I32 = jnp.int32

D_MODEL = 4096
BATCH = 4
SEQ = 2048
GRID_W = 64
CTX_LEN = 256
HG_HEADS = 16
HG_HEAD_DIM = 128
HG_WIDTH = HG_HEADS * HG_HEAD_DIM
POOL_WIDTH = D_MODEL - HG_WIDTH
POOL_WINDOWS = (2, 4, 8, 16)
POOL_GROUP = POOL_WIDTH // len(POOL_WINDOWS)
N_EXPERTS = 16
EXPERT_FF = D_MODEL // 2
CAPACITY = 2 * SEQ // N_EXPERTS
EPS = 1e-6

V7X_VMEM_BYTES = 64 * 1024 * 1024
V7X_VMEM_HEADROOM_BYTES = 10 * 1024 * 1024
V7X_SUBLANES = 8
V7X_LANES = 128
NORM_ROWS = 16
ADA_ROWS = 8

SCAN_CHUNK = 128
SCAN_LEVELS = 7
SCAN_UNROLL = 8

_NT = (((1,), (1,)), ((), ()))
_TN = (((0,), (0,)), ((), ()))


def _dot(a, b):
    return jnp.dot(a, b, preferred_element_type=F32)


def _dot_nt(a, b):
    return lax.dot_general(a, b, _NT, preferred_element_type=F32)


def _dot_tn(a, b):
    return lax.dot_general(a, b, _TN, preferred_element_type=F32)


def _split_f32(x):
    hi = x.astype(BF16).astype(F32)
    return hi, x - hi


def _params(vmem_bytes, semantics):
    limit = min(int(vmem_bytes) + V7X_VMEM_HEADROOM_BYTES, V7X_VMEM_BYTES)
    return pltpu.CompilerParams(dimension_semantics=semantics, vmem_limit_bytes=limit)


def _ada_kernel(c_ref, w_ref, b_ref, o_ref):
    c = c_ref[...]
    s = c * jax.nn.sigmoid(c)
    hi, lo = _split_f32(s)
    lhs = jnp.concatenate([hi, lo], axis=0).astype(BF16)
    r = _dot(lhs, w_ref[...].astype(BF16))
    o_ref[...] = r[:ADA_ROWS] + r[ADA_ROWS:] + b_ref[...]


def _ada_mod(cvecs, w, b):
    d, n = w.shape
    tn = 512
    vmem = 2 * d * tn * 4 + d * tn * 2 + 4 * ADA_ROWS * d * 4
    return pl.pallas_call(
        _ada_kernel,
        grid=(n // tn,),
        in_specs=[pl.BlockSpec((ADA_ROWS, d), lambda j: (0, 0)),
                  pl.BlockSpec((d, tn), lambda j: (0, j)),
                  pl.BlockSpec((1, tn), lambda j: (0, j))],
        out_specs=pl.BlockSpec((ADA_ROWS, tn), lambda j: (0, j)),
        out_shape=jax.ShapeDtypeStruct((ADA_ROWS, n), F32),
        compiler_params=_params(vmem, ("arbitrary",)),
        name="ada_mod",
    )(cvecs, w, b)


def _modulated_norm(x, g, scale, shift):
    ms = jnp.mean(x * x, axis=-1, keepdims=True)
    return (x * lax.rsqrt(ms + EPS) * g) * (1.0 + scale) + shift


def _norm1_kernel(x_ref, g_ref, sc_ref, sh_ref, h_ref):
    g, sc, sh = g_ref[...], sc_ref[...], sh_ref[...]

    def norm_rows(r, carry):
        rows = pl.ds(pl.multiple_of(r * NORM_ROWS, NORM_ROWS), NORM_ROWS)
        h_ref[rows, :] = _modulated_norm(x_ref[rows, :], g, sc, sh).astype(BF16)
        return carry

    lax.fori_loop(0, x_ref.shape[0] // NORM_ROWS, norm_rows, 0, unroll=2)


def _norm1(x2d, g, sc, sh, *, rows_per_sample):
    n, d = x2d.shape
    tm = 512
    tiles_per_sample = rows_per_sample // tm
    mod = pl.BlockSpec((None, 1, d), lambda i: (i // tiles_per_sample, 0, 0))
    vmem = 2 * tm * d * (4 + 2) + 8 * NORM_ROWS * d * 4
    return pl.pallas_call(
        _norm1_kernel,
        grid=(n // tm,),
        in_specs=[pl.BlockSpec((tm, d), lambda i: (i, 0)), pl.BlockSpec((1, d), lambda i: (0, 0)), mod, mod],
        out_specs=pl.BlockSpec((tm, d), lambda i: (i, 0)),
        out_shape=jax.ShapeDtypeStruct((n, d), BF16),
        compiler_params=_params(vmem, ("arbitrary",)),
        name="norm1",
    )(x2d, g, sc, sh)


def _gate_proj_kernel(h_ref, lbp_ref, w_ref, f_ref, k_ref, w_scr):
    @pl.when(pl.program_id(1) == 0)
    def _():
        w_scr[...] = w_ref[...].astype(BF16)

    z = _dot(h_ref[...], w_scr[...])
    p = lbp_ref[...]
    e = jnp.exp(p - jnp.max(p, axis=0, keepdims=True))
    lb = e[0:1] / jnp.sum(e, axis=0, keepdims=True)
    f = lb + (1.0 - lb) * jax.nn.sigmoid(z)
    f_ref[...] = f
    k_ref[...] = (1.0 - f).astype(BF16)


def _rest_proj_kernel(h_ref, w_ref, r_ref, w_scr):
    @pl.when(pl.program_id(1) == 0)
    def _():
        w_scr[...] = w_ref[...].astype(BF16)

    r_ref[...] = _dot(h_ref[...], w_scr[...]).astype(BF16)


def _in_proj(h, lbp, w, *, n_cols):
    n, d = h.shape
    tm, tn = 1024, 512
    n_gate = 2 * HG_WIDTH // tn
    w_bytes = 2 * d * tn * 4 + d * tn * 2
    vmem = 2 * tm * d * 2 + w_bytes + 2 * tm * tn * (4 + 2) + 4 * tm * tn * 4
    fdec, kk = pl.pallas_call(
        _gate_proj_kernel,
        grid=(n_gate, n // tm),
        in_specs=[pl.BlockSpec((tm, d), lambda j, i: (i, 0)),
                  pl.BlockSpec((2, tn), lambda j, i: (0, j)),
                  pl.BlockSpec((d, tn), lambda j, i: (0, j))],
        out_specs=[pl.BlockSpec((tm, tn), lambda j, i: (i, j)),
                   pl.BlockSpec((tm, tn), lambda j, i: (i, j))],
        out_shape=[jax.ShapeDtypeStruct((n, 2 * HG_WIDTH), F32),
                   jax.ShapeDtypeStruct((n, 2 * HG_WIDTH), BF16)],
        scratch_shapes=[pltpu.VMEM((d, tn), BF16)],
        compiler_params=_params(vmem, ("arbitrary", "arbitrary")),
        name="gate_proj",
    )(h, lbp, w)
    vmem = 2 * tm * d * 2 + w_bytes + 2 * tm * tn * 2 + 2 * tm * tn * 4
    rest = pl.pallas_call(
        _rest_proj_kernel,
        grid=((n_cols - 2 * HG_WIDTH) // tn, n // tm),
        in_specs=[pl.BlockSpec((tm, d), lambda j, i: (i, 0)),
                  pl.BlockSpec((d, tn), lambda j, i: (0, n_gate + j))],
        out_specs=pl.BlockSpec((tm, tn), lambda j, i: (i, j)),
        out_shape=jax.ShapeDtypeStruct((n, n_cols - 2 * HG_WIDTH), BF16),
        scratch_shapes=[pltpu.VMEM((d, tn), BF16)],
        compiler_params=_params(vmem, ("arbitrary", "arbitrary")),
        name="rest_proj",
    )(h, w)
    return fdec, kk, rest


def _ctx_tri():
    s = np.arange(CTX_LEN)[:, None]
    u = np.arange(CTX_LEN)[None, :]
    tri = np.stack([u > s, u < s]).astype(np.float32)
    return jnp.asarray(np.concatenate([tri, tri], axis=2), dtype=BF16)


CTX_HEADS = 4


def _ctx_state_kernel(ff_ref, fb_ref, kf_ref, kb_ref, i_ref, tri_ref, sf_ref, sb_ref):
    dh = HG_HEAD_DIM

    def state(f, k, v, a):
        hi, lo = _split_f32(jnp.log(f))
        g = _dot(a, jnp.concatenate([hi, lo], axis=0).astype(BF16))
        kd = (k.astype(F32) * jnp.exp(g)).astype(BF16)
        return _dot_tn(v, kd)

    for j in range(CTX_HEADS):
        cols = slice(j * dh, (j + 1) * dh)
        v = i_ref[:, cols]
        sf_ref[j] = state(ff_ref[:, cols], kf_ref[:, cols], v, tri_ref[0])
        sb_ref[j] = state(fb_ref[:, cols], kb_ref[:, cols], v, tri_ref[1])


def _ctx_states(fdec, kk, vi):
    t, dh, nh = CTX_LEN, HG_HEAD_DIM, HG_HEADS
    groups = nh // CTX_HEADS
    blk = lambda off: pl.BlockSpec((t, CTX_HEADS * dh), lambda b, h: (b, off + h))
    st = pl.BlockSpec((None, CTX_HEADS, dh, dh), lambda b, h: (b, h, 0, 0))
    shape = jax.ShapeDtypeStruct((BATCH, nh, dh, dh), F32)
    vmem = CTX_HEADS * 2 * (2 * t * dh * 4 + 3 * t * dh * 2 + 2 * dh * dh * 4) + 2 * 2 * t * 2 * t * 2
    return pl.pallas_call(
        _ctx_state_kernel,
        grid=(BATCH, groups),
        in_specs=[blk(0), blk(groups), blk(0), blk(groups), blk(0),
                  pl.BlockSpec((2, t, 2 * t), lambda b, h: (0, 0, 0))],
        out_specs=[st, st],
        out_shape=[shape, shape],
        compiler_params=_params(vmem, ("arbitrary", "arbitrary")),
        name="ctx_state",
    )(fdec, fdec, kk, kk, vi, _ctx_tri())


def _pair_levels():
    c, nl = SCAN_CHUNK, SCAN_LEVELS
    t = np.arange(c)[:, None]
    u = np.arange(c)[None, :]
    lev = np.floor(np.log2(np.maximum(t ^ u, 1))).astype(np.int32)
    lidx = np.stack([np.where(t > u, lev, np.where(t == u, nl, -1)),
                     np.where(t < u, lev, np.where(t == u, nl, -1))]).astype(np.int32)
    return jnp.asarray(lidx)


def _swap_halves(x, h, t_idx):
    c = x.shape[0]
    assert 2 * h <= V7X_SUBLANES
    x3 = x.reshape(c // V7X_SUBLANES, V7X_SUBLANES, x.shape[1])
    if 2 * h == V7X_SUBLANES:
        return pltpu.roll(x3, h, axis=1).reshape(x.shape)
    up = pltpu.roll(x3, h, axis=1).reshape(x.shape)
    down = pltpu.roll(x3, V7X_SUBLANES - h, axis=1).reshape(x.shape)
    return jnp.where((t_idx & h) != 0, up, down)


def _scan_chunk(direction, f, k, q, v, st_ref, lidx, t_idx):
    c, nl = SCAN_CHUNK, SCAN_LEVELS
    qf = q.astype(F32)
    kf = k.astype(F32)
    q_dec = f
    k_dec = jnp.ones_like(f)
    total = f
    scores = jnp.zeros((c, c), F32)
    sublane = lax.broadcasted_iota(I32, (1, V7X_SUBLANES, f.shape[1]), 1)
    tiles = lambda a: a.reshape(c // V7X_SUBLANES, V7X_SUBLANES, a.shape[1])
    blocks = lambda a, h: [a[i * h:(i + 1) * h] for i in range(c // h)]
    for l in range(nl):
        h = 1 << l
        if h < V7X_SUBLANES:
            bit = (sublane & h) != 0
            is_query = bit if direction == 0 else jnp.logical_not(bit)
            x = jnp.where(is_query, tiles(q_dec * qf), tiles(k_dec * kf)).reshape(c, -1).astype(BF16)
            scores = jnp.where(lidx == l, _dot_nt(x, x), scores)
            other = _swap_halves(total, h, t_idx)
            q_dec = (tiles(q_dec) * jnp.where(is_query, tiles(other), 1.0)).reshape(c, -1)
            k_dec = (tiles(k_dec) * jnp.where(is_query, 1.0, tiles(other))).reshape(c, -1)
            total = total * other
        else:
            n_blocks = c // h
            is_query = [((i & 1) == 1) == (direction == 0) for i in range(n_blocks)]
            qd, kd, qb, kb, tb = (blocks(a, h) for a in (q_dec, k_dec, qf, kf, total))
            x = jnp.concatenate([qd[i] * qb[i] if is_query[i] else kd[i] * kb[i] for i in range(n_blocks)],
                                axis=0).astype(BF16)
            pb, sb, lb = (blocks(a, h) for a in (_dot_nt(x, x), scores, lidx))
            scores = jnp.concatenate([jnp.where(lb[i] == l, pb[i], sb[i]) if is_query[i] else sb[i]
                                      for i in range(n_blocks)], axis=0)
            ob = [tb[i ^ 1] for i in range(n_blocks)]
            q_dec = jnp.concatenate([qd[i] * ob[i] if is_query[i] else qd[i] for i in range(n_blocks)], axis=0)
            k_dec = jnp.concatenate([kd[i] if is_query[i] else kd[i] * ob[i] for i in range(n_blocks)], axis=0)
            total = jnp.concatenate([tb[i] * ob[i] for i in range(n_blocks)], axis=0)
    own = jnp.sum(qf * kf, axis=-1, keepdims=True)
    q_in = (q_dec * qf).astype(BF16)
    k_st = (k_dec * kf).astype(BF16)
    st = st_ref[...]
    o = _dot(scores.astype(BF16), v) + _dot_nt(q_in, st.astype(BF16)) + own * v.astype(F32)
    st_ref[...] = total[0:1] * st + _dot_tn(v, k_st)
    return o


def _scan_kernel(ff_ref, fb_ref, kf_ref, kb_ref, i_ref, q_ref, g_ref, s0f_ref, s0b_ref, ng_ref,
                 lidx_ref, o_ref, stf_ref, stb_ref, part_ref):
    c = SCAN_CHUNK
    n_chunks = SEQ // c
    stf_ref[...] = s0f_ref[...]
    stb_ref[...] = s0b_ref[...]
    t_idx = lax.broadcasted_iota(I32, (c, HG_HEAD_DIM), 0)
    ng = ng_ref[...]

    def chunk(direction, r0):
        rows = pl.ds(r0, c)
        f_ref, k_ref, st_ref = (ff_ref, kf_ref, stf_ref) if direction == 0 else (fb_ref, kb_ref, stb_ref)
        return _scan_chunk(direction, f_ref[rows, :], k_ref[rows, :], q_ref[rows, :], i_ref[rows, :],
                           st_ref, lidx_ref[direction], t_idx)

    def finalize(o, r0):
        rows = pl.ds(r0, c)
        gate = g_ref[rows, :].astype(F32)
        ms = jnp.mean(o * o, axis=-1, keepdims=True)
        o_ref[rows, :] = (o * lax.rsqrt(ms + EPS) * ng * (gate * jax.nn.sigmoid(gate))).astype(BF16)

    def starts(n):
        return pl.multiple_of(n * c, c), pl.multiple_of((n_chunks - 1 - n) * c, c)

    def first_half(n, carry):
        rf, rb = starts(n)
        part_ref[pl.ds(rf, c), :] = chunk(0, rf)
        part_ref[pl.ds(rb, c), :] = chunk(1, rb)
        return carry

    def second_half(n, carry):
        rf, rb = starts(n)
        finalize(chunk(0, rf) + part_ref[pl.ds(rf, c), :], rf)
        finalize(chunk(1, rb) + part_ref[pl.ds(rb, c), :], rb)
        return carry

    lax.fori_loop(0, n_chunks // 2, first_half, 0, unroll=SCAN_UNROLL)
    lax.fori_loop(n_chunks // 2, n_chunks, second_half, 0, unroll=SCAN_UNROLL)


def _hgrn2_scan(fdec, kk, rest, s0f, s0b, norm_g):
    dh, nh = HG_HEAD_DIM, HG_HEADS
    lidx = _pair_levels()
    blk = lambda off: pl.BlockSpec((SEQ, dh), lambda b, h: (b, off + h))
    st = pl.BlockSpec((None, None, dh, dh), lambda b, h: (b, h, 0, 0))
    vmem = (2 * (2 * SEQ * dh * 4 + 6 * SEQ * dh * 2 + 2 * dh * dh * 4) + SEQ * dh * 4 + 2 * dh * dh * 4
            + 2 * 2 * SCAN_CHUNK * V7X_LANES * 4)
    return pl.pallas_call(
        _scan_kernel,
        grid=(BATCH, nh),
        in_specs=[blk(0), blk(nh), blk(0), blk(nh), blk(0), blk(nh), blk(2 * nh), st, st,
                  pl.BlockSpec((1, dh), lambda b, h: (0, h)),
                  pl.BlockSpec(lidx.shape, lambda b, h: (0, 0, 0))],
        out_specs=blk(0),
        out_shape=jax.ShapeDtypeStruct((BATCH * SEQ, HG_WIDTH), BF16),
        scratch_shapes=[pltpu.VMEM((dh, dh), F32), pltpu.VMEM((dh, dh), F32), pltpu.VMEM((SEQ, dh), F32)],
        compiler_params=_params(vmem, ("arbitrary", "arbitrary")),
        name="hgrn2_scan",
    )(fdec, fdec, kk, kk, rest, rest, rest, s0f, s0b, norm_g, lidx)


POOL_PIECE = 256


def _box_bounds(n, w):
    start = np.arange(n) - w // 2
    return np.clip(start, 0, n), np.clip(start + w, 0, n)


def _pool_consts():
    rows = SEQ // GRID_W
    col_box = np.zeros((len(POOL_WINDOWS), POOL_PIECE, POOL_PIECE), np.float32)
    cnt = np.zeros((len(POOL_WINDOWS), SEQ, 1), np.float32)
    cc = np.arange(GRID_W)[None, :]
    for gi, w in enumerate(POOL_WINDOWS):
        c0, c1 = _box_bounds(GRID_W, w)
        r0, r1 = _box_bounds(rows, w)
        wc = ((cc >= c0[:, None]) & (cc < c1[:, None])).astype(np.float32)
        col_box[gi] = np.kron(np.eye(POOL_PIECE // GRID_W, dtype=np.float32), wc)
        cnt[gi] = ((r1 - r0)[:, None] * (c1 - c0)[None, :]).reshape(SEQ, 1)
    return jnp.asarray(col_box, dtype=BF16), jnp.asarray(cnt)


def _pool_kernel(v_ref, box_ref, cnt_ref, pw_ref, ps_ref, o_ref, pre_ref, diff_ref):
    gi = pl.program_id(1)
    rows = SEQ // GRID_W
    gw = GRID_W
    box = box_ref[...]
    pre_ref[0:gw, :] = jnp.zeros((gw, POOL_GROUP), F32)
    for p in range(SEQ // POOL_PIECE):
        yc = _dot(box, v_ref[p * POOL_PIECE:(p + 1) * POOL_PIECE, :])
        for rr in range(POOL_PIECE // gw):
            r = p * (POOL_PIECE // gw) + rr
            pre_ref[(r + 1) * gw:(r + 2) * gw, :] = pre_ref[r * gw:(r + 1) * gw, :] + yc[rr * gw:(rr + 1) * gw]
    for k, w in enumerate(POOL_WINDOWS):
        @pl.when(gi == k)
        def _(w=w):
            r0, r1 = _box_bounds(rows, w)
            for r in range(rows):
                sl = slice(r * gw, (r + 1) * gw)
                box_sum = pre_ref[int(r1[r]) * gw:(int(r1[r]) + 1) * gw, :] - pre_ref[int(r0[r]) * gw:(int(r0[r]) + 1) * gw, :]
                diff_ref[sl, :] = (box_sum / cnt_ref[sl, :] - v_ref[sl, :].astype(F32)).astype(BF16)
    o_ref[...] = (_dot(diff_ref[...], pw_ref[...].astype(BF16)) * ps_ref[...]).astype(BF16)


def _grid_pool(rest, pool_w, pool_scale):
    ng, pg = len(POOL_WINDOWS), POOL_GROUP
    col_box, cnt = _pool_consts()
    v_off = 3 * HG_WIDTH // pg
    vmem = (2 * (2 * SEQ * pg * 2 + POOL_PIECE * POOL_PIECE * 2 + SEQ * 128 * 4 + pg * pg * 4)
            + (SEQ + GRID_W) * pg * 4 + SEQ * pg * 2 + SEQ * pg * 4)
    return pl.pallas_call(
        _pool_kernel,
        grid=(BATCH, ng),
        in_specs=[pl.BlockSpec((SEQ, pg), lambda b, k: (b, v_off + k)),
                  pl.BlockSpec((None, POOL_PIECE, POOL_PIECE), lambda b, k: (k, 0, 0)),
                  pl.BlockSpec((None, SEQ, 1), lambda b, k: (k, 0, 0)),
                  pl.BlockSpec((None, pg, pg), lambda b, k: (k, 0, 0)),
                  pl.BlockSpec((1, pg), lambda b, k: (0, k))],
        out_specs=pl.BlockSpec((SEQ, pg), lambda b, k: (b, k)),
        out_shape=jax.ShapeDtypeStruct((BATCH * SEQ, POOL_WIDTH), BF16),
        scratch_shapes=[pltpu.VMEM((SEQ + GRID_W, pg), F32), pltpu.VMEM((SEQ, pg), BF16)],
        compiler_params=_params(vmem, ("arbitrary", "arbitrary")),
        name="grid_pool",
    )(rest, col_box, cnt, pool_w, pool_scale)


def _out_proj_kernel(o_ref, pm_ref, wa_ref, wb_ref, x_ref, g1_ref, out_ref, wa_scr, wb_scr):
    @pl.when(pl.program_id(1) == 0)
    def _():
        wa_scr[...] = wa_ref[...].astype(BF16)
        wb_scr[...] = wb_ref[...].astype(BF16)

    acc = _dot(o_ref[...], wa_scr[...]) + _dot(pm_ref[...], wb_scr[...])
    out_ref[...] = x_ref[...] + g1_ref[...] * acc


def _out_proj(o, pm, w, x2d, g1):
    n, d = x2d.shape
    tm, tn = 1024, 512
    half = w.shape[0] // 2
    tiles_per_sample = SEQ // tm
    vmem = 2 * (2 * tm * half * 2 + 2 * half * tn * 4 + 2 * tm * tn * 4) + 2 * half * tn * 2 + tm * tn * 4
    return pl.pallas_call(
        _out_proj_kernel,
        grid=(d // tn, n // tm),
        in_specs=[pl.BlockSpec((tm, half), lambda j, i: (i, 0)),
                  pl.BlockSpec((tm, half), lambda j, i: (i, 0)),
                  pl.BlockSpec((half, tn), lambda j, i: (0, j)),
                  pl.BlockSpec((half, tn), lambda j, i: (1, j)),
                  pl.BlockSpec((tm, tn), lambda j, i: (i, j)),
                  pl.BlockSpec((None, 1, tn), lambda j, i: (i // tiles_per_sample, 0, j))],
        out_specs=pl.BlockSpec((tm, tn), lambda j, i: (i, j)),
        out_shape=jax.ShapeDtypeStruct((n, d), F32),
        scratch_shapes=[pltpu.VMEM((half, tn), BF16), pltpu.VMEM((half, tn), BF16)],
        compiler_params=_params(vmem, ("arbitrary", "arbitrary")),
        name="out_proj",
    )(o, pm, w, w, x2d, g1)


def _norm2_router_kernel(x_ref, g_ref, sc_ref, sh_ref, rw_ref, h2_ref, aff_ref):
    h = _modulated_norm(x_ref[...], g_ref[...], sc_ref[...], sh_ref[...])
    h2_ref[...] = h
    h_hi, h_lo = _split_f32(h)
    r_hi, r_lo = _split_f32(rw_ref[...])
    h_hi, h_lo, r_hi, r_lo = (a.astype(BF16) for a in (h_hi, h_lo, r_hi, r_lo))
    logits = _dot_nt(r_hi, h_hi) + (_dot_nt(r_hi, h_lo) + _dot_nt(r_lo, h_hi))
    e = jnp.exp(logits - jnp.max(logits, axis=0, keepdims=True))
    aff_ref[...] = e / jnp.sum(e, axis=0, keepdims=True)


def _norm2_router(x1, g, sc, sh, router_wt):
    n, d = x1.shape
    tm = 256
    tiles_per_sample = SEQ // tm
    vmem = 2 * (2 * tm * d * 4 + N_EXPERTS * d * 4) + 6 * tm * d * 4
    return pl.pallas_call(
        _norm2_router_kernel,
        grid=(n // tm,),
        in_specs=[pl.BlockSpec((tm, d), lambda i: (i, 0)),
                  pl.BlockSpec((1, d), lambda i: (0, 0)),
                  pl.BlockSpec((None, 1, d), lambda i: (i // tiles_per_sample, 0, 0)),
                  pl.BlockSpec((None, 1, d), lambda i: (i // tiles_per_sample, 0, 0)),
                  pl.BlockSpec((N_EXPERTS, d), lambda i: (0, 0))],
        out_specs=[pl.BlockSpec((tm, d), lambda i: (i, 0)),
                   pl.BlockSpec((None, N_EXPERTS, tm), lambda i: (i // tiles_per_sample, 0, i % tiles_per_sample))],
        out_shape=[jax.ShapeDtypeStruct((n, d), F32),
                   jax.ShapeDtypeStruct((BATCH, N_EXPERTS, SEQ), F32)],
        compiler_params=_params(vmem, ("arbitrary",)),
        name="norm2_router",
    )(x1, g, sc, sh, router_wt)


ROUTE_COLS = 256
ROUTE_ROWS = 8
ROUTE_BITS = 3
TOKEN_DIGIT_BITS = 6
TOKEN_DIGIT = 1 << TOKEN_DIGIT_BITS


def _route_kernel(aff_ref, slot_ref, gate_ref, idx_ref):
    a = aff_ref[...]
    rows, n = a.shape
    thr = jnp.zeros((rows, 1), I32)
    for top in range(30, -1, -ROUTE_BITS):
        low = max(top - ROUTE_BITS + 1, 0)
        best = thr
        for pattern in range(1, 1 << (top - low + 1)):
            cand = thr | (pattern << low)
            cnt = jnp.sum((a >= pltpu.bitcast(cand, F32)).astype(I32), axis=-1, keepdims=True)
            best = jnp.where(cnt >= CAPACITY, cand, best)
        thr = best
    gt = a >= pltpu.bitcast(thr + 1, F32)
    eq = jnp.logical_and(a >= pltpu.bitcast(thr, F32), jnp.logical_not(gt))
    need = CAPACITY - jnp.sum(gt.astype(I32), axis=-1, keepdims=True)

    def prefix_count(mask):
        m = jnp.where(mask, 1.0, 0.0).astype(BF16)
        r = lax.broadcasted_iota(I32, (n, ROUTE_COLS), 0)
        c = lax.broadcasted_iota(I32, (n, ROUTE_COLS), 1)
        parts = [_dot(m, jnp.where(r < c + cb * ROUTE_COLS, 1.0, 0.0).astype(BF16))
                 for cb in range(n // ROUTE_COLS)]
        return jnp.concatenate(parts, axis=1).astype(I32)

    sel = jnp.logical_or(gt, jnp.logical_and(eq, prefix_count(eq) < need))
    slot = jnp.where(sel, prefix_count(sel), -1)
    slot_ref[...] = slot

    a_hi, a_rest = _split_f32(a)
    a_mid, a_lo = _split_f32(a_rest)
    tok = lax.broadcasted_iota(I32, (1, n), 1)
    tok_hi = (tok >> TOKEN_DIGIT_BITS).astype(F32)
    tok_lo = (tok & (TOKEN_DIGIT - 1)).astype(F32)
    slot_ids = lax.broadcasted_iota(I32, (CAPACITY, n), 0)
    pad = jnp.zeros((ROUTE_ROWS - 5, n), F32)
    for r in range(rows):
        onehot = jnp.where(slot[r:r + 1] == slot_ids, 1.0, 0.0).astype(BF16)
        pieces = jnp.concatenate([a_hi[r:r + 1], a_mid[r:r + 1], a_lo[r:r + 1], tok_hi, tok_lo, pad],
                                 axis=0).astype(BF16)
        res = _dot_nt(pieces, onehot)
        gate_ref[r:r + 1, :] = res[0:1] + res[1:2] + res[2:3]
        idx_ref[r:r + 1, :] = (res[3:4] * TOKEN_DIGIT + res[4:5]).astype(I32)


def _route(aff_rows):
    rows, n = aff_rows.shape
    vmem = 4 * ROUTE_ROWS * n * 4 + 4 * n * ROUTE_COLS * 4 + 4 * CAPACITY * n * 4
    return pl.pallas_call(
        _route_kernel,
        grid=(rows // ROUTE_ROWS,),
        in_specs=[pl.BlockSpec((ROUTE_ROWS, n), lambda i: (i, 0))],
        out_specs=[pl.BlockSpec((ROUTE_ROWS, n), lambda i: (i, 0)),
                   pl.BlockSpec((ROUTE_ROWS, CAPACITY), lambda i: (i, 0)),
                   pl.BlockSpec((ROUTE_ROWS, CAPACITY), lambda i: (i, 0))],
        out_shape=[jax.ShapeDtypeStruct((rows, n), I32),
                   jax.ShapeDtypeStruct((rows, CAPACITY), F32),
                   jax.ShapeDtypeStruct((rows, CAPACITY), I32)],
        compiler_params=_params(vmem, ("arbitrary",)),
        name="route",
    )(aff_rows)


CAST_ROWS = 64


def _moe_up_kernel(idx_ref, h2_hbm, w1_ref, w3_ref, hid_ref, rows32, rows16, sem):
    e, f = pl.program_id(0), pl.program_id(1)
    n_experts = pl.num_programs(0)
    n_ff, share, _ = rows32.shape
    m = n_ff * share

    def row_copy(src_row, group, r):
        return pltpu.make_async_copy(h2_hbm.at[pl.ds(src_row, 1), :], rows32.at[group, pl.ds(r, 1), :], sem)

    def start_group(expert, group):
        first = expert * m + group * share
        for r in range(share):
            row_copy(idx_ref[first + r], group, r).start()

    @pl.when(jnp.logical_and(e == 0, f == 0))
    def _():
        lax.fori_loop(0, n_ff, lambda grp, carry: (start_group(0, grp), carry)[1], 0)

    @pl.when(f == 0)
    def _():
        def wait_group(grp, carry):
            for r in range(share):
                row_copy(0, grp, r).wait()
            return carry
        lax.fori_loop(0, n_ff, wait_group, 0)

        def cast_group(grp, carry):
            for c0 in range(0, share, CAST_ROWS):
                dst = pl.ds(pl.multiple_of(grp * share + c0, CAST_ROWS), CAST_ROWS)
                rows16[dst, :] = rows32[grp, c0:c0 + CAST_ROWS, :].astype(BF16)
            return carry
        lax.fori_loop(0, n_ff, cast_group, 0)

    @pl.when(e + 1 < n_experts)
    def _():
        start_group(e + 1, f)

    xg = rows16[...]
    a = _dot(xg, w1_ref[...].astype(BF16))
    b = _dot(xg, w3_ref[...].astype(BF16))
    hid_ref[...] = (a * jax.nn.sigmoid(a) * b).astype(BF16)


def _moe_up(idx_table, h2, w1, w3):
    ne, m = idx_table.shape
    d = h2.shape[1]
    ff = w1.shape[2]
    tf = 256
    vmem = m * d * (4 + 2) + 2 * 2 * d * tf * 4 + 2 * d * tf * 2 + 3 * m * tf * 4 + 2 * m * tf * 2
    n_ff = ff // tf
    w_spec = pl.BlockSpec((None, d, tf), lambda e, f, idx: (e, 0, f))
    return pl.pallas_call(
        _moe_up_kernel,
        grid_spec=pltpu.PrefetchScalarGridSpec(
            num_scalar_prefetch=1,
            grid=(ne, n_ff),
            in_specs=[pl.BlockSpec(memory_space=pl.ANY), w_spec, w_spec],
            out_specs=pl.BlockSpec((None, m, tf), lambda e, f, idx: (e, 0, f)),
            scratch_shapes=[pltpu.VMEM((n_ff, m // n_ff, d), F32), pltpu.VMEM((m, d), BF16),
                            pltpu.SemaphoreType.DMA(())]),
        out_shape=jax.ShapeDtypeStruct((ne, m, ff), BF16),
        compiler_params=_params(vmem, ("arbitrary", "arbitrary")),
        name="moe_up",
    )(idx_table.reshape(ne * m), h2, w1, w3)


def _moe_down_kernel(hid_ref, w2_ref, gate_ref, y_ref):
    y = _dot(hid_ref[...], w2_ref[...].astype(BF16))
    y_ref[...] = (y * gate_ref[...]).astype(BF16)


def _moe_down(hid, w2, gate_col):
    ne, m, ff = hid.shape
    d = w2.shape[2]
    tn = 1024
    vmem = 2 * m * ff * 2 + 2 * ff * tn * 4 + ff * tn * 2 + 2 * m * 128 * 4 + 2 * m * tn * 4 + 2 * m * tn * 2
    return pl.pallas_call(
        _moe_down_kernel,
        grid=(ne, d // tn),
        in_specs=[pl.BlockSpec((None, m, ff), lambda e, j: (e, 0, 0)),
                  pl.BlockSpec((None, ff, tn), lambda e, j: (e, 0, j)),
                  pl.BlockSpec((None, m, 1), lambda e, j: (e, 0, 0))],
        out_specs=pl.BlockSpec((None, m, tn), lambda e, j: (e, 0, j)),
        out_shape=jax.ShapeDtypeStruct((ne, m, d), BF16),
        compiler_params=_params(vmem, ("arbitrary", "arbitrary")),
        name="moe_down",
    )(hid, w2, gate_col)


COMBINE_TOKENS = 128


def _moe_combine_kernel(slot_ref, y_ref, x1_ref, g2_ref, fg_ref, out_ref):
    st = slot_ref[...]
    slot_ids = lax.broadcasted_iota(I32, (COMBINE_TOKENS, CAPACITY), 1)
    onehot = jnp.concatenate(
        [jnp.where(st[:, e:e + 1] == slot_ids, 1.0, 0.0).astype(BF16) for e in range(N_EXPERTS)], axis=1)
    y = y_ref[...].reshape(N_EXPERTS * CAPACITY, y_ref.shape[-1])
    x2 = x1_ref[...] + g2_ref[...] * _dot(onehot, y)
    ms = jnp.mean(x2 * x2, axis=-1, keepdims=True)
    out_ref[...] = x2 * lax.rsqrt(ms + EPS) * fg_ref[...]


def _moe_combine(slot_ble, y, x1, g2, final_g):
    n, d = x1.shape
    tm = COMBINE_TOKENS
    tiles_per_sample = SEQ // tm
    vmem = (N_EXPERTS * CAPACITY * d * 2 + 2 * 2 * tm * d * 4 + 2 * tm * 128 * 4
            + tm * N_EXPERTS * CAPACITY * 2 + 3 * tm * d * 4)
    return pl.pallas_call(
        _moe_combine_kernel,
        grid=(BATCH, tiles_per_sample),
        in_specs=[pl.BlockSpec((None, tm, N_EXPERTS), lambda b, t: (b, t, 0)),
                  pl.BlockSpec((N_EXPERTS, None, CAPACITY, d), lambda b, t: (0, b, 0, 0),
                               pipeline_mode=pl.Buffered(1)),
                  pl.BlockSpec((tm, d), lambda b, t: (b * tiles_per_sample + t, 0)),
                  pl.BlockSpec((None, 1, d), lambda b, t: (b, 0, 0)),
                  pl.BlockSpec((1, d), lambda b, t: (0, 0))],
        out_specs=pl.BlockSpec((tm, d), lambda b, t: (b * tiles_per_sample + t, 0)),
        out_shape=jax.ShapeDtypeStruct((n, d), F32),
        compiler_params=_params(vmem, ("arbitrary", "arbitrary")),
        name="moe_combine",
    )(slot_ble, y, x1, g2, final_g)


def kernel(x, c, ctx, c_ctx, ada_w, ada_b, norm1_g, norm2_g, w_in, lb_param, hg_norm_g, pool_w, pool_scale,
           w_out, router_w, moe_w1, moe_w3, moe_w2, final_norm_g):
    nb, seq, d = x.shape
    assert (nb, seq, d) == (BATCH, SEQ, D_MODEL) and ctx.shape[1] == CTX_LEN
    assert ada_w.shape[0] == 1 and lb_param.shape[0] == 2, "single-layer block: layer 0 uses lower-bound row 0"
    x2d = x.reshape(nb * seq, d)
    ctx2d = ctx.reshape(nb * CTX_LEN, d)

    cvecs = jnp.zeros((ADA_ROWS, d), F32).at[:nb].set(c).at[nb].set(c_ctx)
    mod = _ada_mod(cvecs, ada_w[0], ada_b).reshape(ADA_ROWS, 6, d)
    sh1, sc1, g1, sh2, sc2, g2 = (mod[:nb, k][:, None, :] for k in range(6))
    csh1, csc1 = mod[nb:nb + 1, 0][:, None, :], mod[nb:nb + 1, 1][:, None, :]

    lbp = lb_param.reshape(lb_param.shape[0], 2 * HG_WIDTH)

    h_c = _norm1(ctx2d, norm1_g, csc1, csh1, rows_per_sample=nb * CTX_LEN)
    f_c, k_c, i_c = _in_proj(h_c, lbp, w_in[0], n_cols=3 * HG_WIDTH)
    s_f, s_b = _ctx_states(f_c, k_c, i_c)

    h_x = _norm1(x2d, norm1_g, sc1, sh1, rows_per_sample=seq)
    fdec, kk, rest = _in_proj(h_x, lbp, w_in[0], n_cols=w_in.shape[2])
    o = _hgrn2_scan(fdec, kk, rest, s_f, s_b, hg_norm_g)
    pm = _grid_pool(rest, pool_w[0], pool_scale)
    x1 = _out_proj(o, pm, w_out[0], x2d, g1)

    h2, aff_t = _norm2_router(x1, norm2_g, sc2, sh2, router_w[0].T)
    slot, gates, tok = _route(aff_t.reshape(nb * N_EXPERTS, seq))
    slot = slot.reshape(nb, N_EXPERTS, seq)
    h2_rows = tok.reshape(nb, N_EXPERTS, CAPACITY) + (jnp.arange(nb, dtype=I32) * seq)[:, None, None]
    hid = _moe_up(h2_rows.transpose(1, 0, 2).reshape(N_EXPERTS, nb * CAPACITY), h2, moe_w1[0], moe_w3[0])
    gate_col = gates.reshape(nb, N_EXPERTS, CAPACITY).transpose(1, 0, 2).reshape(N_EXPERTS, nb * CAPACITY, 1)
    y = _moe_down(hid, moe_w2[0], gate_col).reshape(N_EXPERTS, nb, CAPACITY, d)
    out = _moe_combine(slot.transpose(0, 2, 1), y, x1, g2, final_norm_g[None, :])
    return out.reshape(nb, seq, d).astype(x.dtype)
```

```python
import jax
import jax.numpy as jnp
import numpy as np
from jax import lax
from jax.experimental import pallas as pl
from jax.experimental.pallas import tpu as pltpu

F32 = jnp.float32
BF16 = jnp.bfloat16
I32 = jnp.int32

D_MODEL = 4096
BATCH = 4
SEQ = 2048
GRID_W = 64
CTX_LEN = 256
HG_HEADS = 16
HG_HEAD_DIM = 128
HG_WIDTH = HG_HEADS * HG_HEAD_DIM
POOL_WIDTH = D_MODEL - HG_WIDTH
POOL_WINDOWS = (2, 4, 8, 16)
POOL_GROUP = POOL_WIDTH // len(POOL_WINDOWS)
N_EXPERTS = 16
EXPERT_FF = D_MODEL // 2
CAPACITY = 2 * SEQ // N_EXPERTS
EPS = 1e-6

V7X_VMEM_BYTES = 64 * 1024 * 1024
V7X_VMEM_HEADROOM_BYTES = 10 * 1024 * 1024
V7X_SUBLANES = 8
V7X_LANES = 128
NORM_ROWS = 16
ADA_ROWS = 8

SCAN_CHUNK = 128
SCAN_LEVELS = 7
SCAN_UNROLL = 8

_NT = (((1,), (1,)), ((), ()))
_TN = (((0,), (0,)), ((), ()))


def _dot(a, b):
    return jnp.dot(a, b, preferred_element_type=F32)


def _dot_nt(a, b):
    return lax.dot_general(a, b, _NT, preferred_element_type=F32)


def _dot_tn(a, b):
    return lax.dot_general(a, b, _TN, preferred_element_type=F32)


def _split_f32(x):
    hi = x.astype(BF16).astype(F32)
    return hi, x - hi


def _params(vmem_bytes, semantics):
    limit = min(int(vmem_bytes) + V7X_VMEM_HEADROOM_BYTES, V7X_VMEM_BYTES)
    return pltpu.CompilerParams(dimension_semantics=semantics, vmem_limit_bytes=limit)


def _ada_kernel(c_ref, w_ref, b_ref, o_ref):
    c = c_ref[...]
    s = c * jax.nn.sigmoid(c)
    hi, lo = _split_f32(s)
    lhs = jnp.concatenate([hi, lo], axis=0).astype(BF16)
    r = _dot(lhs, w_ref[...].astype(BF16))
    o_ref[...] = r[:ADA_ROWS] + r[ADA_ROWS:] + b_ref[...]


def _ada_mod(cvecs, w, b):
    d, n = w.shape
    tn = 512
    vmem = 2 * d * tn * 4 + d * tn * 2 + 4 * ADA_ROWS * d * 4
    return pl.pallas_call(
        _ada_kernel,
        grid=(n // tn,),
        in_specs=[pl.BlockSpec((ADA_ROWS, d), lambda j: (0, 0)),
                  pl.BlockSpec((d, tn), lambda j: (0, j)),
                  pl.BlockSpec((1, tn), lambda j: (0, j))],
        out_specs=pl.BlockSpec((ADA_ROWS, tn), lambda j: (0, j)),
        out_shape=jax.ShapeDtypeStruct((ADA_ROWS, n), F32),
        compiler_params=_params(vmem, ("arbitrary",)),
        name="ada_mod",
    )(cvecs, w, b)


def _modulated_norm(x, g, scale, shift):
    ms = jnp.mean(x * x, axis=-1, keepdims=True)
    return (x * lax.rsqrt(ms + EPS) * g) * (1.0 + scale) + shift


def _norm1_kernel(x_ref, g_ref, sc_ref, sh_ref, h_ref):
    g, sc, sh = g_ref[...], sc_ref[...], sh_ref[...]

    def norm_rows(r, carry):
        rows = pl.ds(pl.multiple_of(r * NORM_ROWS, NORM_ROWS), NORM_ROWS)
        h_ref[rows, :] = _modulated_norm(x_ref[rows, :], g, sc, sh).astype(BF16)
        return carry

    lax.fori_loop(0, x_ref.shape[0] // NORM_ROWS, norm_rows, 0, unroll=2)


def _norm1(x2d, g, sc, sh, *, rows_per_sample):
    n, d = x2d.shape
    tm = 512
    tiles_per_sample = rows_per_sample // tm
    mod = pl.BlockSpec((None, 1, d), lambda i: (i // tiles_per_sample, 0, 0))
    vmem = 2 * tm * d * (4 + 2) + 8 * NORM_ROWS * d * 4
    return pl.pallas_call(
        _norm1_kernel,
        grid=(n // tm,),
        in_specs=[pl.BlockSpec((tm, d), lambda i: (i, 0)), pl.BlockSpec((1, d), lambda i: (0, 0)), mod, mod],
        out_specs=pl.BlockSpec((tm, d), lambda i: (i, 0)),
        out_shape=jax.ShapeDtypeStruct((n, d), BF16),
        compiler_params=_params(vmem, ("arbitrary",)),
        name="norm1",
    )(x2d, g, sc, sh)


def _gate_proj_kernel(h_ref, lbp_ref, w_ref, f_ref, k_ref, w_scr):
    @pl.when(pl.program_id(1) == 0)
    def _():
        w_scr[...] = w_ref[...].astype(BF16)

    z = _dot(h_ref[...], w_scr[...])
    p = lbp_ref[...]
    e = jnp.exp(p - jnp.max(p, axis=0, keepdims=True))
    lb = e[0:1] / jnp.sum(e, axis=0, keepdims=True)
    f = lb + (1.0 - lb) * jax.nn.sigmoid(z)
    f_ref[...] = f
    k_ref[...] = (1.0 - f).astype(BF16)


def _rest_proj_kernel(h_ref, w_ref, r_ref, w_scr):
    @pl.when(pl.program_id(1) == 0)
    def _():
        w_scr[...] = w_ref[...].astype(BF16)

    r_ref[...] = _dot(h_ref[...], w_scr[...]).astype(BF16)


def _in_proj(h, lbp, w, *, n_cols):
    n, d = h.shape
    tm, tn = 1024, 512
    n_gate = 2 * HG_WIDTH // tn
    w_bytes = 2 * d * tn * 4 + d * tn * 2
    vmem = 2 * tm * d * 2 + w_bytes + 2 * tm * tn * (4 + 2) + 4 * tm * tn * 4
    fdec, kk = pl.pallas_call(
        _gate_proj_kernel,
        grid=(n_gate, n // tm),
        in_specs=[pl.BlockSpec((tm, d), lambda j, i: (i, 0)),
                  pl.BlockSpec((2, tn), lambda j, i: (0, j)),
                  pl.BlockSpec((d, tn), lambda j, i: (0, j))],
        out_specs=[pl.BlockSpec((tm, tn), lambda j, i: (i, j)),
                   pl.BlockSpec((tm, tn), lambda j, i: (i, j))],
        out_shape=[jax.ShapeDtypeStruct((n, 2 * HG_WIDTH), F32),
                   jax.ShapeDtypeStruct((n, 2 * HG_WIDTH), BF16)],
        scratch_shapes=[pltpu.VMEM((d, tn), BF16)],
        compiler_params=_params(vmem, ("arbitrary", "arbitrary")),
        name="gate_proj",
    )(h, lbp, w)
    vmem = 2 * tm * d * 2 + w_bytes + 2 * tm * tn * 2 + 2 * tm * tn * 4
    rest = pl.pallas_call(
        _rest_proj_kernel,
        grid=((n_cols - 2 * HG_WIDTH) // tn, n // tm),
        in_specs=[pl.BlockSpec((tm, d), lambda j, i: (i, 0)),
                  pl.BlockSpec((d, tn), lambda j, i: (0, n_gate + j))],
        out_specs=pl.BlockSpec((tm, tn), lambda j, i: (i, j)),
        out_shape=jax.ShapeDtypeStruct((n, n_cols - 2 * HG_WIDTH), BF16),
        scratch_shapes=[pltpu.VMEM((d, tn), BF16)],
        compiler_params=_params(vmem, ("arbitrary", "arbitrary")),
        name="rest_proj",
    )(h, w)
    return fdec, kk, rest


def _ctx_tri():
    s = np.arange(CTX_LEN)[:, None]
    u = np.arange(CTX_LEN)[None, :]
    tri = np.stack([u > s, u < s]).astype(np.float32)
    return jnp.asarray(np.concatenate([tri, tri], axis=2), dtype=BF16)


CTX_HEADS = 4


def _ctx_state_kernel(ff_ref, fb_ref, kf_ref, kb_ref, i_ref, tri_ref, sf_ref, sb_ref):
    dh = HG_HEAD_DIM

    def state(f, k, v, a):
        hi, lo = _split_f32(jnp.log(f))
        g = _dot(a, jnp.concatenate([hi, lo], axis=0).astype(BF16))
        kd = (k.astype(F32) * jnp.exp(g)).astype(BF16)
        return _dot_tn(v, kd)

    for j in range(CTX_HEADS):
        cols = slice(j * dh, (j + 1) * dh)
        v = i_ref[:, cols]
        sf_ref[j] = state(ff_ref[:, cols], kf_ref[:, cols], v, tri_ref[0])
        sb_ref[j] = state(fb_ref[:, cols], kb_ref[:, cols], v, tri_ref[1])


def _ctx_states(fdec, kk, vi):
    t, dh, nh = CTX_LEN, HG_HEAD_DIM, HG_HEADS
    groups = nh // CTX_HEADS
    blk = lambda off: pl.BlockSpec((t, CTX_HEADS * dh), lambda b, h: (b, off + h))
    st = pl.BlockSpec((None, CTX_HEADS, dh, dh), lambda b, h: (b, h, 0, 0))
    shape = jax.ShapeDtypeStruct((BATCH, nh, dh, dh), F32)
    vmem = CTX_HEADS * 2 * (2 * t * dh * 4 + 3 * t * dh * 2 + 2 * dh * dh * 4) + 2 * 2 * t * 2 * t * 2
    return pl.pallas_call(
        _ctx_state_kernel,
        grid=(BATCH, groups),
        in_specs=[blk(0), blk(groups), blk(0), blk(groups), blk(0),
                  pl.BlockSpec((2, t, 2 * t), lambda b, h: (0, 0, 0))],
        out_specs=[st, st],
        out_shape=[shape, shape],
        compiler_params=_params(vmem, ("arbitrary", "arbitrary")),
        name="ctx_state",
    )(fdec, fdec, kk, kk, vi, _ctx_tri())


def _pair_levels():
    c, nl = SCAN_CHUNK, SCAN_LEVELS
    t = np.arange(c)[:, None]
    u = np.arange(c)[None, :]
    lev = np.floor(np.log2(np.maximum(t ^ u, 1))).astype(np.int32)
    lidx = np.stack([np.where(t > u, lev, np.where(t == u, nl, -1)),
                     np.where(t < u, lev, np.where(t == u, nl, -1))]).astype(np.int32)
    return jnp.asarray(lidx)


def _swap_halves(x, h, t_idx):
    c = x.shape[0]
    assert 2 * h <= V7X_SUBLANES
    x3 = x.reshape(c // V7X_SUBLANES, V7X_SUBLANES, x.shape[1])
    if 2 * h == V7X_SUBLANES:
        return pltpu.roll(x3, h, axis=1).reshape(x.shape)
    up = pltpu.roll(x3, h, axis=1).reshape(x.shape)
    down = pltpu.roll(x3, V7X_SUBLANES - h, axis=1).reshape(x.shape)
    return jnp.where((t_idx & h) != 0, up, down)


def _scan_chunk(direction, f, k, q, v, st_ref, lidx, t_idx):
    c, nl = SCAN_CHUNK, SCAN_LEVELS
    qf = q.astype(F32)
    kf = k.astype(F32)
    q_dec = f
    k_dec = jnp.ones_like(f)
    total = f
    scores = jnp.zeros((c, c), F32)
    sublane = lax.broadcasted_iota(I32, (1, V7X_SUBLANES, f.shape[1]), 1)
    tiles = lambda a: a.reshape(c // V7X_SUBLANES, V7X_SUBLANES, a.shape[1])
    blocks = lambda a, h: [a[i * h:(i + 1) * h] for i in range(c // h)]
    for l in range(nl):
        h = 1 << l
        if h < V7X_SUBLANES:
            bit = (sublane & h) != 0
            is_query = bit if direction == 0 else jnp.logical_not(bit)
            x = jnp.where(is_query, tiles(q_dec * qf), tiles(k_dec * kf)).reshape(c, -1).astype(BF16)
            scores = jnp.where(lidx == l, _dot_nt(x, x), scores)
            other = _swap_halves(total, h, t_idx)
            q_dec = (tiles(q_dec) * jnp.where(is_query, tiles(other), 1.0)).reshape(c, -1)
            k_dec = (tiles(k_dec) * jnp.where(is_query, 1.0, tiles(other))).reshape(c, -1)
            total = total * other
        else:
            n_blocks = c // h
            is_query = [((i & 1) == 1) == (direction == 0) for i in range(n_blocks)]
            qd, kd, qb, kb, tb = (blocks(a, h) for a in (q_dec, k_dec, qf, kf, total))
            x = jnp.concatenate([qd[i] * qb[i] if is_query[i] else kd[i] * kb[i] for i in range(n_blocks)],
                                axis=0).astype(BF16)
            pb, sb, lb = (blocks(a, h) for a in (_dot_nt(x, x), scores, lidx))
            scores = jnp.concatenate([jnp.where(lb[i] == l, pb[i], sb[i]) if is_query[i] else sb[i]
                                      for i in range(n_blocks)], axis=0)
            ob = [tb[i ^ 1] for i in range(n_blocks)]
            q_dec = jnp.concatenate([qd[i] * ob[i] if is_query[i] else qd[i] for i in range(n_blocks)], axis=0)
            k_dec = jnp.concatenate([kd[i] if is_query[i] else kd[i] * ob[i] for i in range(n_blocks)], axis=0)
            total = jnp.concatenate([tb[i] * ob[i] for i in range(n_blocks)], axis=0)
    own = jnp.sum(qf * kf, axis=-1, keepdims=True)
    q_in = (q_dec * qf).astype(BF16)
    k_st = (k_dec * kf).astype(BF16)
    st = st_ref[...]
    o = _dot(scores.astype(BF16), v) + _dot_nt(q_in, st.astype(BF16)) + own * v.astype(F32)
    st_ref[...] = total[0:1] * st + _dot_tn(v, k_st)
    return o


def _scan_kernel(ff_ref, fb_ref, kf_ref, kb_ref, i_ref, q_ref, g_ref, s0f_ref, s0b_ref, ng_ref,
                 lidx_ref, o_ref, stf_ref, stb_ref, part_ref):
    c = SCAN_CHUNK
    n_chunks = SEQ // c
    stf_ref[...] = s0f_ref[...]
    stb_ref[...] = s0b_ref[...]
    t_idx = lax.broadcasted_iota(I32, (c, HG_HEAD_DIM), 0)
    ng = ng_ref[...]

    def chunk(direction, r0):
        rows = pl.ds(r0, c)
        f_ref, k_ref, st_ref = (ff_ref, kf_ref, stf_ref) if direction == 0 else (fb_ref, kb_ref, stb_ref)
        return _scan_chunk(direction, f_ref[rows, :], k_ref[rows, :], q_ref[rows, :], i_ref[rows, :],
                           st_ref, lidx_ref[direction], t_idx)

    def finalize(o, r0):
        rows = pl.ds(r0, c)
        gate = g_ref[rows, :].astype(F32)
        ms = jnp.mean(o * o, axis=-1, keepdims=True)
        o_ref[rows, :] = (o * lax.rsqrt(ms + EPS) * ng * (gate * jax.nn.sigmoid(gate))).astype(BF16)

    def starts(n):
        return pl.multiple_of(n * c, c), pl.multiple_of((n_chunks - 1 - n) * c, c)

    def first_half(n, carry):
        rf, rb = starts(n)
        part_ref[pl.ds(rf, c), :] = chunk(0, rf)
        part_ref[pl.ds(rb, c), :] = chunk(1, rb)
        return carry

    def second_half(n, carry):
        rf, rb = starts(n)
        finalize(chunk(0, rf) + part_ref[pl.ds(rf, c), :], rf)
        finalize(chunk(1, rb) + part_ref[pl.ds(rb, c), :], rb)
        return carry

    lax.fori_loop(0, n_chunks // 2, first_half, 0, unroll=SCAN_UNROLL)
    lax.fori_loop(n_chunks // 2, n_chunks, second_half, 0, unroll=SCAN_UNROLL)


def _hgrn2_scan(fdec, kk, rest, s0f, s0b, norm_g):
    dh, nh = HG_HEAD_DIM, HG_HEADS
    lidx = _pair_levels()
    blk = lambda off: pl.BlockSpec((SEQ, dh), lambda b, h: (b, off + h))
    st = pl.BlockSpec((None, None, dh, dh), lambda b, h: (b, h, 0, 0))
    vmem = (2 * (2 * SEQ * dh * 4 + 6 * SEQ * dh * 2 + 2 * dh * dh * 4) + SEQ * dh * 4 + 2 * dh * dh * 4
            + 2 * 2 * SCAN_CHUNK * V7X_LANES * 4)
    return pl.pallas_call(
        _scan_kernel,
        grid=(BATCH, nh),
        in_specs=[blk(0), blk(nh), blk(0), blk(nh), blk(0), blk(nh), blk(2 * nh), st, st,
                  pl.BlockSpec((1, dh), lambda b, h: (0, h)),
                  pl.BlockSpec(lidx.shape, lambda b, h: (0, 0, 0))],
        out_specs=blk(0),
        out_shape=jax.ShapeDtypeStruct((BATCH * SEQ, HG_WIDTH), BF16),
        scratch_shapes=[pltpu.VMEM((dh, dh), F32), pltpu.VMEM((dh, dh), F32), pltpu.VMEM((SEQ, dh), F32)],
        compiler_params=_params(vmem, ("arbitrary", "arbitrary")),
        name="hgrn2_scan",
    )(fdec, fdec, kk, kk, rest, rest, rest, s0f, s0b, norm_g, lidx)


POOL_PIECE = 256


def _box_bounds(n, w):
    start = np.arange(n) - w // 2
    return np.clip(start, 0, n), np.clip(start + w, 0, n)


def _pool_consts():
    rows = SEQ // GRID_W
    col_box = np.zeros((len(POOL_WINDOWS), POOL_PIECE, POOL_PIECE), np.float32)
    cnt = np.zeros((len(POOL_WINDOWS), SEQ, 1), np.float32)
    cc = np.arange(GRID_W)[None, :]
    for gi, w in enumerate(POOL_WINDOWS):
        c0, c1 = _box_bounds(GRID_W, w)
        r0, r1 = _box_bounds(rows, w)
        wc = ((cc >= c0[:, None]) & (cc < c1[:, None])).astype(np.float32)
        col_box[gi] = np.kron(np.eye(POOL_PIECE // GRID_W, dtype=np.float32), wc)
        cnt[gi] = ((r1 - r0)[:, None] * (c1 - c0)[None, :]).reshape(SEQ, 1)
    return jnp.asarray(col_box, dtype=BF16), jnp.asarray(cnt)


def _pool_kernel(v_ref, box_ref, cnt_ref, pw_ref, ps_ref, o_ref, pre_ref, diff_ref):
    gi = pl.program_id(1)
    rows = SEQ // GRID_W
    gw = GRID_W
    box = box_ref[...]
    pre_ref[0:gw, :] = jnp.zeros((gw, POOL_GROUP), F32)
    for p in range(SEQ // POOL_PIECE):
        yc = _dot(box, v_ref[p * POOL_PIECE:(p + 1) * POOL_PIECE, :])
        for rr in range(POOL_PIECE // gw):
            r = p * (POOL_PIECE // gw) + rr
            pre_ref[(r + 1) * gw:(r + 2) * gw, :] = pre_ref[r * gw:(r + 1) * gw, :] + yc[rr * gw:(rr + 1) * gw]
    for k, w in enumerate(POOL_WINDOWS):
        @pl.when(gi == k)
        def _(w=w):
            r0, r1 = _box_bounds(rows, w)
            for r in range(rows):
                sl = slice(r * gw, (r + 1) * gw)
                box_sum = pre_ref[int(r1[r]) * gw:(int(r1[r]) + 1) * gw, :] - pre_ref[int(r0[r]) * gw:(int(r0[r]) + 1) * gw, :]
                diff_ref[sl, :] = (box_sum / cnt_ref[sl, :] - v_ref[sl, :].astype(F32)).astype(BF16)
    o_ref[...] = (_dot(diff_ref[...], pw_ref[...].astype(BF16)) * ps_ref[...]).astype(BF16)


def _grid_pool(rest, pool_w, pool_scale):
    ng, pg = len(POOL_WINDOWS), POOL_GROUP
    col_box, cnt = _pool_consts()
    v_off = 3 * HG_WIDTH // pg
    vmem = (2 * (2 * SEQ * pg * 2 + POOL_PIECE * POOL_PIECE * 2 + SEQ * 128 * 4 + pg * pg * 4)
            + (SEQ + GRID_W) * pg * 4 + SEQ * pg * 2 + SEQ * pg * 4)
    return pl.pallas_call(
        _pool_kernel,
        grid=(BATCH, ng),
        in_specs=[pl.BlockSpec((SEQ, pg), lambda b, k: (b, v_off + k)),
                  pl.BlockSpec((None, POOL_PIECE, POOL_PIECE), lambda b, k: (k, 0, 0)),
                  pl.BlockSpec((None, SEQ, 1), lambda b, k: (k, 0, 0)),
                  pl.BlockSpec((None, pg, pg), lambda b, k: (k, 0, 0)),
                  pl.BlockSpec((1, pg), lambda b, k: (0, k))],
        out_specs=pl.BlockSpec((SEQ, pg), lambda b, k: (b, k)),
        out_shape=jax.ShapeDtypeStruct((BATCH * SEQ, POOL_WIDTH), BF16),
        scratch_shapes=[pltpu.VMEM((SEQ + GRID_W, pg), F32), pltpu.VMEM((SEQ, pg), BF16)],
        compiler_params=_params(vmem, ("arbitrary", "arbitrary")),
        name="grid_pool",
    )(rest, col_box, cnt, pool_w, pool_scale)


def _out_proj_kernel(o_ref, pm_ref, wa_ref, wb_ref, x_ref, g1_ref, out_ref, wa_scr, wb_scr):
    @pl.when(pl.program_id(1) == 0)
    def _():
        wa_scr[...] = wa_ref[...].astype(BF16)
        wb_scr[...] = wb_ref[...].astype(BF16)

    acc = _dot(o_ref[...], wa_scr[...]) + _dot(pm_ref[...], wb_scr[...])
    out_ref[...] = x_ref[...] + g1_ref[...] * acc


def _out_proj(o, pm, w, x2d, g1):
    n, d = x2d.shape
    tm, tn = 1024, 512
    half = w.shape[0] // 2
    tiles_per_sample = SEQ // tm
    vmem = 2 * (2 * tm * half * 2 + 2 * half * tn * 4 + 2 * tm * tn * 4) + 2 * half * tn * 2 + tm * tn * 4
    return pl.pallas_call(
        _out_proj_kernel,
        grid=(d // tn, n // tm),
        in_specs=[pl.BlockSpec((tm, half), lambda j, i: (i, 0)),
                  pl.BlockSpec((tm, half), lambda j, i: (i, 0)),
                  pl.BlockSpec((half, tn), lambda j, i: (0, j)),
                  pl.BlockSpec((half, tn), lambda j, i: (1, j)),
                  pl.BlockSpec((tm, tn), lambda j, i: (i, j)),
                  pl.BlockSpec((None, 1, tn), lambda j, i: (i // tiles_per_sample, 0, j))],
        out_specs=pl.BlockSpec((tm, tn), lambda j, i: (i, j)),
        out_shape=jax.ShapeDtypeStruct((n, d), F32),
        scratch_shapes=[pltpu.VMEM((half, tn), BF16), pltpu.VMEM((half, tn), BF16)],
        compiler_params=_params(vmem, ("arbitrary", "arbitrary")),
        name="out_proj",
    )(o, pm, w, w, x2d, g1)


def _norm2_router_kernel(x_ref, g_ref, sc_ref, sh_ref, rw_ref, h2_ref, aff_ref):
    h = _modulated_norm(x_ref[...], g_ref[...], sc_ref[...], sh_ref[...])
    h2_ref[...] = h
    h_hi, h_lo = _split_f32(h)
    r_hi, r_lo = _split_f32(rw_ref[...])
    h_hi, h_lo, r_hi, r_lo = (a.astype(BF16) for a in (h_hi, h_lo, r_hi, r_lo))
    logits = _dot_nt(r_hi, h_hi) + (_dot_nt(r_hi, h_lo) + _dot_nt(r_lo, h_hi))
    e = jnp.exp(logits - jnp.max(logits, axis=0, keepdims=True))
    aff_ref[...] = e / jnp.sum(e, axis=0, keepdims=True)


def _norm2_router(x1, g, sc, sh, router_wt):
    n, d = x1.shape
    tm = 256
    tiles_per_sample = SEQ // tm
    vmem = 2 * (2 * tm * d * 4 + N_EXPERTS * d * 4) + 6 * tm * d * 4
    return pl.pallas_call(
        _norm2_router_kernel,
        grid=(n // tm,),
        in_specs=[pl.BlockSpec((tm, d), lambda i: (i, 0)),
                  pl.BlockSpec((1, d), lambda i: (0, 0)),
                  pl.BlockSpec((None, 1, d), lambda i: (i // tiles_per_sample, 0, 0)),
                  pl.BlockSpec((None, 1, d), lambda i: (i // tiles_per_sample, 0, 0)),
                  pl.BlockSpec((N_EXPERTS, d), lambda i: (0, 0))],
        out_specs=[pl.BlockSpec((tm, d), lambda i: (i, 0)),
                   pl.BlockSpec((None, N_EXPERTS, tm), lambda i: (i // tiles_per_sample, 0, i % tiles_per_sample))],
        out_shape=[jax.ShapeDtypeStruct((n, d), F32),
                   jax.ShapeDtypeStruct((BATCH, N_EXPERTS, SEQ), F32)],
        compiler_params=_params(vmem, ("arbitrary",)),
        name="norm2_router",
    )(x1, g, sc, sh, router_wt)


ROUTE_COLS = 256
ROUTE_ROWS = 32
ROUTE_BITS = 3
TOKEN_DIGIT_BITS = 6
TOKEN_DIGIT = 1 << TOKEN_DIGIT_BITS


def _route_kernel(aff_ref, slot_ref, gate_ref, idx_ref):
    a = aff_ref[...]
    rows, n = a.shape
    thr = jnp.zeros((rows, 1), I32)
    for top in range(30, -1, -ROUTE_BITS):
        low = max(top - ROUTE_BITS + 1, 0)
        best = thr
        for pattern in range(1, 1 << (top - low + 1)):
            cand = thr | (pattern << low)
            cnt = jnp.sum((a >= pltpu.bitcast(cand, F32)).astype(I32), axis=-1, keepdims=True)
            best = jnp.where(cnt >= CAPACITY, cand, best)
        thr = best
    gt = a >= pltpu.bitcast(thr + 1, F32)
    eq = jnp.logical_and(a >= pltpu.bitcast(thr, F32), jnp.logical_not(gt))
    need = CAPACITY - jnp.sum(gt.astype(I32), axis=-1, keepdims=True)

    def prefix_count(mask):
        m = jnp.where(mask, 1.0, 0.0).astype(BF16)
        r = lax.broadcasted_iota(I32, (n, ROUTE_COLS), 0)
        c = lax.broadcasted_iota(I32, (n, ROUTE_COLS), 1)
        parts = [_dot(m, jnp.where(r < c + cb * ROUTE_COLS, 1.0, 0.0).astype(BF16))
                 for cb in range(n // ROUTE_COLS)]
        return jnp.concatenate(parts, axis=1).astype(I32)

    sel = jnp.logical_or(gt, jnp.logical_and(eq, prefix_count(eq) < need))
    slot = jnp.where(sel, prefix_count(sel), -1)
    slot_ref[...] = slot

    a_hi, a_rest = _split_f32(a)
    a_mid, a_lo = _split_f32(a_rest)
    tok = lax.broadcasted_iota(I32, (1, n), 1)
    tok_hi = (tok >> TOKEN_DIGIT_BITS).astype(F32)
    tok_lo = (tok & (TOKEN_DIGIT - 1)).astype(F32)
    slot_ids = lax.broadcasted_iota(I32, (CAPACITY, n), 0)
    pad = jnp.zeros((ROUTE_ROWS - 5, n), F32)
    for r in range(rows):
        onehot = jnp.where(slot[r:r + 1] == slot_ids, 1.0, 0.0).astype(BF16)
        pieces = jnp.concatenate([a_hi[r:r + 1], a_mid[r:r + 1], a_lo[r:r + 1], tok_hi, tok_lo, pad],
                                 axis=0).astype(BF16)
        res = _dot_nt(pieces, onehot)
        gate_ref[r:r + 1, :] = res[0:1] + res[1:2] + res[2:3]
        idx_ref[r:r + 1, :] = (res[3:4] * TOKEN_DIGIT + res[4:5]).astype(I32)


def _route(aff_rows):
    rows, n = aff_rows.shape
    vmem = 4 * ROUTE_ROWS * n * 4 + 4 * n * ROUTE_COLS * 4 + 4 * CAPACITY * n * 4
    return pl.pallas_call(
        _route_kernel,
        grid=(rows // ROUTE_ROWS,),
        in_specs=[pl.BlockSpec((ROUTE_ROWS, n), lambda i: (i, 0))],
        out_specs=[pl.BlockSpec((ROUTE_ROWS, n), lambda i: (i, 0)),
                   pl.BlockSpec((ROUTE_ROWS, CAPACITY), lambda i: (i, 0)),
                   pl.BlockSpec((ROUTE_ROWS, CAPACITY), lambda i: (i, 0))],
        out_shape=[jax.ShapeDtypeStruct((rows, n), I32),
                   jax.ShapeDtypeStruct((rows, CAPACITY), F32),
                   jax.ShapeDtypeStruct((rows, CAPACITY), I32)],
        compiler_params=_params(vmem, ("arbitrary",)),
        name="route",
    )(aff_rows)


CAST_ROWS = 64


def _moe_up_kernel(idx_ref, h2_hbm, w1_ref, w3_ref, hid_ref, rows32, rows16, sem):
    e, f = pl.program_id(0), pl.program_id(1)
    n_experts = pl.num_programs(0)
    n_ff, share, _ = rows32.shape
    m = n_ff * share

    def row_copy(src_row, group, r):
        return pltpu.make_async_copy(h2_hbm.at[pl.ds(src_row, 1), :], rows32.at[group, pl.ds(r, 1), :], sem)

    def start_group(expert, group):
        first = expert * m + group * share
        for r in range(share):
            row_copy(idx_ref[first + r], group, r).start()

    @pl.when(jnp.logical_and(e == 0, f == 0))
    def _():
        lax.fori_loop(0, n_ff, lambda grp, carry: (start_group(0, grp), carry)[1], 0)

    @pl.when(f == 0)
    def _():
        def wait_group(grp, carry):
            for r in range(share):
                row_copy(0, grp, r).wait()
            return carry
        lax.fori_loop(0, n_ff, wait_group, 0)

        def cast_group(grp, carry):
            for c0 in range(0, share, CAST_ROWS):
                dst = pl.ds(pl.multiple_of(grp * share + c0, CAST_ROWS), CAST_ROWS)
                rows16[dst, :] = rows32[grp, c0:c0 + CAST_ROWS, :].astype(BF16)
            return carry
        lax.fori_loop(0, n_ff, cast_group, 0)

    @pl.when(e + 1 < n_experts)
    def _():
        start_group(e + 1, f)

    xg = rows16[...]
    a = _dot(xg, w1_ref[...].astype(BF16))
    b = _dot(xg, w3_ref[...].astype(BF16))
    hid_ref[...] = (a * jax.nn.sigmoid(a) * b).astype(BF16)


def _moe_up(idx_table, h2, w1, w3):
    ne, m = idx_table.shape
    d = h2.shape[1]
    ff = w1.shape[2]
    tf = 256
    vmem = m * d * (4 + 2) + 2 * 2 * d * tf * 4 + 2 * d * tf * 2 + 3 * m * tf * 4 + 2 * m * tf * 2
    n_ff = ff // tf
    w_spec = pl.BlockSpec((None, d, tf), lambda e, f, idx: (e, 0, f))
    return pl.pallas_call(
        _moe_up_kernel,
        grid_spec=pltpu.PrefetchScalarGridSpec(
            num_scalar_prefetch=1,
            grid=(ne, n_ff),
            in_specs=[pl.BlockSpec(memory_space=pl.ANY), w_spec, w_spec],
            out_specs=pl.BlockSpec((None, m, tf), lambda e, f, idx: (e, 0, f)),
            scratch_shapes=[pltpu.VMEM((n_ff, m // n_ff, d), F32), pltpu.VMEM((m, d), BF16),
                            pltpu.SemaphoreType.DMA(())]),
        out_shape=jax.ShapeDtypeStruct((ne, m, ff), BF16),
        compiler_params=_params(vmem, ("arbitrary", "arbitrary")),
        name="moe_up",
    )(idx_table.reshape(ne * m), h2, w1, w3)


def _moe_down_kernel(hid_ref, w2_ref, gate_ref, y_ref):
    y = _dot(hid_ref[...], w2_ref[...].astype(BF16))
    y_ref[...] = (y * gate_ref[...]).astype(BF16)


def _moe_down(hid, w2, gate_col):
    ne, m, ff = hid.shape
    d = w2.shape[2]
    tn = 1024
    vmem = 2 * m * ff * 2 + 2 * ff * tn * 4 + ff * tn * 2 + 2 * m * 128 * 4 + 2 * m * tn * 4 + 2 * m * tn * 2
    return pl.pallas_call(
        _moe_down_kernel,
        grid=(ne, d // tn),
        in_specs=[pl.BlockSpec((None, m, ff), lambda e, j: (e, 0, 0)),
                  pl.BlockSpec((None, ff, tn), lambda e, j: (e, 0, j)),
                  pl.BlockSpec((None, m, 1), lambda e, j: (e, 0, 0))],
        out_specs=pl.BlockSpec((None, m, tn), lambda e, j: (e, 0, j)),
        out_shape=jax.ShapeDtypeStruct((ne, m, d), BF16),
        compiler_params=_params(vmem, ("arbitrary", "arbitrary")),
        name="moe_down",
    )(hid, w2, gate_col)


COMBINE_TOKENS = 128


def _moe_combine_kernel(slot_ref, y_ref, x1_ref, g2_ref, fg_ref, out_ref):
    st = slot_ref[...]
    slot_ids = lax.broadcasted_iota(I32, (COMBINE_TOKENS, CAPACITY), 1)
    onehot = jnp.concatenate(
        [jnp.where(st[:, e:e + 1] == slot_ids, 1.0, 0.0).astype(BF16) for e in range(N_EXPERTS)], axis=1)
    y = y_ref[...].reshape(N_EXPERTS * CAPACITY, y_ref.shape[-1])
    x2 = x1_ref[...] + g2_ref[...] * _dot(onehot, y)
    ms = jnp.mean(x2 * x2, axis=-1, keepdims=True)
    out_ref[...] = x2 * lax.rsqrt(ms + EPS) * fg_ref[...]


def _moe_combine(slot_ble, y, x1, g2, final_g):
    n, d = x1.shape
    tm = COMBINE_TOKENS
    tiles_per_sample = SEQ // tm
    vmem = (N_EXPERTS * CAPACITY * d * 2 + 2 * 2 * tm * d * 4 + 2 * tm * 128 * 4
            + tm * N_EXPERTS * CAPACITY * 2 + 3 * tm * d * 4)
    return pl.pallas_call(
        _moe_combine_kernel,
        grid=(BATCH, tiles_per_sample),
        in_specs=[pl.BlockSpec((None, tm, N_EXPERTS), lambda b, t: (b, t, 0)),
                  pl.BlockSpec((N_EXPERTS, None, CAPACITY, d), lambda b, t: (0, b, 0, 0),
                               pipeline_mode=pl.Buffered(1)),
                  pl.BlockSpec((tm, d), lambda b, t: (b * tiles_per_sample + t, 0)),
                  pl.BlockSpec((None, 1, d), lambda b, t: (b, 0, 0)),
                  pl.BlockSpec((1, d), lambda b, t: (0, 0))],
        out_specs=pl.BlockSpec((tm, d), lambda b, t: (b * tiles_per_sample + t, 0)),
        out_shape=jax.ShapeDtypeStruct((n, d), F32),
        compiler_params=_params(vmem, ("arbitrary", "arbitrary")),
        name="moe_combine",
    )(slot_ble, y, x1, g2, final_g)


def kernel(x, c, ctx, c_ctx, ada_w, ada_b, norm1_g, norm2_g, w_in, lb_param, hg_norm_g, pool_w, pool_scale,
           w_out, router_w, moe_w1, moe_w3, moe_w2, final_norm_g):
    nb, seq, d = x.shape
    assert (nb, seq, d) == (BATCH, SEQ, D_MODEL) and ctx.shape[1] == CTX_LEN
    assert ada_w.shape[0] == 1 and lb_param.shape[0] == 2, "single-layer block: layer 0 uses lower-bound row 0"
    x2d = x.reshape(nb * seq, d)
    ctx2d = ctx.reshape(nb * CTX_LEN, d)

    cvecs = jnp.zeros((ADA_ROWS, d), F32).at[:nb].set(c).at[nb].set(c_ctx)
    mod = _ada_mod(cvecs, ada_w[0], ada_b).reshape(ADA_ROWS, 6, d)
    sh1, sc1, g1, sh2, sc2, g2 = (mod[:nb, k][:, None, :] for k in range(6))
    csh1, csc1 = mod[nb:nb + 1, 0][:, None, :], mod[nb:nb + 1, 1][:, None, :]

    lbp = lb_param.reshape(lb_param.shape[0], 2 * HG_WIDTH)

    h_c = _norm1(ctx2d, norm1_g, csc1, csh1, rows_per_sample=nb * CTX_LEN)
    f_c, k_c, i_c = _in_proj(h_c, lbp, w_in[0], n_cols=3 * HG_WIDTH)
    s_f, s_b = _ctx_states(f_c, k_c, i_c)

    h_x = _norm1(x2d, norm1_g, sc1, sh1, rows_per_sample=seq)
    fdec, kk, rest = _in_proj(h_x, lbp, w_in[0], n_cols=w_in.shape[2])
    o = _hgrn2_scan(fdec, kk, rest, s_f, s_b, hg_norm_g)
    pm = _grid_pool(rest, pool_w[0], pool_scale)
    x1 = _out_proj(o, pm, w_out[0], x2d, g1)

    h2, aff_t = _norm2_router(x1, norm2_g, sc2, sh2, router_w[0].T)
    slot, gates, tok = _route(aff_t.reshape(nb * N_EXPERTS, seq))
    slot = slot.reshape(nb, N_EXPERTS, seq)
    h2_rows = tok.reshape(nb, N_EXPERTS, CAPACITY) + (jnp.arange(nb, dtype=I32) * seq)[:, None, None]
    hid = _moe_up(h2_rows.transpose(1, 0, 2).reshape(N_EXPERTS, nb * CAPACITY), h2, moe_w1[0], moe_w3[0])
    gate_col = gates.reshape(nb, N_EXPERTS, CAPACITY).transpose(1, 0, 2).reshape(N_EXPERTS, nb * CAPACITY, 1)
    y = _moe_down(hid, moe_w2[0], gate_col).reshape(N_EXPERTS, nb, CAPACITY, d)
    out = _moe_combine(slot.transpose(0, 2, 1), y, x1, g2, final_norm_g[None, :])
    return out.reshape(nb, seq, d).astype(x.dtype)
```

```python
import jax
import jax.numpy as jnp
import numpy as np
from jax import lax
from jax.experimental import pallas as pl
from jax.experimental.pallas import tpu as pltpu

F32 = jnp.float32
BF16 = jnp.bfloat16
I32 = jnp.int32

D_MODEL = 4096
BATCH = 4
SEQ = 2048
GRID_W = 64
CTX_LEN = 256
HG_HEADS = 16
HG_HEAD_DIM = 128
HG_WIDTH = HG_HEADS * HG_HEAD_DIM
POOL_WIDTH = D_MODEL - HG_WIDTH
POOL_WINDOWS = (2, 4, 8, 16)
POOL_GROUP = POOL_WIDTH // len(POOL_WINDOWS)
N_EXPERTS = 16
EXPERT_FF = D_MODEL // 2
CAPACITY = 2 * SEQ // N_EXPERTS
EPS = 1e-6

V7X_VMEM_BYTES = 64 * 1024 * 1024
V7X_VMEM_HEADROOM_BYTES = 10 * 1024 * 1024
V7X_SUBLANES = 8
V7X_LANES = 128
NORM_ROWS = 16
ADA_ROWS = 8

SCAN_CHUNK = 128
SCAN_LEVELS = 7
SCAN_UNROLL = 8

_NT = (((1,), (1,)), ((), ()))
_TN = (((0,), (0,)), ((), ()))


def _dot(a, b):
    return jnp.dot(a, b, preferred_element_type=F32)


def _dot_nt(a, b):
    return lax.dot_general(a, b, _NT, preferred_element_type=F32)


def _dot_tn(a, b):
    return lax.dot_general(a, b, _TN, preferred_element_type=F32)


def _split_f32(x):
    hi = x.astype(BF16).astype(F32)
    return hi, x - hi


def _params(vmem_bytes, semantics):
    limit = min(int(vmem_bytes) + V7X_VMEM_HEADROOM_BYTES, V7X_VMEM_BYTES)
    return pltpu.CompilerParams(dimension_semantics=semantics, vmem_limit_bytes=limit)


def _ada_kernel(c_ref, w_ref, b_ref, o_ref):
    c = c_ref[...]
    s = c * jax.nn.sigmoid(c)
    hi, lo = _split_f32(s)
    lhs = jnp.concatenate([hi, lo], axis=0).astype(BF16)
    r = _dot(lhs, w_ref[...].astype(BF16))
    o_ref[...] = r[:ADA_ROWS] + r[ADA_ROWS:] + b_ref[...]


def _ada_mod(cvecs, w, b):
    d, n = w.shape
    tn = 512
    vmem = 2 * d * tn * 4 + d * tn * 2 + 4 * ADA_ROWS * d * 4
    return pl.pallas_call(
        _ada_kernel,
        grid=(n // tn,),
        in_specs=[pl.BlockSpec((ADA_ROWS, d), lambda j: (0, 0)),
                  pl.BlockSpec((d, tn), lambda j: (0, j)),
                  pl.BlockSpec((1, tn), lambda j: (0, j))],
        out_specs=pl.BlockSpec((ADA_ROWS, tn), lambda j: (0, j)),
        out_shape=jax.ShapeDtypeStruct((ADA_ROWS, n), F32),
        compiler_params=_params(vmem, ("arbitrary",)),
        name="ada_mod",
    )(cvecs, w, b)


def _modulated_norm(x, g, scale, shift):
    ms = jnp.mean(x * x, axis=-1, keepdims=True)
    return (x * lax.rsqrt(ms + EPS) * g) * (1.0 + scale) + shift


def _norm1_kernel(x_ref, g_ref, sc_ref, sh_ref, h_ref):
    g, sc, sh = g_ref[...], sc_ref[...], sh_ref[...]

    def norm_rows(r, carry):
        rows = pl.ds(pl.multiple_of(r * NORM_ROWS, NORM_ROWS), NORM_ROWS)
        h_ref[rows, :] = _modulated_norm(x_ref[rows, :], g, sc, sh).astype(BF16)
        return carry

    lax.fori_loop(0, x_ref.shape[0] // NORM_ROWS, norm_rows, 0, unroll=2)


def _norm1(x2d, g, sc, sh, *, rows_per_sample):
    n, d = x2d.shape
    tm = 512
    tiles_per_sample = rows_per_sample // tm
    mod = pl.BlockSpec((None, 1, d), lambda i: (i // tiles_per_sample, 0, 0))
    vmem = 2 * tm * d * (4 + 2) + 8 * NORM_ROWS * d * 4
    return pl.pallas_call(
        _norm1_kernel,
        grid=(n // tm,),
        in_specs=[pl.BlockSpec((tm, d), lambda i: (i, 0)), pl.BlockSpec((1, d), lambda i: (0, 0)), mod, mod],
        out_specs=pl.BlockSpec((tm, d), lambda i: (i, 0)),
        out_shape=jax.ShapeDtypeStruct((n, d), BF16),
        compiler_params=_params(vmem, ("arbitrary",)),
        name="norm1",
    )(x2d, g, sc, sh)


def _gate_proj_kernel(h_ref, lbp_ref, w_ref, f_ref, k_ref, w_scr):
    @pl.when(pl.program_id(1) == 0)
    def _():
        w_scr[...] = w_ref[...].astype(BF16)

    z = _dot(h_ref[...], w_scr[...])
    p = lbp_ref[...]
    e = jnp.exp(p - jnp.max(p, axis=0, keepdims=True))
    lb = e[0:1] / jnp.sum(e, axis=0, keepdims=True)
    f = lb + (1.0 - lb) * jax.nn.sigmoid(z)
    f_ref[...] = f
    k_ref[...] = (1.0 - f).astype(BF16)


def _rest_proj_kernel(h_ref, w_ref, r_ref, w_scr):
    @pl.when(pl.program_id(1) == 0)
    def _():
        w_scr[...] = w_ref[...].astype(BF16)

    r_ref[...] = _dot(h_ref[...], w_scr[...]).astype(BF16)


def _in_proj(h, lbp, w, *, n_cols):
    n, d = h.shape
    tm, tn = 1024, 512
    n_gate = 2 * HG_WIDTH // tn
    w_bytes = 2 * d * tn * 4 + d * tn * 2
    vmem = 2 * tm * d * 2 + w_bytes + 2 * tm * tn * (4 + 2) + 4 * tm * tn * 4
    fdec, kk = pl.pallas_call(
        _gate_proj_kernel,
        grid=(n_gate, n // tm),
        in_specs=[pl.BlockSpec((tm, d), lambda j, i: (i, 0)),
                  pl.BlockSpec((2, tn), lambda j, i: (0, j)),
                  pl.BlockSpec((d, tn), lambda j, i: (0, j))],
        out_specs=[pl.BlockSpec((tm, tn), lambda j, i: (i, j)),
                   pl.BlockSpec((tm, tn), lambda j, i: (i, j))],
        out_shape=[jax.ShapeDtypeStruct((n, 2 * HG_WIDTH), F32),
                   jax.ShapeDtypeStruct((n, 2 * HG_WIDTH), BF16)],
        scratch_shapes=[pltpu.VMEM((d, tn), BF16)],
        compiler_params=_params(vmem, ("arbitrary", "arbitrary")),
        name="gate_proj",
    )(h, lbp, w)
    vmem = 2 * tm * d * 2 + w_bytes + 2 * tm * tn * 2 + 2 * tm * tn * 4
    rest = pl.pallas_call(
        _rest_proj_kernel,
        grid=((n_cols - 2 * HG_WIDTH) // tn, n // tm),
        in_specs=[pl.BlockSpec((tm, d), lambda j, i: (i, 0)),
                  pl.BlockSpec((d, tn), lambda j, i: (0, n_gate + j))],
        out_specs=pl.BlockSpec((tm, tn), lambda j, i: (i, j)),
        out_shape=jax.ShapeDtypeStruct((n, n_cols - 2 * HG_WIDTH), BF16),
        scratch_shapes=[pltpu.VMEM((d, tn), BF16)],
        compiler_params=_params(vmem, ("arbitrary", "arbitrary")),
        name="rest_proj",
    )(h, w)
    return fdec, kk, rest


def _ctx_tri():
    s = np.arange(CTX_LEN)[:, None]
    u = np.arange(CTX_LEN)[None, :]
    tri = np.stack([u > s, u < s]).astype(np.float32)
    return jnp.asarray(np.concatenate([tri, tri], axis=2), dtype=BF16)


CTX_HEADS = 4


def _ctx_state_kernel(ff_ref, fb_ref, kf_ref, kb_ref, i_ref, tri_ref, sf_ref, sb_ref):
    dh = HG_HEAD_DIM

    def state(f, k, v, a):
        hi, lo = _split_f32(jnp.log(f))
        g = _dot(a, jnp.concatenate([hi, lo], axis=0).astype(BF16))
        kd = (k.astype(F32) * jnp.exp(g)).astype(BF16)
        return _dot_tn(v, kd)

    for j in range(CTX_HEADS):
        cols = slice(j * dh, (j + 1) * dh)
        v = i_ref[:, cols]
        sf_ref[j] = state(ff_ref[:, cols], kf_ref[:, cols], v, tri_ref[0])
        sb_ref[j] = state(fb_ref[:, cols], kb_ref[:, cols], v, tri_ref[1])


def _ctx_states(fdec, kk, vi):
    t, dh, nh = CTX_LEN, HG_HEAD_DIM, HG_HEADS
    groups = nh // CTX_HEADS
    blk = lambda off: pl.BlockSpec((t, CTX_HEADS * dh), lambda b, h: (b, off + h))
    st = pl.BlockSpec((None, CTX_HEADS, dh, dh), lambda b, h: (b, h, 0, 0))
    shape = jax.ShapeDtypeStruct((BATCH, nh, dh, dh), F32)
    vmem = CTX_HEADS * 2 * (2 * t * dh * 4 + 3 * t * dh * 2 + 2 * dh * dh * 4) + 2 * 2 * t * 2 * t * 2
    return pl.pallas_call(
        _ctx_state_kernel,
        grid=(BATCH, groups),
        in_specs=[blk(0), blk(groups), blk(0), blk(groups), blk(0),
                  pl.BlockSpec((2, t, 2 * t), lambda b, h: (0, 0, 0))],
        out_specs=[st, st],
        out_shape=[shape, shape],
        compiler_params=_params(vmem, ("arbitrary", "arbitrary")),
        name="ctx_state",
    )(fdec, fdec, kk, kk, vi, _ctx_tri())


def _pair_levels():
    c, nl = SCAN_CHUNK, SCAN_LEVELS
    t = np.arange(c)[:, None]
    u = np.arange(c)[None, :]
    lev = np.floor(np.log2(np.maximum(t ^ u, 1))).astype(np.int32)
    lidx = np.stack([np.where(t > u, lev, np.where(t == u, nl, -1)),
                     np.where(t < u, lev, np.where(t == u, nl, -1))]).astype(np.int32)
    return jnp.asarray(lidx)


def _swap_halves(x, h, t_idx):
    c = x.shape[0]
    assert 2 * h <= V7X_SUBLANES
    x3 = x.reshape(c // V7X_SUBLANES, V7X_SUBLANES, x.shape[1])
    if 2 * h == V7X_SUBLANES:
        return pltpu.roll(x3, h, axis=1).reshape(x.shape)
    up = pltpu.roll(x3, h, axis=1).reshape(x.shape)
    down = pltpu.roll(x3, V7X_SUBLANES - h, axis=1).reshape(x.shape)
    return jnp.where((t_idx & h) != 0, up, down)


def _scan_chunk(direction, f, k, q, v, st_ref, lidx, t_idx):
    c, nl = SCAN_CHUNK, SCAN_LEVELS
    qf = q.astype(F32)
    kf = k.astype(F32)
    q_dec = f
    k_dec = jnp.ones_like(f)
    total = f
    scores = jnp.zeros((c, c), F32)
    sublane = lax.broadcasted_iota(I32, (1, V7X_SUBLANES, f.shape[1]), 1)
    tiles = lambda a: a.reshape(c // V7X_SUBLANES, V7X_SUBLANES, a.shape[1])
    blocks = lambda a, h: [a[i * h:(i + 1) * h] for i in range(c // h)]
    for l in range(nl):
        h = 1 << l
        if h < V7X_SUBLANES:
            bit = (sublane & h) != 0
            is_query = bit if direction == 0 else jnp.logical_not(bit)
            x = jnp.where(is_query, tiles(q_dec * qf), tiles(k_dec * kf)).reshape(c, -1).astype(BF16)
            scores = jnp.where(lidx == l, _dot_nt(x, x), scores)
            other = _swap_halves(total, h, t_idx)
            q_dec = (tiles(q_dec) * jnp.where(is_query, tiles(other), 1.0)).reshape(c, -1)
            k_dec = (tiles(k_dec) * jnp.where(is_query, 1.0, tiles(other))).reshape(c, -1)
            total = total * other
        else:
            n_blocks = c // h
            is_query = [((i & 1) == 1) == (direction == 0) for i in range(n_blocks)]
            qd, kd, qb, kb, tb = (blocks(a, h) for a in (q_dec, k_dec, qf, kf, total))
            x = jnp.concatenate([qd[i] * qb[i] if is_query[i] else kd[i] * kb[i] for i in range(n_blocks)],
                                axis=0).astype(BF16)
            pb, sb, lb = (blocks(a, h) for a in (_dot_nt(x, x), scores, lidx))
            scores = jnp.concatenate([jnp.where(lb[i] == l, pb[i], sb[i]) if is_query[i] else sb[i]
                                      for i in range(n_blocks)], axis=0)
            ob = [tb[i ^ 1] for i in range(n_blocks)]
            q_dec = jnp.concatenate([qd[i] * ob[i] if is_query[i] else qd[i] for i in range(n_blocks)], axis=0)
            k_dec = jnp.concatenate([kd[i] if is_query[i] else kd[i] * ob[i] for i in range(n_blocks)], axis=0)
            total = jnp.concatenate([tb[i] * ob[i] for i in range(n_blocks)], axis=0)
    own = jnp.sum(qf * kf, axis=-1, keepdims=True)
    q_in = (q_dec * qf).astype(BF16)
    k_st = (k_dec * kf).astype(BF16)
    st = st_ref[...]
    o = _dot(scores.astype(BF16), v) + _dot_nt(q_in, st.astype(BF16)) + own * v.astype(F32)
    st_ref[...] = total[0:1] * st + _dot_tn(v, k_st)
    return o


def _scan_kernel(ff_ref, fb_ref, kf_ref, kb_ref, i_ref, q_ref, g_ref, s0f_ref, s0b_ref, ng_ref,
                 lidx_ref, o_ref, stf_ref, stb_ref, part_ref):
    c = SCAN_CHUNK
    n_chunks = SEQ // c
    stf_ref[...] = s0f_ref[...]
    stb_ref[...] = s0b_ref[...]
    t_idx = lax.broadcasted_iota(I32, (c, HG_HEAD_DIM), 0)
    ng = ng_ref[...]

    def chunk(direction, r0):
        rows = pl.ds(r0, c)
        f_ref, k_ref, st_ref = (ff_ref, kf_ref, stf_ref) if direction == 0 else (fb_ref, kb_ref, stb_ref)
        return _scan_chunk(direction, f_ref[rows, :], k_ref[rows, :], q_ref[rows, :], i_ref[rows, :],
                           st_ref, lidx_ref[direction], t_idx)

    def finalize(o, r0):
        rows = pl.ds(r0, c)
        gate = g_ref[rows, :].astype(F32)
        ms = jnp.mean(o * o, axis=-1, keepdims=True)
        o_ref[rows, :] = (o * lax.rsqrt(ms + EPS) * ng * (gate * jax.nn.sigmoid(gate))).astype(BF16)

    def starts(n):
        return pl.multiple_of(n * c, c), pl.multiple_of((n_chunks - 1 - n) * c, c)

    def first_half(n, carry):
        rf, rb = starts(n)
        part_ref[pl.ds(rf, c), :] = chunk(0, rf)
        part_ref[pl.ds(rb, c), :] = chunk(1, rb)
        return carry

    def second_half(n, carry):
        rf, rb = starts(n)
        finalize(chunk(0, rf) + part_ref[pl.ds(rf, c), :], rf)
        finalize(chunk(1, rb) + part_ref[pl.ds(rb, c), :], rb)
        return carry

    lax.fori_loop(0, n_chunks // 2, first_half, 0, unroll=SCAN_UNROLL)
    lax.fori_loop(n_chunks // 2, n_chunks, second_half, 0, unroll=SCAN_UNROLL)


def _hgrn2_scan(fdec, kk, rest, s0f, s0b, norm_g):
    dh, nh = HG_HEAD_DIM, HG_HEADS
    lidx = _pair_levels()
    blk = lambda off: pl.BlockSpec((SEQ, dh), lambda b, h: (b, off + h))
    st = pl.BlockSpec((None, None, dh, dh), lambda b, h: (b, h, 0, 0))
    vmem = (2 * (2 * SEQ * dh * 4 + 6 * SEQ * dh * 2 + 2 * dh * dh * 4) + SEQ * dh * 4 + 2 * dh * dh * 4
            + 2 * 2 * SCAN_CHUNK * V7X_LANES * 4)
    return pl.pallas_call(
        _scan_kernel,
        grid=(BATCH, nh),
        in_specs=[blk(0), blk(nh), blk(0), blk(nh), blk(0), blk(nh), blk(2 * nh), st, st,
                  pl.BlockSpec((1, dh), lambda b, h: (0, h)),
                  pl.BlockSpec(lidx.shape, lambda b, h: (0, 0, 0))],
        out_specs=blk(0),
        out_shape=jax.ShapeDtypeStruct((BATCH * SEQ, HG_WIDTH), BF16),
        scratch_shapes=[pltpu.VMEM((dh, dh), F32), pltpu.VMEM((dh, dh), F32), pltpu.VMEM((SEQ, dh), F32)],
        compiler_params=_params(vmem, ("arbitrary", "arbitrary")),
        name="hgrn2_scan",
    )(fdec, fdec, kk, kk, rest, rest, rest, s0f, s0b, norm_g, lidx)


POOL_PIECE = 256


def _box_bounds(n, w):
    start = np.arange(n) - w // 2
    return np.clip(start, 0, n), np.clip(start + w, 0, n)


def _pool_consts():
    rows = SEQ // GRID_W
    col_box = np.zeros((len(POOL_WINDOWS), POOL_PIECE, POOL_PIECE), np.float32)
    cnt = np.zeros((len(POOL_WINDOWS), SEQ, 1), np.float32)
    cc = np.arange(GRID_W)[None, :]
    for gi, w in enumerate(POOL_WINDOWS):
        c0, c1 = _box_bounds(GRID_W, w)
        r0, r1 = _box_bounds(rows, w)
        wc = ((cc >= c0[:, None]) & (cc < c1[:, None])).astype(np.float32)
        col_box[gi] = np.kron(np.eye(POOL_PIECE // GRID_W, dtype=np.float32), wc)
        cnt[gi] = ((r1 - r0)[:, None] * (c1 - c0)[None, :]).reshape(SEQ, 1)
    return jnp.asarray(col_box, dtype=BF16), jnp.asarray(cnt)


def _pool_kernel(v_ref, box_ref, cnt_ref, pw_ref, ps_ref, o_ref, pre_ref, diff_ref):
    gi = pl.program_id(1)
    rows = SEQ // GRID_W
    gw = GRID_W
    box = box_ref[...]
    pre_ref[0:gw, :] = jnp.zeros((gw, POOL_GROUP), F32)
    for p in range(SEQ // POOL_PIECE):
        yc = _dot(box, v_ref[p * POOL_PIECE:(p + 1) * POOL_PIECE, :])
        for rr in range(POOL_PIECE // gw):
            r = p * (POOL_PIECE // gw) + rr
            pre_ref[(r + 1) * gw:(r + 2) * gw, :] = pre_ref[r * gw:(r + 1) * gw, :] + yc[rr * gw:(rr + 1) * gw]
    for k, w in enumerate(POOL_WINDOWS):
        @pl.when(gi == k)
        def _(w=w):
            r0, r1 = _box_bounds(rows, w)
            for r in range(rows):
                sl = slice(r * gw, (r + 1) * gw)
                box_sum = pre_ref[int(r1[r]) * gw:(int(r1[r]) + 1) * gw, :] - pre_ref[int(r0[r]) * gw:(int(r0[r]) + 1) * gw, :]
                diff_ref[sl, :] = (box_sum / cnt_ref[sl, :] - v_ref[sl, :].astype(F32)).astype(BF16)
    o_ref[...] = (_dot(diff_ref[...], pw_ref[...].astype(BF16)) * ps_ref[...]).astype(BF16)


def _grid_pool(rest, pool_w, pool_scale):
    ng, pg = len(POOL_WINDOWS), POOL_GROUP
    col_box, cnt = _pool_consts()
    v_off = 3 * HG_WIDTH // pg
    vmem = (2 * (2 * SEQ * pg * 2 + POOL_PIECE * POOL_PIECE * 2 + SEQ * 128 * 4 + pg * pg * 4)
            + (SEQ + GRID_W) * pg * 4 + SEQ * pg * 2 + SEQ * pg * 4)
    return pl.pallas_call(
        _pool_kernel,
        grid=(BATCH, ng),
        in_specs=[pl.BlockSpec((SEQ, pg), lambda b, k: (b, v_off + k)),
                  pl.BlockSpec((None, POOL_PIECE, POOL_PIECE), lambda b, k: (k, 0, 0)),
                  pl.BlockSpec((None, SEQ, 1), lambda b, k: (k, 0, 0)),
                  pl.BlockSpec((None, pg, pg), lambda b, k: (k, 0, 0)),
                  pl.BlockSpec((1, pg), lambda b, k: (0, k))],
        out_specs=pl.BlockSpec((SEQ, pg), lambda b, k: (b, k)),
        out_shape=jax.ShapeDtypeStruct((BATCH * SEQ, POOL_WIDTH), BF16),
        scratch_shapes=[pltpu.VMEM((SEQ + GRID_W, pg), F32), pltpu.VMEM((SEQ, pg), BF16)],
        compiler_params=_params(vmem, ("arbitrary", "arbitrary")),
        name="grid_pool",
    )(rest, col_box, cnt, pool_w, pool_scale)


def _out_proj_kernel(o_ref, pm_ref, wa_ref, wb_ref, x_ref, g1_ref, out_ref, wa_scr, wb_scr):
    @pl.when(pl.program_id(1) == 0)
    def _():
        wa_scr[...] = wa_ref[...].astype(BF16)
        wb_scr[...] = wb_ref[...].astype(BF16)

    acc = _dot(o_ref[...], wa_scr[...]) + _dot(pm_ref[...], wb_scr[...])
    out_ref[...] = x_ref[...] + g1_ref[...] * acc


def _out_proj(o, pm, w, x2d, g1):
    n, d = x2d.shape
    tm, tn = 1024, 512
    half = w.shape[0] // 2
    tiles_per_sample = SEQ // tm
    vmem = 2 * (2 * tm * half * 2 + 2 * half * tn * 4 + 2 * tm * tn * 4) + 2 * half * tn * 2 + tm * tn * 4
    return pl.pallas_call(
        _out_proj_kernel,
        grid=(d // tn, n // tm),
        in_specs=[pl.BlockSpec((tm, half), lambda j, i: (i, 0)),
                  pl.BlockSpec((tm, half), lambda j, i: (i, 0)),
                  pl.BlockSpec((half, tn), lambda j, i: (0, j)),
                  pl.BlockSpec((half, tn), lambda j, i: (1, j)),
                  pl.BlockSpec((tm, tn), lambda j, i: (i, j)),
                  pl.BlockSpec((None, 1, tn), lambda j, i: (i // tiles_per_sample, 0, j))],
        out_specs=pl.BlockSpec((tm, tn), lambda j, i: (i, j)),
        out_shape=jax.ShapeDtypeStruct((n, d), F32),
        scratch_shapes=[pltpu.VMEM((half, tn), BF16), pltpu.VMEM((half, tn), BF16)],
        compiler_params=_params(vmem, ("arbitrary", "arbitrary")),
        name="out_proj",
    )(o, pm, w, w, x2d, g1)


def _norm2_router_kernel(x_ref, g_ref, sc_ref, sh_ref, rw_ref, h2_ref, aff_ref):
    h = _modulated_norm(x_ref[...], g_ref[...], sc_ref[...], sh_ref[...])
    h2_ref[...] = h
    h_hi, h_lo = _split_f32(h)
    r_hi, r_lo = _split_f32(rw_ref[...])
    h_hi, h_lo, r_hi, r_lo = (a.astype(BF16) for a in (h_hi, h_lo, r_hi, r_lo))
    logits = _dot_nt(r_hi, h_hi) + (_dot_nt(r_hi, h_lo) + _dot_nt(r_lo, h_hi))
    e = jnp.exp(logits - jnp.max(logits, axis=0, keepdims=True))
    aff_ref[...] = e / jnp.sum(e, axis=0, keepdims=True)


def _norm2_router(x1, g, sc, sh, router_wt):
    n, d = x1.shape
    tm = 256
    tiles_per_sample = SEQ // tm
    vmem = 2 * (2 * tm * d * 4 + N_EXPERTS * d * 4) + 6 * tm * d * 4
    return pl.pallas_call(
        _norm2_router_kernel,
        grid=(n // tm,),
        in_specs=[pl.BlockSpec((tm, d), lambda i: (i, 0)),
                  pl.BlockSpec((1, d), lambda i: (0, 0)),
                  pl.BlockSpec((None, 1, d), lambda i: (i // tiles_per_sample, 0, 0)),
                  pl.BlockSpec((None, 1, d), lambda i: (i // tiles_per_sample, 0, 0)),
                  pl.BlockSpec((N_EXPERTS, d), lambda i: (0, 0))],
        out_specs=[pl.BlockSpec((tm, d), lambda i: (i, 0)),
                   pl.BlockSpec((None, N_EXPERTS, tm), lambda i: (i // tiles_per_sample, 0, i % tiles_per_sample))],
        out_shape=[jax.ShapeDtypeStruct((n, d), F32),
                   jax.ShapeDtypeStruct((BATCH, N_EXPERTS, SEQ), F32)],
        compiler_params=_params(vmem, ("arbitrary",)),
        name="norm2_router",
    )(x1, g, sc, sh, router_wt)


ROUTE_COLS = 256
ROUTE_ROWS = 32
ROUTE_BITS = 3
TOKEN_DIGIT_BITS = 6
TOKEN_DIGIT = 1 << TOKEN_DIGIT_BITS


def _route_kernel(aff_ref, slot_ref, gate_ref, idx_ref):
    a = aff_ref[...]
    rows, n = a.shape
    thr = jnp.zeros((rows, 1), I32)
    for top in range(30, -1, -ROUTE_BITS):
        low = max(top - ROUTE_BITS + 1, 0)
        best = thr
        for pattern in range(1, 1 << (top - low + 1)):
            cand = thr | (pattern << low)
            cnt = jnp.sum((a >= pltpu.bitcast(cand, F32)).astype(I32), axis=-1, keepdims=True)
            best = jnp.where(cnt >= CAPACITY, cand, best)
        thr = best
    gt = a >= pltpu.bitcast(thr + 1, F32)
    eq = jnp.logical_and(a >= pltpu.bitcast(thr, F32), jnp.logical_not(gt))
    need = CAPACITY - jnp.sum(gt.astype(I32), axis=-1, keepdims=True)

    def prefix_count(mask):
        m = jnp.where(mask, 1.0, 0.0).astype(BF16)
        r = lax.broadcasted_iota(I32, (n, ROUTE_COLS), 0)
        c = lax.broadcasted_iota(I32, (n, ROUTE_COLS), 1)
        parts = [_dot(m, jnp.where(r < c + cb * ROUTE_COLS, 1.0, 0.0).astype(BF16))
                 for cb in range(n // ROUTE_COLS)]
        return jnp.concatenate(parts, axis=1).astype(I32)

    sel = jnp.logical_or(gt, jnp.logical_and(eq, prefix_count(eq) < need))
    slot = jnp.where(sel, prefix_count(sel), -1)
    slot_ref[...] = slot

    a_hi, a_rest = _split_f32(a)
    a_mid, a_lo = _split_f32(a_rest)
    tok = lax.broadcasted_iota(I32, (1, n), 1)
    tok_hi = (tok >> TOKEN_DIGIT_BITS).astype(F32)
    tok_lo = (tok & (TOKEN_DIGIT - 1)).astype(F32)
    slot_ids = lax.broadcasted_iota(I32, (CAPACITY, n), 0)
    pad = jnp.zeros((ROUTE_ROWS - 5, n), F32)
    for r in range(rows):
        onehot = jnp.where(slot[r:r + 1] == slot_ids, 1.0, 0.0).astype(BF16)
        pieces = jnp.concatenate([a_hi[r:r + 1], a_mid[r:r + 1], a_lo[r:r + 1], tok_hi, tok_lo, pad],
                                 axis=0).astype(BF16)
        res = _dot_nt(pieces, onehot)
        gate_ref[r:r + 1, :] = res[0:1] + res[1:2] + res[2:3]
        idx_ref[r:r + 1, :] = (res[3:4] * TOKEN_DIGIT + res[4:5]).astype(I32)


def _route(aff_rows):
    rows, n = aff_rows.shape
    vmem = 4 * ROUTE_ROWS * n * 4 + 4 * n * ROUTE_COLS * 4 + 4 * CAPACITY * n * 4
    return pl.pallas_call(
        _route_kernel,
        grid=(rows // ROUTE_ROWS,),
        in_specs=[pl.BlockSpec((ROUTE_ROWS, n), lambda i: (i, 0))],
        out_specs=[pl.BlockSpec((ROUTE_ROWS, n), lambda i: (i, 0)),
                   pl.BlockSpec((ROUTE_ROWS, CAPACITY), lambda i: (i, 0)),
                   pl.BlockSpec((ROUTE_ROWS, CAPACITY), lambda i: (i, 0))],
        out_shape=[jax.ShapeDtypeStruct((rows, n), I32),
                   jax.ShapeDtypeStruct((rows, CAPACITY), F32),
                   jax.ShapeDtypeStruct((rows, CAPACITY), I32)],
        compiler_params=_params(vmem, ("arbitrary",)),
        name="route",
    )(aff_rows)


CAST_ROWS = 64


def _moe_up_kernel(idx_ref, h2_hbm, w1_ref, w3_ref, hid_ref, rows32, rows16, sem):
    e, f = pl.program_id(0), pl.program_id(1)
    n_experts = pl.num_programs(0)
    n_ff, share, _ = rows32.shape
    m = n_ff * share

    def row_copy(src_row, group, r):
        return pltpu.make_async_copy(h2_hbm.at[pl.ds(src_row, 1), :], rows32.at[group, pl.ds(r, 1), :], sem)

    def start_group(expert, group):
        first = expert * m + group * share
        for r in range(share):
            row_copy(idx_ref[first + r], group, r).start()

    @pl.when(jnp.logical_and(e == 0, f == 0))
    def _():
        lax.fori_loop(0, n_ff, lambda grp, carry: (start_group(0, grp), carry)[1], 0)

    @pl.when(f == 0)
    def _():
        def wait_group(grp, carry):
            for r in range(share):
                row_copy(0, grp, r).wait()
            return carry
        lax.fori_loop(0, n_ff, wait_group, 0)

        def cast_group(grp, carry):
            for c0 in range(0, share, CAST_ROWS):
                dst = pl.ds(pl.multiple_of(grp * share + c0, CAST_ROWS), CAST_ROWS)
                rows16[dst, :] = rows32[grp, c0:c0 + CAST_ROWS, :].astype(BF16)
            return carry
        lax.fori_loop(0, n_ff, cast_group, 0)

    @pl.when(e + 1 < n_experts)
    def _():
        start_group(e + 1, f)

    xg = rows16[...]
    a = _dot(xg, w1_ref[...].astype(BF16))
    b = _dot(xg, w3_ref[...].astype(BF16))
    hid_ref[...] = (a * jax.nn.sigmoid(a) * b).astype(BF16)


def _moe_up(idx_table, h2, w1, w3):
    ne, m = idx_table.shape
    d = h2.shape[1]
    ff = w1.shape[2]
    tf = 256
    vmem = m * d * (4 + 2) + 2 * 2 * d * tf * 4 + 2 * d * tf * 2 + 3 * m * tf * 4 + 2 * m * tf * 2
    n_ff = ff // tf
    w_spec = pl.BlockSpec((None, d, tf), lambda e, f, idx: (e, 0, f))
    return pl.pallas_call(
        _moe_up_kernel,
        grid_spec=pltpu.PrefetchScalarGridSpec(
            num_scalar_prefetch=1,
            grid=(ne, n_ff),
            in_specs=[pl.BlockSpec(memory_space=pl.ANY), w_spec, w_spec],
            out_specs=pl.BlockSpec((None, m, tf), lambda e, f, idx: (e, 0, f)),
            scratch_shapes=[pltpu.VMEM((n_ff, m // n_ff, d), F32), pltpu.VMEM((m, d), BF16),
                            pltpu.SemaphoreType.DMA(())]),
        out_shape=jax.ShapeDtypeStruct((ne, m, ff), BF16),
        compiler_params=_params(vmem, ("arbitrary", "arbitrary")),
        name="moe_up",
    )(idx_table.reshape(ne * m), h2, w1, w3)


def _moe_down_kernel(hid_ref, w2_ref, gate_ref, y_ref):
    y = _dot(hid_ref[...], w2_ref[...].astype(BF16))
    y_ref[...] = (y * gate_ref[...]).astype(BF16)


def _moe_down(hid, w2, gate_col):
    ne, m, ff = hid.shape
    d = w2.shape[2]
    tn = 1024
    vmem = 2 * m * ff * 2 + 2 * ff * tn * 4 + ff * tn * 2 + 2 * m * 128 * 4 + 2 * m * tn * 4 + 2 * m * tn * 2
    return pl.pallas_call(
        _moe_down_kernel,
        grid=(ne, d // tn),
        in_specs=[pl.BlockSpec((None, m, ff), lambda e, j: (e, 0, 0)),
                  pl.BlockSpec((None, ff, tn), lambda e, j: (e, 0, j)),
                  pl.BlockSpec((None, m, 1), lambda e, j: (e, 0, 0))],
        out_specs=pl.BlockSpec((None, m, tn), lambda e, j: (e, 0, j)),
        out_shape=jax.ShapeDtypeStruct((ne, m, d), BF16),
        compiler_params=_params(vmem, ("arbitrary", "arbitrary")),
        name="moe_down",
    )(hid, w2, gate_col)


COMBINE_TOKENS = 256


def _moe_combine_kernel(slot_ref, y_ref, x1_ref, g2_ref, fg_ref, out_ref):
    st = slot_ref[...]
    slot_ids = lax.broadcasted_iota(I32, (COMBINE_TOKENS, CAPACITY), 1)
    onehot = jnp.concatenate(
        [jnp.where(st[:, e:e + 1] == slot_ids, 1.0, 0.0).astype(BF16) for e in range(N_EXPERTS)], axis=1)
    y = y_ref[...].reshape(N_EXPERTS * CAPACITY, y_ref.shape[-1])
    x2 = x1_ref[...] + g2_ref[...] * _dot(onehot, y)
    ms = jnp.mean(x2 * x2, axis=-1, keepdims=True)
    out_ref[...] = x2 * lax.rsqrt(ms + EPS) * fg_ref[...]


def _moe_combine(slot_ble, y, x1, g2, final_g):
    n, d = x1.shape
    tm = COMBINE_TOKENS
    tiles_per_sample = SEQ // tm
    vmem = (N_EXPERTS * CAPACITY * d * 2 + 2 * 2 * tm * d * 4 + 2 * tm * 128 * 4
            + tm * N_EXPERTS * CAPACITY * 2 + 3 * tm * d * 4)
    return pl.pallas_call(
        _moe_combine_kernel,
        grid=(BATCH, tiles_per_sample),
        in_specs=[pl.BlockSpec((None, tm, N_EXPERTS), lambda b, t: (b, t, 0)),
                  pl.BlockSpec((N_EXPERTS, None, CAPACITY, d), lambda b, t: (0, b, 0, 0),
                               pipeline_mode=pl.Buffered(1)),
                  pl.BlockSpec((tm, d), lambda b, t: (b * tiles_per_sample + t, 0)),
                  pl.BlockSpec((None, 1, d), lambda b, t: (b, 0, 0)),
                  pl.BlockSpec((1, d), lambda b, t: (0, 0))],
        out_specs=pl.BlockSpec((tm, d), lambda b, t: (b * tiles_per_sample + t, 0)),
        out_shape=jax.ShapeDtypeStruct((n, d), F32),
        compiler_params=_params(vmem, ("arbitrary", "arbitrary")),
        name="moe_combine",
    )(slot_ble, y, x1, g2, final_g)


def kernel(x, c, ctx, c_ctx, ada_w, ada_b, norm1_g, norm2_g, w_in, lb_param, hg_norm_g, pool_w, pool_scale,
           w_out, router_w, moe_w1, moe_w3, moe_w2, final_norm_g):
    nb, seq, d = x.shape
    assert (nb, seq, d) == (BATCH, SEQ, D_MODEL) and ctx.shape[1] == CTX_LEN
    assert ada_w.shape[0] == 1 and lb_param.shape[0] == 2, "single-layer block: layer 0 uses lower-bound row 0"
    x2d = x.reshape(nb * seq, d)
    ctx2d = ctx.reshape(nb * CTX_LEN, d)

    cvecs = jnp.zeros((ADA_ROWS, d), F32).at[:nb].set(c).at[nb].set(c_ctx)
    mod = _ada_mod(cvecs, ada_w[0], ada_b).reshape(ADA_ROWS, 6, d)
    sh1, sc1, g1, sh2, sc2, g2 = (mod[:nb, k][:, None, :] for k in range(6))
    csh1, csc1 = mod[nb:nb + 1, 0][:, None, :], mod[nb:nb + 1, 1][:, None, :]

    lbp = lb_param.reshape(lb_param.shape[0], 2 * HG_WIDTH)

    h_c = _norm1(ctx2d, norm1_g, csc1, csh1, rows_per_sample=nb * CTX_LEN)
    f_c, k_c, i_c = _in_proj(h_c, lbp, w_in[0], n_cols=3 * HG_WIDTH)
    s_f, s_b = _ctx_states(f_c, k_c, i_c)

    h_x = _norm1(x2d, norm1_g, sc1, sh1, rows_per_sample=seq)
    fdec, kk, rest = _in_proj(h_x, lbp, w_in[0], n_cols=w_in.shape[2])
    o = _hgrn2_scan(fdec, kk, rest, s_f, s_b, hg_norm_g)
    pm = _grid_pool(rest, pool_w[0], pool_scale)
    x1 = _out_proj(o, pm, w_out[0], x2d, g1)

    h2, aff_t = _norm2_router(x1, norm2_g, sc2, sh2, router_w[0].T)
    slot, gates, tok = _route(aff_t.reshape(nb * N_EXPERTS, seq))
    slot = slot.reshape(nb, N_EXPERTS, seq)
    h2_rows = tok.reshape(nb, N_EXPERTS, CAPACITY) + (jnp.arange(nb, dtype=I32) * seq)[:, None, None]
    hid = _moe_up(h2_rows.transpose(1, 0, 2).reshape(N_EXPERTS, nb * CAPACITY), h2, moe_w1[0], moe_w3[0])
    gate_col = gates.reshape(nb, N_EXPERTS, CAPACITY).transpose(1, 0, 2).reshape(N_EXPERTS, nb * CAPACITY, 1)
    y = _moe_down(hid, moe_w2[0], gate_col).reshape(N_EXPERTS, nb, CAPACITY, d)
    out = _moe_combine(slot.transpose(0, 2, 1), y, x1, g2, final_norm_g[None, :])
    return out.reshape(nb, seq, d).astype(x.dtype)
```

```python
import jax
import jax.numpy as jnp
import numpy as np
from jax import lax
from jax.experimental import pallas as pl
from jax.experimental.pallas import tpu as pltpu

F32 = jnp.float32
BF16 = jnp.bfloat16
I32 = jnp.int32

D_MODEL = 4096
BATCH = 4
SEQ = 2048
GRID_W = 64
CTX_LEN = 256
HG_HEADS = 16
HG_HEAD_DIM = 128
HG_WIDTH = HG_HEADS * HG_HEAD_DIM
POOL_WIDTH = D_MODEL - HG_WIDTH
POOL_WINDOWS = (2, 4, 8, 16)
POOL_GROUP = POOL_WIDTH // len(POOL_WINDOWS)
N_EXPERTS = 16
EXPERT_FF = D_MODEL // 2
CAPACITY = 2 * SEQ // N_EXPERTS
EPS = 1e-6

V7X_VMEM_BYTES = 64 * 1024 * 1024
V7X_VMEM_HEADROOM_BYTES = 10 * 1024 * 1024
V7X_SUBLANES = 8
V7X_LANES = 128
NORM_ROWS = 16
ADA_ROWS = 8

SCAN_CHUNK = 128
SCAN_LEVELS = 7
SCAN_UNROLL = 8

_NT = (((1,), (1,)), ((), ()))
_TN = (((0,), (0,)), ((), ()))


def _dot(a, b):
    return jnp.dot(a, b, preferred_element_type=F32)


def _dot_nt(a, b):
    return lax.dot_general(a, b, _NT, preferred_element_type=F32)


def _dot_tn(a, b):
    return lax.dot_general(a, b, _TN, preferred_element_type=F32)


def _split_f32(x):
    hi = x.astype(BF16).astype(F32)
    return hi, x - hi


def _params(vmem_bytes, semantics):
    limit = min(int(vmem_bytes) + V7X_VMEM_HEADROOM_BYTES, V7X_VMEM_BYTES)
    return pltpu.CompilerParams(dimension_semantics=semantics, vmem_limit_bytes=limit)


def _ada_kernel(c_ref, w_ref, b_ref, o_ref):
    c = c_ref[...]
    s = c * jax.nn.sigmoid(c)
    hi, lo = _split_f32(s)
    lhs = jnp.concatenate([hi, lo], axis=0).astype(BF16)
    r = _dot(lhs, w_ref[...].astype(BF16))
    o_ref[...] = r[:ADA_ROWS] + r[ADA_ROWS:] + b_ref[...]


def _ada_mod(cvecs, w, b):
    d, n = w.shape
    tn = 1024
    vmem = 2 * d * tn * 4 + d * tn * 2 + 4 * ADA_ROWS * d * 4
    return pl.pallas_call(
        _ada_kernel,
        grid=(n // tn,),
        in_specs=[pl.BlockSpec((ADA_ROWS, d), lambda j: (0, 0)),
                  pl.BlockSpec((d, tn), lambda j: (0, j)),
                  pl.BlockSpec((1, tn), lambda j: (0, j))],
        out_specs=pl.BlockSpec((ADA_ROWS, tn), lambda j: (0, j)),
        out_shape=jax.ShapeDtypeStruct((ADA_ROWS, n), F32),
        compiler_params=_params(vmem, ("arbitrary",)),
        name="ada_mod",
    )(cvecs, w, b)


def _modulated_norm(x, g, scale, shift):
    ms = jnp.mean(x * x, axis=-1, keepdims=True)
    return (x * lax.rsqrt(ms + EPS) * g) * (1.0 + scale) + shift


def _norm1_kernel(x_ref, g_ref, sc_ref, sh_ref, h_ref):
    g, sc, sh = g_ref[...], sc_ref[...], sh_ref[...]

    def norm_rows(r, carry):
        rows = pl.ds(pl.multiple_of(r * NORM_ROWS, NORM_ROWS), NORM_ROWS)
        h_ref[rows, :] = _modulated_norm(x_ref[rows, :], g, sc, sh).astype(BF16)
        return carry

    lax.fori_loop(0, x_ref.shape[0] // NORM_ROWS, norm_rows, 0, unroll=2)


def _norm1(x2d, g, sc, sh, *, rows_per_sample):
    n, d = x2d.shape
    tm = 512
    tiles_per_sample = rows_per_sample // tm
    mod = pl.BlockSpec((None, 1, d), lambda i: (i // tiles_per_sample, 0, 0))
    vmem = 2 * tm * d * (4 + 2) + 8 * NORM_ROWS * d * 4
    return pl.pallas_call(
        _norm1_kernel,
        grid=(n // tm,),
        in_specs=[pl.BlockSpec((tm, d), lambda i: (i, 0)), pl.BlockSpec((1, d), lambda i: (0, 0)), mod, mod],
        out_specs=pl.BlockSpec((tm, d), lambda i: (i, 0)),
        out_shape=jax.ShapeDtypeStruct((n, d), BF16),
        compiler_params=_params(vmem, ("arbitrary",)),
        name="norm1",
    )(x2d, g, sc, sh)


def _gate_proj_kernel(h_ref, lbp_ref, w_ref, f_ref, k_ref, w_scr):
    @pl.when(pl.program_id(1) == 0)
    def _():
        w_scr[...] = w_ref[...].astype(BF16)

    z = _dot(h_ref[...], w_scr[...])
    p = lbp_ref[...]
    e = jnp.exp(p - jnp.max(p, axis=0, keepdims=True))
    lb = e[0:1] / jnp.sum(e, axis=0, keepdims=True)
    f = lb + (1.0 - lb) * jax.nn.sigmoid(z)
    f_ref[...] = f
    k_ref[...] = (1.0 - f).astype(BF16)


def _rest_proj_kernel(h_ref, w_ref, r_ref, w_scr):
    @pl.when(pl.program_id(1) == 0)
    def _():
        w_scr[...] = w_ref[...].astype(BF16)

    r_ref[...] = _dot(h_ref[...], w_scr[...]).astype(BF16)


def _in_proj(h, lbp, w, *, n_cols):
    n, d = h.shape
    tm, tn = 1024, 512
    n_gate = 2 * HG_WIDTH // tn
    w_bytes = 2 * d * tn * 4 + d * tn * 2
    vmem = 2 * tm * d * 2 + w_bytes + 2 * tm * tn * (4 + 2) + 4 * tm * tn * 4
    fdec, kk = pl.pallas_call(
        _gate_proj_kernel,
        grid=(n_gate, n // tm),
        in_specs=[pl.BlockSpec((tm, d), lambda j, i: (i, 0)),
                  pl.BlockSpec((2, tn), lambda j, i: (0, j)),
                  pl.BlockSpec((d, tn), lambda j, i: (0, j))],
        out_specs=[pl.BlockSpec((tm, tn), lambda j, i: (i, j)),
                   pl.BlockSpec((tm, tn), lambda j, i: (i, j))],
        out_shape=[jax.ShapeDtypeStruct((n, 2 * HG_WIDTH), F32),
                   jax.ShapeDtypeStruct((n, 2 * HG_WIDTH), BF16)],
        scratch_shapes=[pltpu.VMEM((d, tn), BF16)],
        compiler_params=_params(vmem, ("arbitrary", "arbitrary")),
        name="gate_proj",
    )(h, lbp, w)
    vmem = 2 * tm * d * 2 + w_bytes + 2 * tm * tn * 2 + 2 * tm * tn * 4
    rest = pl.pallas_call(
        _rest_proj_kernel,
        grid=((n_cols - 2 * HG_WIDTH) // tn, n // tm),
        in_specs=[pl.BlockSpec((tm, d), lambda j, i: (i, 0)),
                  pl.BlockSpec((d, tn), lambda j, i: (0, n_gate + j))],
        out_specs=pl.BlockSpec((tm, tn), lambda j, i: (i, j)),
        out_shape=jax.ShapeDtypeStruct((n, n_cols - 2 * HG_WIDTH), BF16),
        scratch_shapes=[pltpu.VMEM((d, tn), BF16)],
        compiler_params=_params(vmem, ("arbitrary", "arbitrary")),
        name="rest_proj",
    )(h, w)
    return fdec, kk, rest


def _ctx_tri():
    s = np.arange(CTX_LEN)[:, None]
    u = np.arange(CTX_LEN)[None, :]
    tri = np.stack([u > s, u < s]).astype(np.float32)
    return jnp.asarray(np.concatenate([tri, tri], axis=2), dtype=BF16)


CTX_HEADS = 8


def _ctx_state_kernel(ff_ref, fb_ref, kf_ref, kb_ref, i_ref, tri_ref, sf_ref, sb_ref):
    dh = HG_HEAD_DIM

    def state(f, k, v, a):
        hi, lo = _split_f32(jnp.log(f))
        g = _dot(a, jnp.concatenate([hi, lo], axis=0).astype(BF16))
        kd = (k.astype(F32) * jnp.exp(g)).astype(BF16)
        return _dot_tn(v, kd)

    for j in range(CTX_HEADS):
        cols = slice(j * dh, (j + 1) * dh)
        v = i_ref[:, cols]
        sf_ref[j] = state(ff_ref[:, cols], kf_ref[:, cols], v, tri_ref[0])
        sb_ref[j] = state(fb_ref[:, cols], kb_ref[:, cols], v, tri_ref[1])


def _ctx_states(fdec, kk, vi):
    t, dh, nh = CTX_LEN, HG_HEAD_DIM, HG_HEADS
    groups = nh // CTX_HEADS
    blk = lambda off: pl.BlockSpec((t, CTX_HEADS * dh), lambda b, h: (b, off + h))
    st = pl.BlockSpec((None, CTX_HEADS, dh, dh), lambda b, h: (b, h, 0, 0))
    shape = jax.ShapeDtypeStruct((BATCH, nh, dh, dh), F32)
    vmem = CTX_HEADS * 2 * (2 * t * dh * 4 + 3 * t * dh * 2 + 2 * dh * dh * 4) + 2 * 2 * t * 2 * t * 2
    return pl.pallas_call(
        _ctx_state_kernel,
        grid=(BATCH, groups),
        in_specs=[blk(0), blk(groups), blk(0), blk(groups), blk(0),
                  pl.BlockSpec((2, t, 2 * t), lambda b, h: (0, 0, 0))],
        out_specs=[st, st],
        out_shape=[shape, shape],
        compiler_params=_params(vmem, ("arbitrary", "arbitrary")),
        name="ctx_state",
    )(fdec, fdec, kk, kk, vi, _ctx_tri())


def _pair_levels():
    c, nl = SCAN_CHUNK, SCAN_LEVELS
    t = np.arange(c)[:, None]
    u = np.arange(c)[None, :]
    lev = np.floor(np.log2(np.maximum(t ^ u, 1))).astype(np.int32)
    lidx = np.stack([np.where(t > u, lev, np.where(t == u, nl, -1)),
                     np.where(t < u, lev, np.where(t == u, nl, -1))]).astype(np.int32)
    return jnp.asarray(lidx)


def _swap_halves(x, h, t_idx):
    c = x.shape[0]
    assert 2 * h <= V7X_SUBLANES
    x3 = x.reshape(c // V7X_SUBLANES, V7X_SUBLANES, x.shape[1])
    if 2 * h == V7X_SUBLANES:
        return pltpu.roll(x3, h, axis=1).reshape(x.shape)
    up = pltpu.roll(x3, h, axis=1).reshape(x.shape)
    down = pltpu.roll(x3, V7X_SUBLANES - h, axis=1).reshape(x.shape)
    return jnp.where((t_idx & h) != 0, up, down)


def _scan_chunk(direction, f, k, q, v, st_ref, lidx, t_idx):
    c, nl = SCAN_CHUNK, SCAN_LEVELS
    qf = q.astype(F32)
    kf = k.astype(F32)
    q_dec = f
    k_dec = jnp.ones_like(f)
    total = f
    scores = jnp.zeros((c, c), F32)
    sublane = lax.broadcasted_iota(I32, (1, V7X_SUBLANES, f.shape[1]), 1)
    tiles = lambda a: a.reshape(c // V7X_SUBLANES, V7X_SUBLANES, a.shape[1])
    blocks = lambda a, h: [a[i * h:(i + 1) * h] for i in range(c // h)]
    for l in range(nl):
        h = 1 << l
        if h < V7X_SUBLANES:
            bit = (sublane & h) != 0
            is_query = bit if direction == 0 else jnp.logical_not(bit)
            x = jnp.where(is_query, tiles(q_dec * qf), tiles(k_dec * kf)).reshape(c, -1).astype(BF16)
            scores = jnp.where(lidx == l, _dot_nt(x, x), scores)
            other = _swap_halves(total, h, t_idx)
            q_dec = (tiles(q_dec) * jnp.where(is_query, tiles(other), 1.0)).reshape(c, -1)
            k_dec = (tiles(k_dec) * jnp.where(is_query, 1.0, tiles(other))).reshape(c, -1)
            total = total * other
        else:
            n_blocks = c // h
            is_query = [((i & 1) == 1) == (direction == 0) for i in range(n_blocks)]
            qd, kd, qb, kb, tb = (blocks(a, h) for a in (q_dec, k_dec, qf, kf, total))
            x = jnp.concatenate([qd[i] * qb[i] if is_query[i] else kd[i] * kb[i] for i in range(n_blocks)],
                                axis=0).astype(BF16)
            pb, sb, lb = (blocks(a, h) for a in (_dot_nt(x, x), scores, lidx))
            scores = jnp.concatenate([jnp.where(lb[i] == l, pb[i], sb[i]) if is_query[i] else sb[i]
                                      for i in range(n_blocks)], axis=0)
            ob = [tb[i ^ 1] for i in range(n_blocks)]
            q_dec = jnp.concatenate([qd[i] * ob[i] if is_query[i] else qd[i] for i in range(n_blocks)], axis=0)
            k_dec = jnp.concatenate([kd[i] if is_query[i] else kd[i] * ob[i] for i in range(n_blocks)], axis=0)
            total = jnp.concatenate([tb[i] * ob[i] for i in range(n_blocks)], axis=0)
    own = jnp.sum(qf * kf, axis=-1, keepdims=True)
    q_in = (q_dec * qf).astype(BF16)
    k_st = (k_dec * kf).astype(BF16)
    st = st_ref[...]
    o = _dot(scores.astype(BF16), v) + _dot_nt(q_in, st.astype(BF16)) + own * v.astype(F32)
    st_ref[...] = total[0:1] * st + _dot_tn(v, k_st)
    return o


def _scan_kernel(ff_ref, fb_ref, kf_ref, kb_ref, i_ref, q_ref, g_ref, s0f_ref, s0b_ref, ng_ref,
                 lidx_ref, o_ref, stf_ref, stb_ref, part_ref):
    c = SCAN_CHUNK
    n_chunks = SEQ // c
    stf_ref[...] = s0f_ref[...]
    stb_ref[...] = s0b_ref[...]
    t_idx = lax.broadcasted_iota(I32, (c, HG_HEAD_DIM), 0)
    ng = ng_ref[...]

    def chunk(direction, r0):
        rows = pl.ds(r0, c)
        f_ref, k_ref, st_ref = (ff_ref, kf_ref, stf_ref) if direction == 0 else (fb_ref, kb_ref, stb_ref)
        return _scan_chunk(direction, f_ref[rows, :], k_ref[rows, :], q_ref[rows, :], i_ref[rows, :],
                           st_ref, lidx_ref[direction], t_idx)

    def finalize(o, r0):
        rows = pl.ds(r0, c)
        gate = g_ref[rows, :].astype(F32)
        ms = jnp.mean(o * o, axis=-1, keepdims=True)
        o_ref[rows, :] = (o * lax.rsqrt(ms + EPS) * ng * (gate * jax.nn.sigmoid(gate))).astype(BF16)

    def starts(n):
        return pl.multiple_of(n * c, c), pl.multiple_of((n_chunks - 1 - n) * c, c)

    def first_half(n, carry):
        rf, rb = starts(n)
        part_ref[pl.ds(rf, c), :] = chunk(0, rf)
        part_ref[pl.ds(rb, c), :] = chunk(1, rb)
        return carry

    def second_half(n, carry):
        rf, rb = starts(n)
        finalize(chunk(0, rf) + part_ref[pl.ds(rf, c), :], rf)
        finalize(chunk(1, rb) + part_ref[pl.ds(rb, c), :], rb)
        return carry

    lax.fori_loop(0, n_chunks // 2, first_half, 0, unroll=SCAN_UNROLL)
    lax.fori_loop(n_chunks // 2, n_chunks, second_half, 0, unroll=SCAN_UNROLL)


def _hgrn2_scan(fdec, kk, rest, s0f, s0b, norm_g):
    dh, nh = HG_HEAD_DIM, HG_HEADS
    lidx = _pair_levels()
    blk = lambda off: pl.BlockSpec((SEQ, dh), lambda b, h: (b, off + h))
    st = pl.BlockSpec((None, None, dh, dh), lambda b, h: (b, h, 0, 0))
    vmem = (2 * (2 * SEQ * dh * 4 + 6 * SEQ * dh * 2 + 2 * dh * dh * 4) + SEQ * dh * 4 + 2 * dh * dh * 4
            + 2 * 2 * SCAN_CHUNK * V7X_LANES * 4)
    return pl.pallas_call(
        _scan_kernel,
        grid=(BATCH, nh),
        in_specs=[blk(0), blk(nh), blk(0), blk(nh), blk(0), blk(nh), blk(2 * nh), st, st,
                  pl.BlockSpec((1, dh), lambda b, h: (0, h)),
                  pl.BlockSpec(lidx.shape, lambda b, h: (0, 0, 0))],
        out_specs=blk(0),
        out_shape=jax.ShapeDtypeStruct((BATCH * SEQ, HG_WIDTH), BF16),
        scratch_shapes=[pltpu.VMEM((dh, dh), F32), pltpu.VMEM((dh, dh), F32), pltpu.VMEM((SEQ, dh), F32)],
        compiler_params=_params(vmem, ("arbitrary", "arbitrary")),
        name="hgrn2_scan",
    )(fdec, fdec, kk, kk, rest, rest, rest, s0f, s0b, norm_g, lidx)


POOL_PIECE = 256


def _box_bounds(n, w):
    start = np.arange(n) - w // 2
    return np.clip(start, 0, n), np.clip(start + w, 0, n)


def _pool_consts():
    rows = SEQ // GRID_W
    col_box = np.zeros((len(POOL_WINDOWS), POOL_PIECE, POOL_PIECE), np.float32)
    cnt = np.zeros((len(POOL_WINDOWS), SEQ, 1), np.float32)
    cc = np.arange(GRID_W)[None, :]
    for gi, w in enumerate(POOL_WINDOWS):
        c0, c1 = _box_bounds(GRID_W, w)
        r0, r1 = _box_bounds(rows, w)
        wc = ((cc >= c0[:, None]) & (cc < c1[:, None])).astype(np.float32)
        col_box[gi] = np.kron(np.eye(POOL_PIECE // GRID_W, dtype=np.float32), wc)
        cnt[gi] = ((r1 - r0)[:, None] * (c1 - c0)[None, :]).reshape(SEQ, 1)
    return jnp.asarray(col_box, dtype=BF16), jnp.asarray(cnt)


def _pool_kernel(v_ref, box_ref, cnt_ref, pw_ref, ps_ref, o_ref, pre_ref, diff_ref):
    gi = pl.program_id(1)
    rows = SEQ // GRID_W
    gw = GRID_W
    box = box_ref[...]
    pre_ref[0:gw, :] = jnp.zeros((gw, POOL_GROUP), F32)
    for p in range(SEQ // POOL_PIECE):
        yc = _dot(box, v_ref[p * POOL_PIECE:(p + 1) * POOL_PIECE, :])
        for rr in range(POOL_PIECE // gw):
            r = p * (POOL_PIECE // gw) + rr
            pre_ref[(r + 1) * gw:(r + 2) * gw, :] = pre_ref[r * gw:(r + 1) * gw, :] + yc[rr * gw:(rr + 1) * gw]
    for k, w in enumerate(POOL_WINDOWS):
        @pl.when(gi == k)
        def _(w=w):
            r0, r1 = _box_bounds(rows, w)
            for r in range(rows):
                sl = slice(r * gw, (r + 1) * gw)
                box_sum = pre_ref[int(r1[r]) * gw:(int(r1[r]) + 1) * gw, :] - pre_ref[int(r0[r]) * gw:(int(r0[r]) + 1) * gw, :]
                diff_ref[sl, :] = (box_sum / cnt_ref[sl, :] - v_ref[sl, :].astype(F32)).astype(BF16)
    o_ref[...] = (_dot(diff_ref[...], pw_ref[...].astype(BF16)) * ps_ref[...]).astype(BF16)


def _grid_pool(rest, pool_w, pool_scale):
    ng, pg = len(POOL_WINDOWS), POOL_GROUP
    col_box, cnt = _pool_consts()
    v_off = 3 * HG_WIDTH // pg
    vmem = (2 * (2 * SEQ * pg * 2 + POOL_PIECE * POOL_PIECE * 2 + SEQ * 128 * 4 + pg * pg * 4)
            + (SEQ + GRID_W) * pg * 4 + SEQ * pg * 2 + SEQ * pg * 4)
    return pl.pallas_call(
        _pool_kernel,
        grid=(BATCH, ng),
        in_specs=[pl.BlockSpec((SEQ, pg), lambda b, k: (b, v_off + k)),
                  pl.BlockSpec((None, POOL_PIECE, POOL_PIECE), lambda b, k: (k, 0, 0)),
                  pl.BlockSpec((None, SEQ, 1), lambda b, k: (k, 0, 0)),
                  pl.BlockSpec((None, pg, pg), lambda b, k: (k, 0, 0)),
                  pl.BlockSpec((1, pg), lambda b, k: (0, k))],
        out_specs=pl.BlockSpec((SEQ, pg), lambda b, k: (b, k)),
        out_shape=jax.ShapeDtypeStruct((BATCH * SEQ, POOL_WIDTH), BF16),
        scratch_shapes=[pltpu.VMEM((SEQ + GRID_W, pg), F32), pltpu.VMEM((SEQ, pg), BF16)],
        compiler_params=_params(vmem, ("arbitrary", "arbitrary")),
        name="grid_pool",
    )(rest, col_box, cnt, pool_w, pool_scale)


def _out_proj_kernel(o_ref, pm_ref, wa_ref, wb_ref, x_ref, g1_ref, out_ref, wa_scr, wb_scr):
    @pl.when(pl.program_id(1) == 0)
    def _():
        wa_scr[...] = wa_ref[...].astype(BF16)
        wb_scr[...] = wb_ref[...].astype(BF16)

    acc = _dot(o_ref[...], wa_scr[...]) + _dot(pm_ref[...], wb_scr[...])
    out_ref[...] = x_ref[...] + g1_ref[...] * acc


def _out_proj(o, pm, w, x2d, g1):
    n, d = x2d.shape
    tm, tn = 1024, 512
    half = w.shape[0] // 2
    tiles_per_sample = SEQ // tm
    vmem = 2 * (2 * tm * half * 2 + 2 * half * tn * 4 + 2 * tm * tn * 4) + 2 * half * tn * 2 + tm * tn * 4
    return pl.pallas_call(
        _out_proj_kernel,
        grid=(d // tn, n // tm),
        in_specs=[pl.BlockSpec((tm, half), lambda j, i: (i, 0)),
                  pl.BlockSpec((tm, half), lambda j, i: (i, 0)),
                  pl.BlockSpec((half, tn), lambda j, i: (0, j)),
                  pl.BlockSpec((half, tn), lambda j, i: (1, j)),
                  pl.BlockSpec((tm, tn), lambda j, i: (i, j)),
                  pl.BlockSpec((None, 1, tn), lambda j, i: (i // tiles_per_sample, 0, j))],
        out_specs=pl.BlockSpec((tm, tn), lambda j, i: (i, j)),
        out_shape=jax.ShapeDtypeStruct((n, d), F32),
        scratch_shapes=[pltpu.VMEM((half, tn), BF16), pltpu.VMEM((half, tn), BF16)],
        compiler_params=_params(vmem, ("arbitrary", "arbitrary")),
        name="out_proj",
    )(o, pm, w, w, x2d, g1)


def _norm2_router_kernel(x_ref, g_ref, sc_ref, sh_ref, rw_ref, h2_ref, aff_ref):
    h = _modulated_norm(x_ref[...], g_ref[...], sc_ref[...], sh_ref[...])
    h2_ref[...] = h
    h_hi, h_lo = _split_f32(h)
    r_hi, r_lo = _split_f32(rw_ref[...])
    h_hi, h_lo, r_hi, r_lo = (a.astype(BF16) for a in (h_hi, h_lo, r_hi, r_lo))
    logits = _dot_nt(r_hi, h_hi) + (_dot_nt(r_hi, h_lo) + _dot_nt(r_lo, h_hi))
    e = jnp.exp(logits - jnp.max(logits, axis=0, keepdims=True))
    aff_ref[...] = e / jnp.sum(e, axis=0, keepdims=True)


def _norm2_router(x1, g, sc, sh, router_wt):
    n, d = x1.shape
    tm = 256
    tiles_per_sample = SEQ // tm
    vmem = 2 * (2 * tm * d * 4 + N_EXPERTS * d * 4) + 6 * tm * d * 4
    return pl.pallas_call(
        _norm2_router_kernel,
        grid=(n // tm,),
        in_specs=[pl.BlockSpec((tm, d), lambda i: (i, 0)),
                  pl.BlockSpec((1, d), lambda i: (0, 0)),
                  pl.BlockSpec((None, 1, d), lambda i: (i // tiles_per_sample, 0, 0)),
                  pl.BlockSpec((None, 1, d), lambda i: (i // tiles_per_sample, 0, 0)),
                  pl.BlockSpec((N_EXPERTS, d), lambda i: (0, 0))],
        out_specs=[pl.BlockSpec((tm, d), lambda i: (i, 0)),
                   pl.BlockSpec((None, N_EXPERTS, tm), lambda i: (i // tiles_per_sample, 0, i % tiles_per_sample))],
        out_shape=[jax.ShapeDtypeStruct((n, d), F32),
                   jax.ShapeDtypeStruct((BATCH, N_EXPERTS, SEQ), F32)],
        compiler_params=_params(vmem, ("arbitrary",)),
        name="norm2_router",
    )(x1, g, sc, sh, router_wt)


ROUTE_COLS = 256
ROUTE_ROWS = 32
ROUTE_BITS = 3
TOKEN_DIGIT_BITS = 6
TOKEN_DIGIT = 1 << TOKEN_DIGIT_BITS


def _route_kernel(aff_ref, slot_ref, gate_ref, idx_ref):
    a = aff_ref[...]
    rows, n = a.shape
    thr = jnp.zeros((rows, 1), I32)
    for top in range(30, -1, -ROUTE_BITS):
        low = max(top - ROUTE_BITS + 1, 0)
        best = thr
        for pattern in range(1, 1 << (top - low + 1)):
            cand = thr | (pattern << low)
            cnt = jnp.sum((a >= pltpu.bitcast(cand, F32)).astype(I32), axis=-1, keepdims=True)
            best = jnp.where(cnt >= CAPACITY, cand, best)
        thr = best
    gt = a >= pltpu.bitcast(thr + 1, F32)
    eq = jnp.logical_and(a >= pltpu.bitcast(thr, F32), jnp.logical_not(gt))
    need = CAPACITY - jnp.sum(gt.astype(I32), axis=-1, keepdims=True)

    def prefix_count(mask):
        m = jnp.where(mask, 1.0, 0.0).astype(BF16)
        r = lax.broadcasted_iota(I32, (n, ROUTE_COLS), 0)
        c = lax.broadcasted_iota(I32, (n, ROUTE_COLS), 1)
        parts = [_dot(m, jnp.where(r < c + cb * ROUTE_COLS, 1.0, 0.0).astype(BF16))
                 for cb in range(n // ROUTE_COLS)]
        return jnp.concatenate(parts, axis=1).astype(I32)

    sel = jnp.logical_or(gt, jnp.logical_and(eq, prefix_count(eq) < need))
    slot = jnp.where(sel, prefix_count(sel), -1)
    slot_ref[...] = slot

    a_hi, a_rest = _split_f32(a)
    a_mid, a_lo = _split_f32(a_rest)
    tok = lax.broadcasted_iota(I32, (1, n), 1)
    tok_hi = (tok >> TOKEN_DIGIT_BITS).astype(F32)
    tok_lo = (tok & (TOKEN_DIGIT - 1)).astype(F32)
    slot_ids = lax.broadcasted_iota(I32, (CAPACITY, n), 0)
    pad = jnp.zeros((ROUTE_ROWS - 5, n), F32)
    for r in range(rows):
        onehot = jnp.where(slot[r:r + 1] == slot_ids, 1.0, 0.0).astype(BF16)
        pieces = jnp.concatenate([a_hi[r:r + 1], a_mid[r:r + 1], a_lo[r:r + 1], tok_hi, tok_lo, pad],
                                 axis=0).astype(BF16)
        res = _dot_nt(pieces, onehot)
        gate_ref[r:r + 1, :] = res[0:1] + res[1:2] + res[2:3]
        idx_ref[r:r + 1, :] = (res[3:4] * TOKEN_DIGIT + res[4:5]).astype(I32)


def _route(aff_rows):
    rows, n = aff_rows.shape
    vmem = 4 * ROUTE_ROWS * n * 4 + 4 * n * ROUTE_COLS * 4 + 4 * CAPACITY * n * 4
    return pl.pallas_call(
        _route_kernel,
        grid=(rows // ROUTE_ROWS,),
        in_specs=[pl.BlockSpec((ROUTE_ROWS, n), lambda i: (i, 0))],
        out_specs=[pl.BlockSpec((ROUTE_ROWS, n), lambda i: (i, 0)),
                   pl.BlockSpec((ROUTE_ROWS, CAPACITY), lambda i: (i, 0)),
                   pl.BlockSpec((ROUTE_ROWS, CAPACITY), lambda i: (i, 0))],
        out_shape=[jax.ShapeDtypeStruct((rows, n), I32),
                   jax.ShapeDtypeStruct((rows, CAPACITY), F32),
                   jax.ShapeDtypeStruct((rows, CAPACITY), I32)],
        compiler_params=_params(vmem, ("arbitrary",)),
        name="route",
    )(aff_rows)


CAST_ROWS = 64


def _moe_up_kernel(idx_ref, h2_hbm, w1_ref, w3_ref, hid_ref, rows32, rows16, sem):
    e, f = pl.program_id(0), pl.program_id(1)
    n_experts = pl.num_programs(0)
    n_ff, share, _ = rows32.shape
    m = n_ff * share

    def row_copy(src_row, group, r):
        return pltpu.make_async_copy(h2_hbm.at[pl.ds(src_row, 1), :], rows32.at[group, pl.ds(r, 1), :], sem)

    def start_group(expert, group):
        first = expert * m + group * share
        for r in range(share):
            row_copy(idx_ref[first + r], group, r).start()

    @pl.when(jnp.logical_and(e == 0, f == 0))
    def _():
        lax.fori_loop(0, n_ff, lambda grp, carry: (start_group(0, grp), carry)[1], 0)

    @pl.when(f == 0)
    def _():
        def wait_group(grp, carry):
            for r in range(share):
                row_copy(0, grp, r).wait()
            return carry
        lax.fori_loop(0, n_ff, wait_group, 0)

        def cast_group(grp, carry):
            for c0 in range(0, share, CAST_ROWS):
                dst = pl.ds(pl.multiple_of(grp * share + c0, CAST_ROWS), CAST_ROWS)
                rows16[dst, :] = rows32[grp, c0:c0 + CAST_ROWS, :].astype(BF16)
            return carry
        lax.fori_loop(0, n_ff, cast_group, 0)

    @pl.when(e + 1 < n_experts)
    def _():
        start_group(e + 1, f)

    xg = rows16[...]
    a = _dot(xg, w1_ref[...].astype(BF16))
    b = _dot(xg, w3_ref[...].astype(BF16))
    hid_ref[...] = (a * jax.nn.sigmoid(a) * b).astype(BF16)


def _moe_up(idx_table, h2, w1, w3):
    ne, m = idx_table.shape
    d = h2.shape[1]
    ff = w1.shape[2]
    tf = 256
    vmem = m * d * (4 + 2) + 2 * 2 * d * tf * 4 + 2 * d * tf * 2 + 3 * m * tf * 4 + 2 * m * tf * 2
    n_ff = ff // tf
    w_spec = pl.BlockSpec((None, d, tf), lambda e, f, idx: (e, 0, f))
    return pl.pallas_call(
        _moe_up_kernel,
        grid_spec=pltpu.PrefetchScalarGridSpec(
            num_scalar_prefetch=1,
            grid=(ne, n_ff),
            in_specs=[pl.BlockSpec(memory_space=pl.ANY), w_spec, w_spec],
            out_specs=pl.BlockSpec((None, m, tf), lambda e, f, idx: (e, 0, f)),
            scratch_shapes=[pltpu.VMEM((n_ff, m // n_ff, d), F32), pltpu.VMEM((m, d), BF16),
                            pltpu.SemaphoreType.DMA(())]),
        out_shape=jax.ShapeDtypeStruct((ne, m, ff), BF16),
        compiler_params=_params(vmem, ("arbitrary", "arbitrary")),
        name="moe_up",
    )(idx_table.reshape(ne * m), h2, w1, w3)


def _moe_down_kernel(hid_ref, w2_ref, gate_ref, y_ref):
    y = _dot(hid_ref[...], w2_ref[...].astype(BF16))
    y_ref[...] = (y * gate_ref[...]).astype(BF16)


def _moe_down(hid, w2, gate_col):
    ne, m, ff = hid.shape
    d = w2.shape[2]
    tn = 1024
    vmem = 2 * m * ff * 2 + 2 * ff * tn * 4 + ff * tn * 2 + 2 * m * 128 * 4 + 2 * m * tn * 4 + 2 * m * tn * 2
    return pl.pallas_call(
        _moe_down_kernel,
        grid=(ne, d // tn),
        in_specs=[pl.BlockSpec((None, m, ff), lambda e, j: (e, 0, 0)),
                  pl.BlockSpec((None, ff, tn), lambda e, j: (e, 0, j)),
                  pl.BlockSpec((None, m, 1), lambda e, j: (e, 0, 0))],
        out_specs=pl.BlockSpec((None, m, tn), lambda e, j: (e, 0, j)),
        out_shape=jax.ShapeDtypeStruct((ne, m, d), BF16),
        compiler_params=_params(vmem, ("arbitrary", "arbitrary")),
        name="moe_down",
    )(hid, w2, gate_col)


COMBINE_TOKENS = 256


def _moe_combine_kernel(slot_ref, y_ref, x1_ref, g2_ref, fg_ref, out_ref):
    st = slot_ref[...]
    slot_ids = lax.broadcasted_iota(I32, (COMBINE_TOKENS, CAPACITY), 1)
    onehot = jnp.concatenate(
        [jnp.where(st[:, e:e + 1] == slot_ids, 1.0, 0.0).astype(BF16) for e in range(N_EXPERTS)], axis=1)
    y = y_ref[...].reshape(N_EXPERTS * CAPACITY, y_ref.shape[-1])
    x2 = x1_ref[...] + g2_ref[...] * _dot(onehot, y)
    ms = jnp.mean(x2 * x2, axis=-1, keepdims=True)
    out_ref[...] = x2 * lax.rsqrt(ms + EPS) * fg_ref[...]


def _moe_combine(slot_ble, y, x1, g2, final_g):
    n, d = x1.shape
    tm = COMBINE_TOKENS
    tiles_per_sample = SEQ // tm
    vmem = (N_EXPERTS * CAPACITY * d * 2 + 2 * 2 * tm * d * 4 + 2 * tm * 128 * 4
            + tm * N_EXPERTS * CAPACITY * 2 + 3 * tm * d * 4)
    return pl.pallas_call(
        _moe_combine_kernel,
        grid=(BATCH, tiles_per_sample),
        in_specs=[pl.BlockSpec((None, tm, N_EXPERTS), lambda b, t: (b, t, 0)),
                  pl.BlockSpec((N_EXPERTS, None, CAPACITY, d), lambda b, t: (0, b, 0, 0),
                               pipeline_mode=pl.Buffered(1)),
                  pl.BlockSpec((tm, d), lambda b, t: (b * tiles_per_sample + t, 0)),
                  pl.BlockSpec((None, 1, d), lambda b, t: (b, 0, 0)),
                  pl.BlockSpec((1, d), lambda b, t: (0, 0))],
        out_specs=pl.BlockSpec((tm, d), lambda b, t: (b * tiles_per_sample + t, 0)),
        out_shape=jax.ShapeDtypeStruct((n, d), F32),
        compiler_params=_params(vmem, ("arbitrary", "arbitrary")),
        name="moe_combine",
    )(slot_ble, y, x1, g2, final_g)


def kernel(x, c, ctx, c_ctx, ada_w, ada_b, norm1_g, norm2_g, w_in, lb_param, hg_norm_g, pool_w, pool_scale,
           w_out, router_w, moe_w1, moe_w3, moe_w2, final_norm_g):
    nb, seq, d = x.shape
    assert (nb, seq, d) == (BATCH, SEQ, D_MODEL) and ctx.shape[1] == CTX_LEN
    assert ada_w.shape[0] == 1 and lb_param.shape[0] == 2, "single-layer block: layer 0 uses lower-bound row 0"
    x2d = x.reshape(nb * seq, d)
    ctx2d = ctx.reshape(nb * CTX_LEN, d)

    cvecs = jnp.zeros((ADA_ROWS, d), F32).at[:nb].set(c).at[nb].set(c_ctx)
    mod = _ada_mod(cvecs, ada_w[0], ada_b).reshape(ADA_ROWS, 6, d)
    sh1, sc1, g1, sh2, sc2, g2 = (mod[:nb, k][:, None, :] for k in range(6))
    csh1, csc1 = mod[nb:nb + 1, 0][:, None, :], mod[nb:nb + 1, 1][:, None, :]

    lbp = lb_param.reshape(lb_param.shape[0], 2 * HG_WIDTH)

    h_c = _norm1(ctx2d, norm1_g, csc1, csh1, rows_per_sample=nb * CTX_LEN)
    f_c, k_c, i_c = _in_proj(h_c, lbp, w_in[0], n_cols=3 * HG_WIDTH)
    s_f, s_b = _ctx_states(f_c, k_c, i_c)

    h_x = _norm1(x2d, norm1_g, sc1, sh1, rows_per_sample=seq)
    fdec, kk, rest = _in_proj(h_x, lbp, w_in[0], n_cols=w_in.shape[2])
    o = _hgrn2_scan(fdec, kk, rest, s_f, s_b, hg_norm_g)
    pm = _grid_pool(rest, pool_w[0], pool_scale)
    x1 = _out_proj(o, pm, w_out[0], x2d, g1)

    h2, aff_t = _norm2_router(x1, norm2_g, sc2, sh2, router_w[0].T)
    slot, gates, tok = _route(aff_t.reshape(nb * N_EXPERTS, seq))
    slot = slot.reshape(nb, N_EXPERTS, seq)
    h2_rows = tok.reshape(nb, N_EXPERTS, CAPACITY) + (jnp.arange(nb, dtype=I32) * seq)[:, None, None]
    hid = _moe_up(h2_rows.transpose(1, 0, 2).reshape(N_EXPERTS, nb * CAPACITY), h2, moe_w1[0], moe_w3[0])
    gate_col = gates.reshape(nb, N_EXPERTS, CAPACITY).transpose(1, 0, 2).reshape(N_EXPERTS, nb * CAPACITY, 1)
    y = _moe_down(hid, moe_w2[0], gate_col).reshape(N_EXPERTS, nb, CAPACITY, d)
    out = _moe_combine(slot.transpose(0, 2, 1), y, x1, g2, final_norm_g[None, :])
    return out.reshape(nb, seq, d).astype(x.dtype)
```
